```python
import math
import jax, jax.numpy as jnp
from jax import lax
import numpy as np

D_MODEL = 2048
BATCH = 4
SEQ = 2048
DEPTH = 1
DEC_BATCH = 128
DEC_SEQ = 8
PAST_LEN = 16384
PAGE_SIZE = 128

D_SSM = D_MODEL // 2
SSM_GROUP_CH = 16
SSM_GROUPS = D_SSM // SSM_GROUP_CH
SSM_STATE = 64
DT_MIN = 1e-3
DT_MAX = 1e-1
D_CONV = D_MODEL // 2
CONV_WIDTH = 3
N_IN = 2 * D_SSM + 4 * D_CONV + 2 * D_MODEL
SPLIT_POINTS = (D_SSM, 2 * D_SSM, 2 * D_SSM + D_CONV, 2 * D_SSM + 2 * D_CONV,
                2 * D_SSM + 3 * D_CONV, 2 * D_SSM + 4 * D_CONV,
                2 * D_SSM + 4 * D_CONV + D_MODEL)
DN_ALPHA = (2 * DEPTH) ** 0.25
DN_BETA = (8 * DEPTH) ** -0.25
LN_EPS = 1e-5

kernel_name = 's5_shortconv_gated_hybrid_step'


def _complex_scan_combine(left, right):
    a1r, a1i, b1r, b1i = left
    a2r, a2i, b2r, b2i = right
    ar = a2r * a1r - a2i * a1i
    ai = a2r * a1i + a2i * a1r
    br = a2r * b1r - a2i * b1i + b2r
    bi = a2r * b1i + a2i * b1r + b2i
    return ar, ai, br, bi


def ssm_branch(u, h0_re, h0_im, a_re, a_im, log_dt, b_re, b_im, c_re, c_im, d, w_glu):
    f32 = jnp.float32
    bt, L, _ = u.shape
    a_re = a_re.astype(f32)
    a_im = a_im.astype(f32)
    dt = jnp.exp(log_dt.astype(f32))[:, None]
    mag = jnp.exp(a_re * dt)
    ang = a_im * dt
    lam_re = mag * jnp.cos(ang)
    lam_im = mag * jnp.sin(ang)
    den = a_re * a_re + a_im * a_im
    q_re = ((lam_re - 1.0) * a_re + lam_im * a_im) / den
    q_im = (lam_im * a_re - (lam_re - 1.0) * a_im) / den
    b_re = b_re.astype(f32)
    b_im = b_im.astype(f32)
    bb_re = q_re[..., None] * b_re - q_im[..., None] * b_im
    bb_im = q_re[..., None] * b_im + q_im[..., None] * b_re
    uf = u.astype(f32)
    ug = uf.reshape(bt, L, SSM_GROUPS, SSM_GROUP_CH)
    bu_re = jnp.einsum('blgc,gpc->blgp', ug, bb_re)
    bu_im = jnp.einsum('blgc,gpc->blgp', ug, bb_im)
    lam_re_t = jnp.broadcast_to(lam_re, (1, L, SSM_GROUPS, SSM_STATE))
    lam_im_t = jnp.broadcast_to(lam_im, (1, L, SSM_GROUPS, SSM_STATE))
    pw_re, pw_im, h_re, h_im = lax.associative_scan(
        _complex_scan_combine, (lam_re_t, lam_im_t, bu_re, bu_im), axis=1)
    if h0_re is not None:
        h0r = h0_re.astype(f32)[:, None]
        h0i = h0_im.astype(f32)[:, None]
        h_re = h_re + pw_re * h0r - pw_im * h0i
        h_im = h_im + pw_re * h0i + pw_im * h0r
    y = (jnp.einsum('blgp,gcp->blgc', h_re, c_re.astype(f32))
         - jnp.einsum('blgp,gcp->blgc', h_im, c_im.astype(f32)))
    y = y.reshape(bt, L, D_SSM) + d.astype(f32) * uf
    y = jax.nn.gelu(y)
    y = y * jax.nn.sigmoid(y @ w_glu.astype(f32))
    return y.astype(u.dtype), h_re[:, -1], h_im[:, -1]


def conv_branch(b_c, c_c, h_c, buf, conv_w):
    q = c_c * h_c
    bt, L, _ = q.shape
    if buf is None:
        buf = jnp.zeros((bt, CONV_WIDTH - 1, D_CONV), q.dtype)
    padded = jnp.concatenate([buf.astype(q.dtype), q], axis=1)
    conv = padded[:, 0:L] * conv_w[0]
    for k in range(1, CONV_WIDTH):
        conv = conv + padded[:, k:k + L] * conv_w[k]
    return b_c * conv, padded[:, -(CONV_WIDTH - 1):]


def hybrid_layer(x, h0_re, h0_im, conv_buf, w_in, a_re, a_im, log_dt, b_re, b_im,
                 c_re, c_im, d, w_glu, w_out_a, conv_w, w_out_c, w_o, ln_g, ln_b):
    proj = jnp.einsum('bld,dn->bln', x, w_in)
    u_a, z_a, b_c, c_c, h_c, z_c, g_a, g_c = jnp.split(proj, SPLIT_POINTS, axis=-1)
    y_a, hr, hi = ssm_branch(u_a, h0_re, h0_im, a_re, a_im, log_dt, b_re, b_im,
                             c_re, c_im, d, w_glu)
    y_a = (y_a * jax.nn.silu(z_a)) @ w_out_a
    y_c, new_buf = conv_branch(b_c, c_c, h_c, conv_buf, conv_w)
    y_c = (y_c * jax.nn.silu(z_c)) @ w_out_c
    merged = jax.nn.sigmoid(g_a) * y_a + jax.nn.sigmoid(g_c) * y_c
    out = merged @ w_o
    r = (DN_ALPHA * x + out).astype(jnp.float32)
    mu = jnp.mean(r, axis=-1, keepdims=True)
    rc = r - mu
    var = jnp.mean(rc * rc, axis=-1, keepdims=True)
    y = rc * lax.rsqrt(var + LN_EPS) * ln_g.astype(jnp.float32) + ln_b.astype(jnp.float32)
    return y.astype(x.dtype), hr, hi, new_buf


def setup_inputs(seed: int = 0) -> dict:
    key = jax.random.key(seed)
    ks = jax.random.split(key, 24)
    f32 = jnp.float32
    nrm = lambda k, shape, s: jax.random.normal(k, shape, f32) * s
    G, P, GC = SSM_GROUPS, SSM_STATE, SSM_GROUP_CH
    x_prompt = nrm(ks[0], (BATCH, SEQ, D_MODEL), 1.0)
    x_sample = nrm(ks[1], (DEC_BATCH, DEC_SEQ, D_MODEL), 1.0)
    state_ssm_re = nrm(ks[2], (DEPTH, DEC_BATCH, G, P), 0.3)
    state_ssm_im = nrm(ks[3], (DEPTH, DEC_BATCH, G, P), 0.3)
    state_conv = nrm(ks[4], (DEPTH, DEC_BATCH, CONV_WIDTH - 1, D_CONV), 1.0)
    w_in = nrm(ks[5], (DEPTH, D_MODEL, N_IN), D_MODEL ** -0.5)
    ssm_a_re = -0.5 + nrm(ks[6], (DEPTH, G, P), 0.01)
    n_idx = jnp.arange(P, dtype=f32) * math.pi
    ssm_a_im = n_idx + nrm(ks[7], (DEPTH, G, P), 0.01)
    ssm_log_dt = jax.random.uniform(ks[8], (DEPTH, G), f32,
                                    math.log(DT_MIN), math.log(DT_MAX))
    ssm_b_re = nrm(ks[9], (DEPTH, G, P, GC), (2 * GC) ** -0.5)
    ssm_b_im = nrm(ks[10], (DEPTH, G, P, GC), (2 * GC) ** -0.5)
    ssm_c_re = nrm(ks[11], (DEPTH, G, GC, P), P ** -0.5)
    ssm_c_im = nrm(ks[12], (DEPTH, G, GC, P), P ** -0.5)
    ssm_d = nrm(ks[13], (DEPTH, D_SSM), 1.0)
    w_glu = nrm(ks[14], (DEPTH, D_SSM, D_SSM), D_SSM ** -0.5)
    w_out_a = nrm(ks[15], (DEPTH, D_SSM, D_MODEL), DN_BETA * D_SSM ** -0.5)
    conv_w = nrm(ks[16], (DEPTH, CONV_WIDTH, D_CONV), CONV_WIDTH ** -0.5)
    w_out_c = nrm(ks[17], (DEPTH, D_CONV, D_MODEL), DN_BETA * D_CONV ** -0.5)
    w_o = nrm(ks[18], (DEPTH, D_MODEL, D_MODEL), DN_BETA * D_MODEL ** -0.5)
    ln_g = 1.0 + nrm(ks[19], (DEPTH, D_MODEL), 0.01)
    ln_b = nrm(ks[20], (DEPTH, D_MODEL), 0.01)
    return {'x_prompt': x_prompt, 'x_sample': x_sample,
            'state_ssm_re': state_ssm_re, 'state_ssm_im': state_ssm_im,
            'state_conv': state_conv, 'w_in': w_in,
            'ssm_a_re': ssm_a_re, 'ssm_a_im': ssm_a_im, 'ssm_log_dt': ssm_log_dt,
            'ssm_b_re': ssm_b_re, 'ssm_b_im': ssm_b_im,
            'ssm_c_re': ssm_c_re, 'ssm_c_im': ssm_c_im, 'ssm_d': ssm_d,
            'w_glu': w_glu, 'w_out_a': w_out_a, 'conv_w': conv_w,
            'w_out_c': w_out_c, 'w_o': w_o, 'ln_g': ln_g, 'ln_b': ln_b}


def reference(x_prompt, x_sample, state_ssm_re, state_ssm_im, state_conv, w_in,
              ssm_a_re, ssm_a_im, ssm_log_dt, ssm_b_re, ssm_b_im, ssm_c_re, ssm_c_im,
              ssm_d, w_glu, w_out_a, conv_w, w_out_c, w_o, ln_g, ln_b):
    xp = x_prompt
    xs = x_sample
    sre_p, sim_p, cnv_p = [], [], []
    sre_s, sim_s, cnv_s = [], [], []
    for l in range(DEPTH):
        params = (w_in[l], ssm_a_re[l], ssm_a_im[l], ssm_log_dt[l], ssm_b_re[l],
                  ssm_b_im[l], ssm_c_re[l], ssm_c_im[l], ssm_d[l], w_glu[l],
                  w_out_a[l], conv_w[l], w_out_c[l], w_o[l], ln_g[l], ln_b[l])
        xp, hr, hi, cb = hybrid_layer(xp, None, None, None, *params)
        sre_p.append(hr)
        sim_p.append(hi)
        cnv_p.append(cb)
        xs, hr, hi, cb = hybrid_layer(xs, state_ssm_re[l], state_ssm_im[l],
                                      state_conv[l], *params)
        sre_s.append(hr)
        sim_s.append(hi)
        cnv_s.append(cb)
    new_ssm_re_prompt = jnp.stack(sre_p, axis=0)
    new_ssm_im_prompt = jnp.stack(sim_p, axis=0)
    new_conv_prompt = jnp.stack(cnv_p, axis=0)
    new_ssm_re_sample = jnp.stack(sre_s, axis=0)
    new_ssm_im_sample = jnp.stack(sim_s, axis=0)
    new_conv_sample = jnp.stack(cnv_s, axis=0)
    return (xp, xs, new_ssm_re_prompt, new_ssm_im_prompt, new_conv_prompt,
            new_ssm_re_sample, new_ssm_im_sample, new_conv_sample)
```

```python
import functools
import math

import jax
import jax.numpy as jnp
from jax import lax
from jax.experimental import pallas as pl
from jax.experimental.pallas import tpu as pltpu

F32 = jnp.float32
BF16 = jnp.bfloat16

SUBLANES = 8
VMEM_LIMIT_BYTES = 56 * 1024 * 1024

TILE_M = 256
SSM_BUNDLE_CH = 256
CONV_PAD = SUBLANES

LN_EPS = 1e-5
GELU_C = math.sqrt(2.0 / math.pi)


def _sigmoid(x):
    return 0.5 * jnp.tanh(0.5 * x) + 0.5


def _silu(x):
    return x * _sigmoid(x)


def _gelu_tanh(x):
    return 0.5 * x * (1.0 + jnp.tanh(GELU_C * (x + 0.044715 * (x * x * x))))


def _const_spec(shape):
    nd = len(shape)
    return pl.BlockSpec(shape, lambda i: (0,) * nd, pipeline_mode=pl.Buffered(1))


def _proj_kernel(xp_ref, xs_ref, w0_ref, w1_ref, *rest, n_prompt_tiles, epilogue, n_out):
    out_refs = rest[:n_out]
    wbf_ref = rest[n_out]
    i = pl.program_id(0)

    @pl.when(i == 0)
    def _():
        wbf_ref[0] = w0_ref[...].astype(BF16)
        wbf_ref[1] = w1_ref[...].astype(BF16)

    def compute(x_ref):
        xb = x_ref[...].astype(BF16)
        a = jnp.dot(xb, wbf_ref[0], preferred_element_type=F32)
        b = jnp.dot(xb, wbf_ref[1], preferred_element_type=F32)
        epilogue(a, b, out_refs)

    @pl.when(i < n_prompt_tiles)
    def _():
        compute(xp_ref)

    @pl.when(i >= n_prompt_tiles)
    def _():
        compute(xs_ref)


def _epi_ssm_in(a, b, out_refs):
    out_refs[0][...] = a.astype(BF16)
    out_refs[1][...] = _silu(b).astype(BF16)


def _epi_conv_gate(a, b, out_refs):
    out_refs[0][...] = (a * _silu(b)).astype(BF16)


def _epi_conv_in(a, b, out_refs):
    out_refs[0][...] = a * b


def _epi_merge_gate(a, b, out_refs):
    half = a.shape[1]
    out_refs[0][:, :half] = _sigmoid(a).astype(BF16)
    out_refs[0][:, half:] = _sigmoid(b).astype(BF16)


def _proj_call(xp, xs, w_in, col_blocks, epilogue, out_widths, out_dtypes, name):
    m_p, d = xp.shape
    m_s = xs.shape[0]
    n_p, n_s = m_p // TILE_M, m_s // TILE_M
    wcol = 1024
    c0, c1 = col_blocks
    kernel = functools.partial(_proj_kernel, n_prompt_tiles=n_p, epilogue=epilogue,
                               n_out=len(out_widths))
    return pl.pallas_call(
        kernel,
        grid=(n_p + n_s,),
        in_specs=[
            pl.BlockSpec((TILE_M, d), lambda i: (jnp.minimum(i, n_p - 1), 0)),
            pl.BlockSpec((TILE_M, d), lambda i: (jnp.maximum(i - n_p, 0), 0)),
            pl.BlockSpec((d, wcol), lambda i: (0, c0), pipeline_mode=pl.Buffered(1)),
            pl.BlockSpec((d, wcol), lambda i: (0, c1), pipeline_mode=pl.Buffered(1)),
        ],
        out_specs=[pl.BlockSpec((TILE_M, w), lambda i: (i, 0)) for w in out_widths],
        out_shape=[jax.ShapeDtypeStruct((m_p + m_s, w), dt)
                   for w, dt in zip(out_widths, out_dtypes)],
        scratch_shapes=[pltpu.VMEM((2, d, wcol), BF16)],
        compiler_params=pltpu.CompilerParams(
            dimension_semantics=("arbitrary",), vmem_limit_bytes=VMEM_LIMIT_BYTES),
        name=name,
    )(xp, xs, w_in, w_in)


def _scan_block(bu_ref, tab_ref, k, lane0, width, carry_re, carry_im):
    rows = pl.ds(pl.multiple_of(k * SUBLANES, SUBLANES), SUBLANES)
    xr = bu_ref[rows, 0:width]
    xi = bu_ref[rows, width:2 * width]
    lanes = slice(lane0, lane0 + width)
    for step, shift in enumerate((1, 2, 4)):
        lr = tab_ref[2 * step, :, lanes]
        li = tab_ref[2 * step + 1, :, lanes]
        sr = pltpu.roll(xr, shift, 0)
        si = pltpu.roll(xi, shift, 0)
        xr, xi = xr + (lr * sr - li * si), xi + (lr * si + li * sr)
    pr = tab_ref[6, :, lanes]
    pi = tab_ref[7, :, lanes]
    cr = jnp.broadcast_to(carry_re, xr.shape)
    ci = jnp.broadcast_to(carry_im, xi.shape)
    xr, xi = xr + (pr * cr - pi * ci), xi + (pr * ci + pi * cr)
    bu_ref[rows, 0:width] = xr
    bu_ref[rows, width:2 * width] = xi
    return xr, xi


def _ssm_kernel(u_ref, sza_ref, h0re_ref, h0im_ref, tab_ref, d_ref, bbig_ref, cbig_ref,
                wglu_ref, wouta_ref,
                ya_ref, spre_ref, spim_ref, ssre_ref, ssim_ref,
                bu_ref, y_ref, carry_ref,
                *, n_prompt_tiles, tiles_per_seq):
    i = pl.program_id(0)
    n_bundles = bbig_ref.shape[0]
    width = bbig_ref.shape[2] // 2
    n_blocks = TILE_M // SUBLANES

    is_prompt = i < n_prompt_tiles
    t_in_seq = i % tiles_per_seq

    @pl.when(jnp.logical_and(is_prompt, t_in_seq == 0))
    def _():
        carry_ref[...] = jnp.zeros_like(carry_ref)

    for b in range(n_bundles):
        ch = slice(b * SSM_BUNDLE_CH, (b + 1) * SSM_BUNDLE_CH)
        lane0 = b * width
        lanes = slice(lane0, lane0 + width)
        bu_ref[...] = jnp.dot(u_ref[:, ch], bbig_ref[b], preferred_element_type=F32)

        @pl.when(is_prompt)
        def _():
            def body(k, c):
                xr, xi = _scan_block(bu_ref, tab_ref, k, lane0, width, c[0], c[1])
                return xr[SUBLANES - 1:SUBLANES, :], xi[SUBLANES - 1:SUBLANES, :]

            cr, ci = lax.fori_loop(0, n_blocks, body,
                                   (carry_ref[0:1, lanes], carry_ref[1:2, lanes]))
            carry_ref[0:1, lanes] = cr
            carry_ref[1:2, lanes] = ci

        @pl.when(jnp.logical_not(is_prompt))
        def _():
            def body(k, c):
                row = pl.ds(k, 1)
                xr, xi = _scan_block(bu_ref, tab_ref, k, lane0, width,
                                     h0re_ref[row, lanes], h0im_ref[row, lanes])
                ssre_ref[row, lanes] = xr[SUBLANES - 1:SUBLANES, :]
                ssim_ref[row, lanes] = xi[SUBLANES - 1:SUBLANES, :]
                return c

            lax.fori_loop(0, n_blocks, body, 0)

        y_ref[:, ch] = jnp.dot(bu_ref[...].astype(BF16), cbig_ref[b],
                               preferred_element_type=F32)

    @pl.when(jnp.logical_and(is_prompt, t_in_seq == tiles_per_seq - 1))
    def _():
        row = pl.ds(i // tiles_per_seq, 1)
        spre_ref[row, :] = carry_ref[0:1, :]
        spim_ref[row, :] = carry_ref[1:2, :]

    y = y_ref[...] + d_ref[...] * u_ref[...].astype(F32)
    g = _gelu_tanh(y)
    z = jnp.dot(g.astype(BF16), wglu_ref[...], preferred_element_type=F32)
    o = g * _sigmoid(z) * sza_ref[...].astype(F32)
    ya = jnp.dot(o.astype(BF16), wouta_ref[...], preferred_element_type=F32)
    ya_ref[...] = ya.astype(BF16)


def _ssm_call(u, sza, h0re, h0im, tab, d, bbig, cbig, w_glu, w_out_a, n_prompt_rows,
              seq_len):
    m, d_ssm = u.shape
    d_model = w_out_a.shape[1]
    n_state = tab.shape[2]
    n_tiles = m // TILE_M
    n_p = n_prompt_rows // TILE_M
    n_batch = n_prompt_rows // seq_len
    seqs_per_tile = TILE_M // SUBLANES
    n_sample_seq = h0re.shape[0]
    kernel = functools.partial(_ssm_kernel, n_prompt_tiles=n_p,
                               tiles_per_seq=seq_len // TILE_M)
    sample_map = lambda i: (jnp.maximum(i - n_p, 0), 0)
    return pl.pallas_call(
        kernel,
        grid=(n_tiles,),
        in_specs=[
            pl.BlockSpec((TILE_M, d_ssm), lambda i: (i, 0)),
            pl.BlockSpec((TILE_M, d_ssm), lambda i: (i, 0)),
            pl.BlockSpec((seqs_per_tile, n_state), sample_map),
            pl.BlockSpec((seqs_per_tile, n_state), sample_map),
            _const_spec(tab.shape),
            _const_spec(d.shape),
            _const_spec(bbig.shape),
            _const_spec(cbig.shape),
            _const_spec(w_glu.shape),
            _const_spec(w_out_a.shape),
        ],
        out_specs=[
            pl.BlockSpec((TILE_M, d_model), lambda i: (i, 0)),
            pl.BlockSpec((n_batch, n_state), lambda i: (0, 0)),
            pl.BlockSpec((n_batch, n_state), lambda i: (0, 0)),
            pl.BlockSpec((seqs_per_tile, n_state), sample_map),
            pl.BlockSpec((seqs_per_tile, n_state), sample_map),
        ],
        out_shape=[
            jax.ShapeDtypeStruct((m, d_model), BF16),
            jax.ShapeDtypeStruct((n_batch, n_state), F32),
            jax.ShapeDtypeStruct((n_batch, n_state), F32),
            jax.ShapeDtypeStruct((n_sample_seq, n_state), F32),
            jax.ShapeDtypeStruct((n_sample_seq, n_state), F32),
        ],
        scratch_shapes=[
            pltpu.VMEM((TILE_M, bbig.shape[2]), F32),
            pltpu.VMEM((TILE_M, d_ssm), F32),
            pltpu.VMEM((2, n_state), F32),
        ],
        compiler_params=pltpu.CompilerParams(
            dimension_semantics=("arbitrary",), vmem_limit_bytes=VMEM_LIMIT_BYTES),
        name="ssm_branch",
    )(u, sza, h0re, h0im, tab, d, bbig, cbig, w_glu, w_out_a)


def _tail_kernel(q_ref, bz_ref, ya_ref, sga_ref, sgc_ref, xp_ref, xs_ref, e0_ref, e1_ref,
                 cw_ref, woutc_ref, wo_ref, lng_ref, lnb_ref,
                 yp_ref, ys_ref,
                 pad_ref,
                 *, n_prompt_tiles, tiles_per_seq, alpha):
    i = pl.program_id(0)
    is_prompt = i < n_prompt_tiles

    @pl.when(jnp.logical_and(is_prompt, i % tiles_per_seq == 0))
    def _():
        pad_ref[0:CONV_PAD, :] = jnp.zeros((CONV_PAD, pad_ref.shape[1]), F32)

    q = q_ref[...]
    pad_ref[CONV_PAD:CONV_PAD + TILE_M, :] = q
    w0 = cw_ref[0:1, :]
    w1 = cw_ref[1:2, :]
    w2 = cw_ref[2:3, :]

    def finish(q1, q2, x_ref, y_ref):
        conv = w0 * q2 + w1 * q1 + w2 * q
        yc_in = conv * bz_ref[...].astype(F32)
        yc = jnp.dot(yc_in.astype(BF16), woutc_ref[...], preferred_element_type=F32)
        merged = (sga_ref[...].astype(F32) * ya_ref[...].astype(F32)
                  + sgc_ref[...].astype(F32) * yc)
        out = jnp.dot(merged.astype(BF16), wo_ref[...], preferred_element_type=F32)
        r = alpha * x_ref[...] + out
        mu = jnp.mean(r, axis=-1, keepdims=True)
        rc = r - mu
        var = jnp.mean(rc * rc, axis=-1, keepdims=True)
        y_ref[...] = rc * lax.rsqrt(var + LN_EPS) * lng_ref[...] + lnb_ref[...]

    @pl.when(is_prompt)
    def _():
        q1 = pad_ref[CONV_PAD - 1:CONV_PAD - 1 + TILE_M, :]
        q2 = pad_ref[CONV_PAD - 2:CONV_PAD - 2 + TILE_M, :]
        finish(q1, q2, xp_ref, yp_ref)
        pad_ref[0:CONV_PAD, :] = pad_ref[TILE_M:TILE_M + CONV_PAD, :]

    @pl.when(jnp.logical_not(is_prompt))
    def _():
        t = lax.broadcasted_iota(jnp.int32, q.shape, 0) % SUBLANES
        e0 = e0_ref[...]
        e1 = e1_ref[...]
        q1 = jnp.where(t == 0, e1, pad_ref[CONV_PAD - 1:CONV_PAD - 1 + TILE_M, :])
        q2 = jnp.where(t == 0, e0,
                       jnp.where(t == 1, e1, pad_ref[CONV_PAD - 2:CONV_PAD - 2 + TILE_M, :]))
        finish(q1, q2, xs_ref, ys_ref)


def _tail_call(q, bz, ya, sga, sgc, xp, xs, e0, e1, conv_w, w_out_c, w_o, ln_g, ln_b,
               seq_len, alpha):
    m, d_conv = q.shape
    m_p, d_model = xp.shape
    m_s = xs.shape[0]
    n_p = m_p // TILE_M
    kernel = functools.partial(_tail_kernel, n_prompt_tiles=n_p,
                               tiles_per_seq=seq_len // TILE_M, alpha=alpha)
    prompt_map = lambda i: (jnp.minimum(i, n_p - 1), 0)
    sample_map = lambda i: (jnp.maximum(i - n_p, 0), 0)
    row_map = lambda i: (i, 0)
    return pl.pallas_call(
        kernel,
        grid=(m // TILE_M,),
        in_specs=[
            pl.BlockSpec((TILE_M, d_conv), row_map),
            pl.BlockSpec((TILE_M, d_conv), row_map),
            pl.BlockSpec((TILE_M, d_model), row_map),
            pl.BlockSpec((TILE_M, d_model), row_map),
            pl.BlockSpec((TILE_M, d_model), row_map),
            pl.BlockSpec((TILE_M, d_model), prompt_map),
            pl.BlockSpec((TILE_M, d_model), sample_map),
            pl.BlockSpec((TILE_M, d_conv), sample_map),
            pl.BlockSpec((TILE_M, d_conv), sample_map),
            _const_spec(conv_w.shape),
            _const_spec(w_out_c.shape),
            _const_spec(w_o.shape),
            _const_spec(ln_g.shape),
            _const_spec(ln_b.shape),
        ],
        out_specs=[
            pl.BlockSpec((TILE_M, d_model), prompt_map),
            pl.BlockSpec((TILE_M, d_model), sample_map),
        ],
        out_shape=[
            jax.ShapeDtypeStruct((m_p, d_model), F32),
            jax.ShapeDtypeStruct((m_s, d_model), F32),
        ],
        scratch_shapes=[
            pltpu.VMEM((TILE_M + CONV_PAD, d_conv), F32),
        ],
        compiler_params=pltpu.CompilerParams(
            dimension_semantics=("arbitrary",), vmem_limit_bytes=VMEM_LIMIT_BYTES),
        name="tail",
    )(q, bz, ya, sga, sgc, xp, xs, e0, e1, conv_w, w_out_c, w_o, ln_g, ln_b)


def _ssm_params(a_re, a_im, log_dt, b_re, b_im, c_re, c_im):
    g, p, gc = b_re.shape
    groups_per_bundle = SSM_BUNDLE_CH // gc
    n_bundles = g // groups_per_bundle
    dt = jnp.exp(log_dt)[:, None]
    mag = jnp.exp(a_re * dt)
    ang = a_im * dt
    lam_re = mag * jnp.cos(ang)
    lam_im = mag * jnp.sin(ang)
    den = a_re * a_re + a_im * a_im
    q_re = ((lam_re - 1.0) * a_re + lam_im * a_im) / den
    q_im = (lam_im * a_re - (lam_re - 1.0) * a_im) / den
    bb_re = q_re[..., None] * b_re - q_im[..., None] * b_im
    bb_im = q_re[..., None] * b_im + q_im[..., None] * b_re

    eye = jnp.eye(groups_per_bundle, dtype=F32)

    def b_blockdiag(bb):
        blk = bb.reshape(n_bundles, groups_per_bundle, p, gc).transpose(0, 1, 3, 2)
        dense = jnp.einsum('bgcp,gh->bgchp', blk, eye)
        return dense.reshape(n_bundles, groups_per_bundle * gc, groups_per_bundle * p)

    def c_blockdiag(cc):
        blk = cc.reshape(n_bundles, groups_per_bundle, gc, p).transpose(0, 1, 3, 2)
        dense = jnp.einsum('bgpc,gh->bgphc', blk, eye)
        return dense.reshape(n_bundles, groups_per_bundle * p, groups_per_bundle * gc)

    bbig = jnp.concatenate([b_blockdiag(bb_re), b_blockdiag(bb_im)], axis=2).astype(BF16)
    cbig = jnp.concatenate([c_blockdiag(c_re), -c_blockdiag(c_im)], axis=1).astype(BF16)

    lr, li = lam_re.reshape(-1), lam_im.reshape(-1)
    pw_re, pw_im = [lr], [li]
    for _ in range(SUBLANES - 1):
        pr, pi = pw_re[-1], pw_im[-1]
        pw_re.append(pr * lr - pi * li)
        pw_im.append(pr * li + pi * lr)
    pw_re, pw_im = jnp.stack(pw_re), jnp.stack(pw_im)
    row = jnp.arange(SUBLANES)[:, None]
    tabs = []
    for shift in (1, 2, 4):
        tabs.append(jnp.where(row >= shift, pw_re[shift - 1][None, :], 0.0))
        tabs.append(jnp.where(row >= shift, pw_im[shift - 1][None, :], 0.0))
    tabs += [pw_re, pw_im]
    return bbig, cbig, jnp.stack(tabs).astype(F32)


def kernel(x_prompt, x_sample, state_ssm_re, state_ssm_im, state_conv, w_in, ssm_a_re, ssm_a_im, ssm_log_dt, ssm_b_re, ssm_b_im, ssm_c_re, ssm_c_im, ssm_d, w_glu, w_out_a, conv_w, w_out_c, w_o, ln_g, ln_b):
    depth = w_in.shape[0]
    assert depth == 1, "single-layer trunk"
    batch, seq, d_model = x_prompt.shape
    dec_batch, dec_seq, _ = x_sample.shape
    assert dec_seq == SUBLANES and seq % TILE_M == 0 and (dec_batch * dec_seq) % TILE_M == 0
    g, p, gc = ssm_b_re.shape[1:]
    d_ssm = g * gc
    d_conv = conv_w.shape[2]
    n_state = g * p
    alpha = (2 * depth) ** 0.25

    xp = x_prompt.reshape(batch * seq, d_model)
    xs = x_sample.reshape(dec_batch * dec_seq, d_model)
    m_p = xp.shape[0]
    w = w_in[0]

    u, sza = _proj_call(xp, xs, w, (0, 1), _epi_ssm_in, (d_ssm, d_ssm), (BF16, BF16), "proj_ssm")
    (bz,) = _proj_call(xp, xs, w, (2, 5), _epi_conv_gate, (d_conv,), (BF16,), "proj_conv_gate")
    (q,) = _proj_call(xp, xs, w, (3, 4), _epi_conv_in, (d_conv,), (F32,), "proj_conv_in")
    (sga,) = _proj_call(xp, xs, w, (6, 7), _epi_merge_gate, (d_model,), (BF16,), "proj_gate_a")
    (sgc,) = _proj_call(xp, xs, w, (8, 9), _epi_merge_gate, (d_model,), (BF16,), "proj_gate_c")

    bbig, cbig, tab = _ssm_params(ssm_a_re[0], ssm_a_im[0], ssm_log_dt[0], ssm_b_re[0],
                                  ssm_b_im[0], ssm_c_re[0], ssm_c_im[0])
    h0re = state_ssm_re[0].reshape(dec_batch, n_state)
    h0im = state_ssm_im[0].reshape(dec_batch, n_state)
    ya, spre, spim, ssre, ssim = _ssm_call(
        u, sza, h0re, h0im, tab, ssm_d[0][None, :], bbig, cbig,
        w_glu[0].astype(BF16), w_out_a[0].astype(BF16), m_p, seq)

    e0 = jnp.repeat(state_conv[0, :, 0, :], dec_seq, axis=0)
    e1 = jnp.repeat(state_conv[0, :, 1, :], dec_seq, axis=0)
    yp, ys = _tail_call(q, bz, ya, sga, sgc, xp, xs, e0, e1, conv_w[0],
                        w_out_c[0].astype(BF16), w_o[0].astype(BF16),
                        ln_g[0][None, :], ln_b[0][None, :], seq, alpha)

    q_p = q[:m_p].reshape(batch, seq, d_conv)
    q_s = q[m_p:].reshape(dec_batch, dec_seq, d_conv)
    return (yp.reshape(batch, seq, d_model),
            ys.reshape(dec_batch, dec_seq, d_model),
            spre.reshape(1, batch, g, p),
            spim.reshape(1, batch, g, p),
            q_p[:, seq - 2:, :][None],
            ssre.reshape(1, dec_batch, g, p),
            ssim.reshape(1, dec_batch, g, p),
            q_s[:, dec_seq - 2:, :][None])
```

```python
import functools
import math

import jax
import jax.numpy as jnp
from jax import lax
from jax.experimental import pallas as pl
from jax.experimental.pallas import tpu as pltpu

F32 = jnp.float32
BF16 = jnp.bfloat16

SUBLANES = 8
LANES = 128
VMEM_LIMIT_BYTES = 56 * 1024 * 1024

TILE_M = 256
SSM_BUNDLE_CH = 256
SCAN_CHUNK = 4 * LANES
CONV_PAD = SUBLANES

LN_EPS = 1e-5
GELU_C = math.sqrt(2.0 / math.pi)


def _sigmoid(x):
    return 0.5 * jnp.tanh(0.5 * x) + 0.5


def _silu(x):
    return x * _sigmoid(x)


def _gelu_tanh(x):
    return 0.5 * x * (1.0 + jnp.tanh(GELU_C * (x + 0.044715 * (x * x * x))))


def _const_spec(shape):
    nd = len(shape)
    return pl.BlockSpec(shape, lambda i: (0,) * nd, pipeline_mode=pl.Buffered(1))


def _proj_kernel(xp_ref, xs_ref, w0_ref, w1_ref, *rest, n_prompt_tiles, epilogue, n_out):
    out_refs = rest[:n_out]
    wbf_ref = rest[n_out]
    i = pl.program_id(0)

    @pl.when(i == 0)
    def _():
        wbf_ref[0] = w0_ref[...].astype(BF16)
        wbf_ref[1] = w1_ref[...].astype(BF16)

    def compute(x_ref):
        xb = x_ref[...].astype(BF16)
        a = jnp.dot(xb, wbf_ref[0], preferred_element_type=F32)
        b = jnp.dot(xb, wbf_ref[1], preferred_element_type=F32)
        epilogue(a, b, out_refs)

    @pl.when(i < n_prompt_tiles)
    def _():
        compute(xp_ref)

    @pl.when(i >= n_prompt_tiles)
    def _():
        compute(xs_ref)


def _epi_ssm_in(a, b, out_refs):
    out_refs[0][...] = a.astype(BF16)
    out_refs[1][...] = _silu(b).astype(BF16)


def _epi_conv_gate(a, b, out_refs):
    out_refs[0][...] = (a * _silu(b)).astype(BF16)


def _epi_conv_in(a, b, out_refs):
    out_refs[0][...] = a * b


def _epi_merge_gate(a, b, out_refs):
    half = a.shape[1]
    out_refs[0][:, :half] = _sigmoid(a).astype(BF16)
    out_refs[0][:, half:] = _sigmoid(b).astype(BF16)


def _proj_call(xp, xs, w_in, col_blocks, epilogue, out_widths, out_dtypes, name):
    m_p, d = xp.shape
    m_s = xs.shape[0]
    n_p, n_s = m_p // TILE_M, m_s // TILE_M
    wcol = 1024
    c0, c1 = col_blocks
    kernel = functools.partial(_proj_kernel, n_prompt_tiles=n_p, epilogue=epilogue,
                               n_out=len(out_widths))
    return pl.pallas_call(
        kernel,
        grid=(n_p + n_s,),
        in_specs=[
            pl.BlockSpec((TILE_M, d), lambda i: (jnp.minimum(i, n_p - 1), 0)),
            pl.BlockSpec((TILE_M, d), lambda i: (jnp.maximum(i - n_p, 0), 0)),
            pl.BlockSpec((d, wcol), lambda i: (0, c0), pipeline_mode=pl.Buffered(1)),
            pl.BlockSpec((d, wcol), lambda i: (0, c1), pipeline_mode=pl.Buffered(1)),
        ],
        out_specs=[pl.BlockSpec((TILE_M, w), lambda i: (i, 0)) for w in out_widths],
        out_shape=[jax.ShapeDtypeStruct((m_p + m_s, w), dt)
                   for w, dt in zip(out_widths, out_dtypes)],
        scratch_shapes=[pltpu.VMEM((2, d, wcol), BF16)],
        compiler_params=pltpu.CompilerParams(
            dimension_semantics=("arbitrary",), vmem_limit_bytes=VMEM_LIMIT_BYTES),
        name=name,
    )(xp, xs, w_in, w_in)


def _pitch(steps):
    return steps if (steps // SUBLANES) % 2 == 1 else steps + SUBLANES


def _to_stepmajor(val, slab_ref, n_groups, steps):
    pitch = _pitch(steps)
    n_slabs = val.shape[1] // LANES
    n_sub = n_groups * SUBLANES
    for j in range(n_slabs):
        lanes = slice(j * LANES, (j + 1) * LANES)
        if pitch == steps:
            slab_ref[j, 0:n_sub * steps, :] = val[:, lanes]
        else:
            for s in range(n_sub):
                slab_ref[j, s * pitch:s * pitch + steps, :] = val[s * steps:(s + 1) * steps, lanes]
    blocks = []
    for g in range(n_groups):
        for k in range(steps):
            rows = pl.ds(g * SUBLANES * pitch + k, SUBLANES, stride=pitch)
            blocks.append(jnp.concatenate([slab_ref[j, rows, :] for j in range(n_slabs)], axis=1))
    return jnp.concatenate(blocks, axis=0)


def _store_stepmajor(val, slab_ref, slab0, n_groups, steps):
    pitch = _pitch(steps)
    for g in range(n_groups):
        for k in range(steps):
            r0 = (g * steps + k) * SUBLANES
            rows = pl.ds(g * SUBLANES * pitch + k, SUBLANES, stride=pitch)
            for j in range(val.shape[1] // LANES):
                slab_ref[slab0 + j, rows, :] = val[r0:r0 + SUBLANES, j * LANES:(j + 1) * LANES]


def _load_natural(slab_ref, n_groups, steps):
    pitch = _pitch(steps)
    n_slabs = slab_ref.shape[0]
    n_sub = n_groups * SUBLANES
    if pitch == steps:
        return jnp.concatenate([slab_ref[j, 0:n_sub * steps, :] for j in range(n_slabs)], axis=1)
    return jnp.concatenate(
        [jnp.concatenate([slab_ref[j, s * pitch:s * pitch + steps, :] for j in range(n_slabs)],
                         axis=1) for s in range(n_sub)], axis=0)


def _cmuladd(ar, ai, br, bi, cr, ci):
    return ar * br - ai * bi + cr, ar * bi + ai * br + ci


def _scan_group(bu_ref, b, row0, steps, width, lane0, tabl_ref, enter_fn, leave_fn):
    for c in range(width // SCAN_CHUNK):
        re = slice(c * SCAN_CHUNK, (c + 1) * SCAN_CHUNK)
        im = slice(width + c * SCAN_CHUNK, width + (c + 1) * SCAN_CHUNK)
        tl = slice(lane0 + c * SCAN_CHUNK, lane0 + (c + 1) * SCAN_CHUNK)

        def power(k):
            shape = (SUBLANES, SCAN_CHUNK)
            return (jnp.broadcast_to(tabl_ref[0, k:k + 1, tl], shape),
                    jnp.broadcast_to(tabl_ref[1, k:k + 1, tl], shape))

        lr, li = power(0)
        pr = pi = None
        for k in range(steps):
            rows = slice(row0 + k * SUBLANES, row0 + (k + 1) * SUBLANES)
            xr, xi = bu_ref[b, rows, re], bu_ref[b, rows, im]
            if k == 0:
                pr, pi = xr, xi
            else:
                pr, pi = _cmuladd(lr, li, pr, pi, xr, xi)
                bu_ref[b, rows, re] = pr
                bu_ref[b, rows, im] = pi
        cr, ci = enter_fn(pr, pi, tl)
        hr = hi = None
        for k in range(steps):
            rows = slice(row0 + k * SUBLANES, row0 + (k + 1) * SUBLANES)
            wr, wi = power(k)
            hr, hi = _cmuladd(wr, wi, cr, ci, bu_ref[b, rows, re], bu_ref[b, rows, im])
            bu_ref[b, rows, re] = hr
            bu_ref[b, rows, im] = hi
        leave_fn(hr, hi, tl)


def _ssm_kernel(u_ref, sza_ref, h0re_ref, h0im_ref, tabl_ref, tabq_ref, d_ref, bbig_ref, cbig_ref,
                wglu_ref, wouta_ref,
                ya_ref, spre_ref, spim_ref, ssre_ref, ssim_ref,
                bu_ref, uslab_ref, yslab_ref, carry_ref,
                *, n_prompt_tiles, tiles_per_seq, sample_steps):
    i = pl.program_id(0)
    n_bundles = bbig_ref.shape[0]
    width = bbig_ref.shape[2] // 2
    is_prompt = i < n_prompt_tiles
    t_in_seq = i % tiles_per_seq

    @pl.when(jnp.logical_and(is_prompt, t_in_seq == 0))
    def _():
        carry_ref[...] = jnp.zeros_like(carry_ref)

    def run(n_groups, steps, enter_factory, leave_factory):
        u_sm = _to_stepmajor(u_ref[...].astype(F32), uslab_ref, n_groups, steps).astype(BF16)
        for b in range(n_bundles):
            ch = slice(b * SSM_BUNDLE_CH, (b + 1) * SSM_BUNDLE_CH)
            bu_ref[b] = jnp.dot(u_sm[:, ch], bbig_ref[b], preferred_element_type=F32)
            for g in range(n_groups):
                _scan_group(bu_ref, b, g * SUBLANES * steps, steps, width, b * width, tabl_ref,
                            enter_factory(g), leave_factory(g))
            y_b = jnp.dot(bu_ref[b].astype(BF16), cbig_ref[b], preferred_element_type=F32)
            _store_stepmajor(y_b, yslab_ref, b * (SSM_BUNDLE_CH // LANES), n_groups, steps)
        y = _load_natural(yslab_ref, n_groups, steps) + d_ref[...] * u_ref[...].astype(F32)
        g_act = _gelu_tanh(y)
        z = jnp.dot(g_act.astype(BF16), wglu_ref[...], preferred_element_type=F32)
        o = g_act * _sigmoid(z) * sza_ref[...].astype(F32)
        ya = jnp.dot(o.astype(BF16), wouta_ref[...], preferred_element_type=F32)
        ya_ref[...] = ya.astype(BF16)

    def prompt_enter(_g):
        def enter(er, ei, tl):
            xr, xi = er, ei
            for step, shift in enumerate((1, 2, 4)):
                xr, xi = _cmuladd(tabq_ref[2 * step, :, tl], tabq_ref[2 * step + 1, :, tl],
                                  pltpu.roll(xr, shift, 0), pltpu.roll(xi, shift, 0), xr, xi)
            c0r = jnp.broadcast_to(carry_ref[0:1, tl], xr.shape)
            c0i = jnp.broadcast_to(carry_ref[1:2, tl], xi.shape)
            xr, xi = _cmuladd(tabq_ref[6, :, tl], tabq_ref[7, :, tl], c0r, c0i, xr, xi)
            carry_ref[0:1, tl] = xr[SUBLANES - 1:SUBLANES, :]
            carry_ref[1:2, tl] = xi[SUBLANES - 1:SUBLANES, :]
            first = lax.broadcasted_iota(jnp.int32, xr.shape, 0) == 0
            return (jnp.where(first, c0r, pltpu.roll(xr, 1, 0)),
                    jnp.where(first, c0i, pltpu.roll(xi, 1, 0)))
        return enter

    def prompt_leave(_g):
        return lambda hr, hi, tl: None

    def sample_enter(g):
        rows = slice(g * SUBLANES, (g + 1) * SUBLANES)
        return lambda er, ei, tl: (h0re_ref[rows, tl], h0im_ref[rows, tl])

    def sample_leave(g):
        rows = slice(g * SUBLANES, (g + 1) * SUBLANES)

        def leave(hr, hi, tl):
            ssre_ref[rows, tl] = hr
            ssim_ref[rows, tl] = hi
        return leave

    @pl.when(is_prompt)
    def _():
        run(1, TILE_M // SUBLANES, prompt_enter, prompt_leave)

    @pl.when(jnp.logical_not(is_prompt))
    def _():
        run(TILE_M // (SUBLANES * sample_steps), sample_steps, sample_enter, sample_leave)

    @pl.when(jnp.logical_and(is_prompt, t_in_seq == tiles_per_seq - 1))
    def _():
        row = pl.ds(i // tiles_per_seq, 1)
        spre_ref[row, :] = carry_ref[0:1, :]
        spim_ref[row, :] = carry_ref[1:2, :]


def _ssm_call(u, sza, h0re, h0im, tabl, tabq, d, bbig, cbig, w_glu, w_out_a, n_prompt_rows,
              seq_len, sample_steps):
    m, d_ssm = u.shape
    d_model = w_out_a.shape[1]
    n_state = tabl.shape[2]
    n_tiles = m // TILE_M
    n_p = n_prompt_rows // TILE_M
    n_batch = n_prompt_rows // seq_len
    seqs_per_tile = TILE_M // sample_steps
    n_sample_seq = h0re.shape[0]
    slab_rows = SUBLANES * _pitch(TILE_M // SUBLANES)
    kernel = functools.partial(_ssm_kernel, n_prompt_tiles=n_p,
                               tiles_per_seq=seq_len // TILE_M, sample_steps=sample_steps)
    sample_map = lambda i: (jnp.maximum(i - n_p, 0), 0)
    return pl.pallas_call(
        kernel,
        grid=(n_tiles,),
        in_specs=[
            pl.BlockSpec((TILE_M, d_ssm), lambda i: (i, 0)),
            pl.BlockSpec((TILE_M, d_ssm), lambda i: (i, 0)),
            pl.BlockSpec((seqs_per_tile, n_state), sample_map),
            pl.BlockSpec((seqs_per_tile, n_state), sample_map),
            _const_spec(tabl.shape),
            _const_spec(tabq.shape),
            _const_spec(d.shape),
            _const_spec(bbig.shape),
            _const_spec(cbig.shape),
            _const_spec(w_glu.shape),
            _const_spec(w_out_a.shape),
        ],
        out_specs=[
            pl.BlockSpec((TILE_M, d_model), lambda i: (i, 0)),
            pl.BlockSpec((n_batch, n_state), lambda i: (0, 0)),
            pl.BlockSpec((n_batch, n_state), lambda i: (0, 0)),
            pl.BlockSpec((seqs_per_tile, n_state), sample_map),
            pl.BlockSpec((seqs_per_tile, n_state), sample_map),
        ],
        out_shape=[
            jax.ShapeDtypeStruct((m, d_model), BF16),
            jax.ShapeDtypeStruct((n_batch, n_state), F32),
            jax.ShapeDtypeStruct((n_batch, n_state), F32),
            jax.ShapeDtypeStruct((n_sample_seq, n_state), F32),
            jax.ShapeDtypeStruct((n_sample_seq, n_state), F32),
        ],
        scratch_shapes=[
            pltpu.VMEM((bbig.shape[0], TILE_M, bbig.shape[2]), F32),
            pltpu.VMEM((d_ssm // LANES, slab_rows, LANES), F32),
            pltpu.VMEM((d_ssm // LANES, slab_rows, LANES), F32),
            pltpu.VMEM((2, n_state), F32),
        ],
        compiler_params=pltpu.CompilerParams(
            dimension_semantics=("arbitrary",), vmem_limit_bytes=VMEM_LIMIT_BYTES),
        name="ssm_branch",
    )(u, sza, h0re, h0im, tabl, tabq, d, bbig, cbig, w_glu, w_out_a)


def _tail_kernel(q_ref, bz_ref, ya_ref, sga_ref, sgc_ref, xp_ref, xs_ref, e0_ref, e1_ref,
                 cw_ref, woutc_ref, wo_ref, lng_ref, lnb_ref,
                 yp_ref, ys_ref,
                 pad_ref,
                 *, n_prompt_tiles, tiles_per_seq, alpha):
    i = pl.program_id(0)
    is_prompt = i < n_prompt_tiles

    @pl.when(jnp.logical_and(is_prompt, i % tiles_per_seq == 0))
    def _():
        pad_ref[0:CONV_PAD, :] = jnp.zeros((CONV_PAD, pad_ref.shape[1]), F32)

    q = q_ref[...]
    pad_ref[CONV_PAD:CONV_PAD + TILE_M, :] = q
    w0 = cw_ref[0:1, :]
    w1 = cw_ref[1:2, :]
    w2 = cw_ref[2:3, :]

    def finish(q1, q2, x_ref, y_ref):
        conv = w0 * q2 + w1 * q1 + w2 * q
        yc_in = conv * bz_ref[...].astype(F32)
        yc = jnp.dot(yc_in.astype(BF16), woutc_ref[...], preferred_element_type=F32)
        merged = (sga_ref[...].astype(F32) * ya_ref[...].astype(F32)
                  + sgc_ref[...].astype(F32) * yc)
        out = jnp.dot(merged.astype(BF16), wo_ref[...], preferred_element_type=F32)
        r = alpha * x_ref[...] + out
        mu = jnp.mean(r, axis=-1, keepdims=True)
        rc = r - mu
        var = jnp.mean(rc * rc, axis=-1, keepdims=True)
        y_ref[...] = rc * lax.rsqrt(var + LN_EPS) * lng_ref[...] + lnb_ref[...]

    @pl.when(is_prompt)
    def _():
        q1 = pad_ref[CONV_PAD - 1:CONV_PAD - 1 + TILE_M, :]
        q2 = pad_ref[CONV_PAD - 2:CONV_PAD - 2 + TILE_M, :]
        finish(q1, q2, xp_ref, yp_ref)
        pad_ref[0:CONV_PAD, :] = pad_ref[TILE_M:TILE_M + CONV_PAD, :]

    @pl.when(jnp.logical_not(is_prompt))
    def _():
        t = lax.broadcasted_iota(jnp.int32, q.shape, 0) % SUBLANES
        e0 = e0_ref[...]
        e1 = e1_ref[...]
        q1 = jnp.where(t == 0, e1, pad_ref[CONV_PAD - 1:CONV_PAD - 1 + TILE_M, :])
        q2 = jnp.where(t == 0, e0,
                       jnp.where(t == 1, e1, pad_ref[CONV_PAD - 2:CONV_PAD - 2 + TILE_M, :]))
        finish(q1, q2, xs_ref, ys_ref)


def _tail_call(q, bz, ya, sga, sgc, xp, xs, e0, e1, conv_w, w_out_c, w_o, ln_g, ln_b,
               seq_len, alpha):
    m, d_conv = q.shape
    m_p, d_model = xp.shape
    m_s = xs.shape[0]
    n_p = m_p // TILE_M
    kernel = functools.partial(_tail_kernel, n_prompt_tiles=n_p,
                               tiles_per_seq=seq_len // TILE_M, alpha=alpha)
    prompt_map = lambda i: (jnp.minimum(i, n_p - 1), 0)
    sample_map = lambda i: (jnp.maximum(i - n_p, 0), 0)
    row_map = lambda i: (i, 0)
    return pl.pallas_call(
        kernel,
        grid=(m // TILE_M,),
        in_specs=[
            pl.BlockSpec((TILE_M, d_conv), row_map),
            pl.BlockSpec((TILE_M, d_conv), row_map),
            pl.BlockSpec((TILE_M, d_model), row_map),
            pl.BlockSpec((TILE_M, d_model), row_map),
            pl.BlockSpec((TILE_M, d_model), row_map),
            pl.BlockSpec((TILE_M, d_model), prompt_map),
            pl.BlockSpec((TILE_M, d_model), sample_map),
            pl.BlockSpec((TILE_M, d_conv), sample_map),
            pl.BlockSpec((TILE_M, d_conv), sample_map),
            _const_spec(conv_w.shape),
            _const_spec(w_out_c.shape),
            _const_spec(w_o.shape),
            _const_spec(ln_g.shape),
            _const_spec(ln_b.shape),
        ],
        out_specs=[
            pl.BlockSpec((TILE_M, d_model), prompt_map),
            pl.BlockSpec((TILE_M, d_model), sample_map),
        ],
        out_shape=[
            jax.ShapeDtypeStruct((m_p, d_model), F32),
            jax.ShapeDtypeStruct((m_s, d_model), F32),
        ],
        scratch_shapes=[
            pltpu.VMEM((TILE_M + CONV_PAD, d_conv), F32),
        ],
        compiler_params=pltpu.CompilerParams(
            dimension_semantics=("arbitrary",), vmem_limit_bytes=VMEM_LIMIT_BYTES),
        name="tail",
    )(q, bz, ya, sga, sgc, xp, xs, e0, e1, conv_w, w_out_c, w_o, ln_g, ln_b)


def _ssm_params(a_re, a_im, log_dt, b_re, b_im, c_re, c_im, prompt_steps):
    g, p, gc = b_re.shape
    gpb = SSM_BUNDLE_CH // gc
    n_bundles = g // gpb
    dt = jnp.exp(log_dt)[:, None]
    mag = jnp.exp(a_re * dt)
    ang = a_im * dt
    lam_re = mag * jnp.cos(ang)
    lam_im = mag * jnp.sin(ang)
    den = a_re * a_re + a_im * a_im
    q_re = ((lam_re - 1.0) * a_re + lam_im * a_im) / den
    q_im = (lam_im * a_re - (lam_re - 1.0) * a_im) / den
    bb_re = q_re[..., None] * b_re - q_im[..., None] * b_im
    bb_im = q_re[..., None] * b_im + q_im[..., None] * b_re

    same_group = jnp.eye(gpb, dtype=F32)

    def b_blockdiag(bb):
        blk = bb.reshape(n_bundles, gpb, p, gc).transpose(0, 1, 3, 2)
        dense = blk[:, :, :, None, :] * same_group[None, :, None, :, None]
        return dense.reshape(n_bundles, gpb * gc, gpb * p)

    def c_blockdiag(cc):
        blk = cc.reshape(n_bundles, gpb, gc, p).transpose(0, 1, 3, 2)
        dense = blk[:, :, :, None, :] * same_group[None, :, None, :, None]
        return dense.reshape(n_bundles, gpb * p, gpb * gc)

    bbig = jnp.concatenate([b_blockdiag(bb_re), b_blockdiag(bb_im)], axis=2).astype(BF16)
    cbig = jnp.concatenate([c_blockdiag(c_re), -c_blockdiag(c_im)], axis=1).astype(BF16)

    la = (a_re * dt).reshape(1, -1)
    an = ang.reshape(1, -1)

    def power(e):
        m_e = jnp.exp(e * la)
        return m_e * jnp.cos(e * an), m_e * jnp.sin(e * an)

    steps = jnp.arange(1, prompt_steps + 1, dtype=F32)[:, None]
    tabl = jnp.stack(power(steps))

    row = jnp.arange(SUBLANES)[:, None]
    tabs = []
    for shift in (1, 2, 4):
        qr, qi = power(jnp.full((1, 1), float(shift * prompt_steps), F32))
        tabs.append(jnp.where(row >= shift, qr, 0.0))
        tabs.append(jnp.where(row >= shift, qi, 0.0))
    tabs += list(power((row + 1).astype(F32) * float(prompt_steps)))
    return bbig, cbig, tabl.astype(F32), jnp.stack(tabs).astype(F32)


def kernel(x_prompt, x_sample, state_ssm_re, state_ssm_im, state_conv, w_in, ssm_a_re, ssm_a_im, ssm_log_dt, ssm_b_re, ssm_b_im, ssm_c_re, ssm_c_im, ssm_d, w_glu, w_out_a, conv_w, w_out_c, w_o, ln_g, ln_b):
    depth = w_in.shape[0]
    assert depth == 1, "single-layer trunk"
    batch, seq, d_model = x_prompt.shape
    dec_batch, dec_seq, _ = x_sample.shape
    assert dec_seq == SUBLANES and seq % TILE_M == 0 and (dec_batch * dec_seq) % TILE_M == 0
    g, p, gc = ssm_b_re.shape[1:]
    d_ssm = g * gc
    d_conv = conv_w.shape[2]
    n_state = g * p
    alpha = (2 * depth) ** 0.25

    xp = x_prompt.reshape(batch * seq, d_model)
    xs = x_sample.reshape(dec_batch * dec_seq, d_model)
    m_p = xp.shape[0]
    w = w_in[0]

    u, sza = _proj_call(xp, xs, w, (0, 1), _epi_ssm_in, (d_ssm, d_ssm), (BF16, BF16), "proj_ssm")
    (bz,) = _proj_call(xp, xs, w, (2, 5), _epi_conv_gate, (d_conv,), (BF16,), "proj_conv_gate")
    (q,) = _proj_call(xp, xs, w, (3, 4), _epi_conv_in, (d_conv,), (F32,), "proj_conv_in")
    (sga,) = _proj_call(xp, xs, w, (6, 7), _epi_merge_gate, (d_model,), (BF16,), "proj_gate_a")
    (sgc,) = _proj_call(xp, xs, w, (8, 9), _epi_merge_gate, (d_model,), (BF16,), "proj_gate_c")

    bbig, cbig, tabl, tabq = _ssm_params(
        ssm_a_re[0], ssm_a_im[0], ssm_log_dt[0], ssm_b_re[0], ssm_b_im[0], ssm_c_re[0],
        ssm_c_im[0], TILE_M // SUBLANES)
    h0re = state_ssm_re[0].reshape(dec_batch, n_state)
    h0im = state_ssm_im[0].reshape(dec_batch, n_state)
    ya, spre, spim, ssre, ssim = _ssm_call(
        u, sza, h0re, h0im, tabl, tabq, ssm_d[0][None, :], bbig, cbig,
        w_glu[0].astype(BF16), w_out_a[0].astype(BF16), m_p, seq, dec_seq)

    e0 = jnp.repeat(state_conv[0, :, 0, :], dec_seq, axis=0)
    e1 = jnp.repeat(state_conv[0, :, 1, :], dec_seq, axis=0)
    yp, ys = _tail_call(q, bz, ya, sga, sgc, xp, xs, e0, e1, conv_w[0],
                        w_out_c[0].astype(BF16), w_o[0].astype(BF16),
                        ln_g[0][None, :], ln_b[0][None, :], seq, alpha)

    q_p = q[:m_p].reshape(batch, seq, d_conv)
    q_s = q[m_p:].reshape(dec_batch, dec_seq, d_conv)
    return (yp.reshape(batch, seq, d_model),
            ys.reshape(dec_batch, dec_seq, d_model),
            spre.reshape(1, batch, g, p),
            spim.reshape(1, batch, g, p),
            q_p[:, seq - 2:, :][None],
            ssre.reshape(1, dec_batch, g, p),
            ssim.reshape(1, dec_batch, g, p),
            q_s[:, dec_seq - 2:, :][None])
```

```python
import functools
import math

import jax
import jax.numpy as jnp
from jax import lax
from jax.experimental import pallas as pl
from jax.experimental.pallas import tpu as pltpu

F32 = jnp.float32
BF16 = jnp.bfloat16

SUBLANES = 8
LANES = 128
VMEM_LIMIT_BYTES = 56 * 1024 * 1024

TILE_M = 256
SSM_BUNDLE_CH = 256
SCAN_CHUNK = 4 * LANES
CONV_PAD = SUBLANES

LN_EPS = 1e-5
GELU_C = math.sqrt(2.0 / math.pi)


def _sigmoid(x):
    return 0.5 * jnp.tanh(0.5 * x) + 0.5


def _silu(x):
    return x * _sigmoid(x)


def _gelu_tanh(x):
    return 0.5 * x * (1.0 + jnp.tanh(GELU_C * (x + 0.044715 * (x * x * x))))


def _const_spec(shape):
    nd = len(shape)
    return pl.BlockSpec(shape, lambda i: (0,) * nd, pipeline_mode=pl.Buffered(1))


def _proj_kernel(xp_ref, xs_ref, w0_ref, w1_ref, *rest, n_prompt_tiles, epilogue, n_out):
    out_refs = rest[:n_out]
    wbf_ref = rest[n_out]
    i = pl.program_id(0)

    @pl.when(i == 0)
    def _():
        wbf_ref[0] = w0_ref[...].astype(BF16)
        wbf_ref[1] = w1_ref[...].astype(BF16)

    def compute(x_ref):
        xb = x_ref[...].astype(BF16)
        a = jnp.dot(xb, wbf_ref[0], preferred_element_type=F32)
        b = jnp.dot(xb, wbf_ref[1], preferred_element_type=F32)
        epilogue(a, b, out_refs)

    @pl.when(i < n_prompt_tiles)
    def _():
        compute(xp_ref)

    @pl.when(i >= n_prompt_tiles)
    def _():
        compute(xs_ref)


def _epi_ssm_in(a, b, out_refs):
    out_refs[0][...] = a.astype(BF16)
    out_refs[1][...] = _silu(b).astype(BF16)


def _epi_conv_gate(a, b, out_refs):
    out_refs[0][...] = (a * _silu(b)).astype(BF16)


def _epi_conv_in(a, b, out_refs):
    out_refs[0][...] = a * b


def _epi_merge_gate(a, b, out_refs):
    half = a.shape[1]
    out_refs[0][:, :half] = _sigmoid(a).astype(BF16)
    out_refs[0][:, half:] = _sigmoid(b).astype(BF16)


def _proj_call(xp, xs, w_in, col_blocks, epilogue, out_widths, out_dtypes, name):
    m_p, d = xp.shape
    m_s = xs.shape[0]
    n_p, n_s = m_p // TILE_M, m_s // TILE_M
    wcol = 1024
    c0, c1 = col_blocks
    kernel = functools.partial(_proj_kernel, n_prompt_tiles=n_p, epilogue=epilogue,
                               n_out=len(out_widths))
    return pl.pallas_call(
        kernel,
        grid=(n_p + n_s,),
        in_specs=[
            pl.BlockSpec((TILE_M, d), lambda i: (jnp.minimum(i, n_p - 1), 0)),
            pl.BlockSpec((TILE_M, d), lambda i: (jnp.maximum(i - n_p, 0), 0)),
            pl.BlockSpec((d, wcol), lambda i: (0, c0), pipeline_mode=pl.Buffered(1)),
            pl.BlockSpec((d, wcol), lambda i: (0, c1), pipeline_mode=pl.Buffered(1)),
        ],
        out_specs=[pl.BlockSpec((TILE_M, w), lambda i: (i, 0)) for w in out_widths],
        out_shape=[jax.ShapeDtypeStruct((m_p + m_s, w), dt)
                   for w, dt in zip(out_widths, out_dtypes)],
        scratch_shapes=[pltpu.VMEM((2, d, wcol), BF16)],
        compiler_params=pltpu.CompilerParams(
            dimension_semantics=("arbitrary",), vmem_limit_bytes=VMEM_LIMIT_BYTES),
        name=name,
    )(xp, xs, w_in, w_in)


def _pitch(steps):
    return steps if (steps // SUBLANES) % 2 == 1 else steps + SUBLANES


def _to_stepmajor(val, slab_ref, n_groups, steps):
    pitch = _pitch(steps)
    n_slabs = val.shape[1] // LANES
    n_sub = n_groups * SUBLANES
    for j in range(n_slabs):
        lanes = slice(j * LANES, (j + 1) * LANES)
        if pitch == steps:
            slab_ref[j, 0:n_sub * steps, :] = val[:, lanes]
        else:
            for s in range(n_sub):
                slab_ref[j, s * pitch:s * pitch + steps, :] = val[s * steps:(s + 1) * steps, lanes]
    blocks = []
    for g in range(n_groups):
        for k in range(steps):
            rows = pl.ds(g * SUBLANES * pitch + k, SUBLANES, stride=pitch)
            blocks.append(jnp.concatenate([slab_ref[j, rows, :] for j in range(n_slabs)], axis=1))
    return jnp.concatenate(blocks, axis=0)


def _store_stepmajor(val, slab_ref, slab0, n_groups, steps):
    pitch = _pitch(steps)
    for g in range(n_groups):
        for k in range(steps):
            r0 = (g * steps + k) * SUBLANES
            rows = pl.ds(g * SUBLANES * pitch + k, SUBLANES, stride=pitch)
            for j in range(val.shape[1] // LANES):
                slab_ref[slab0 + j, rows, :] = val[r0:r0 + SUBLANES, j * LANES:(j + 1) * LANES]


def _load_natural(slab_ref, n_groups, steps):
    pitch = _pitch(steps)
    n_slabs = slab_ref.shape[0]
    n_sub = n_groups * SUBLANES
    if pitch == steps:
        return jnp.concatenate([slab_ref[j, 0:n_sub * steps, :] for j in range(n_slabs)], axis=1)
    return jnp.concatenate(
        [jnp.concatenate([slab_ref[j, s * pitch:s * pitch + steps, :] for j in range(n_slabs)],
                         axis=1) for s in range(n_sub)], axis=0)


def _cmuladd(ar, ai, br, bi, cr, ci):
    return ar * br - ai * bi + cr, ar * bi + ai * br + ci


def _scan_group(bu_ref, b, row0, steps, width, lane0, tabl_ref, enter_fn, leave_fn):
    for c in range(width // SCAN_CHUNK):
        re = slice(c * SCAN_CHUNK, (c + 1) * SCAN_CHUNK)
        im = slice(width + c * SCAN_CHUNK, width + (c + 1) * SCAN_CHUNK)
        tl = slice(lane0 + c * SCAN_CHUNK, lane0 + (c + 1) * SCAN_CHUNK)

        def power(k):
            rows = slice(k * SUBLANES, (k + 1) * SUBLANES)
            return tabl_ref[0, rows, tl], tabl_ref[1, rows, tl]

        lr, li = power(0)
        pr = pi = None
        for k in range(steps):
            rows = slice(row0 + k * SUBLANES, row0 + (k + 1) * SUBLANES)
            xr, xi = bu_ref[b, rows, re], bu_ref[b, rows, im]
            if k == 0:
                pr, pi = xr, xi
            else:
                pr, pi = _cmuladd(lr, li, pr, pi, xr, xi)
                bu_ref[b, rows, re] = pr
                bu_ref[b, rows, im] = pi
        cr, ci = enter_fn(pr, pi, tl)
        hr = hi = None
        for k in range(steps):
            rows = slice(row0 + k * SUBLANES, row0 + (k + 1) * SUBLANES)
            wr, wi = power(k)
            hr, hi = _cmuladd(wr, wi, cr, ci, bu_ref[b, rows, re], bu_ref[b, rows, im])
            bu_ref[b, rows, re] = hr
            bu_ref[b, rows, im] = hi
        leave_fn(hr, hi, tl)


def _build_blockdiag(b2_ref, ct_ref, bbig_ref, cbig_ref, group_ch, group_states):
    n_bundles, n_ch, two_width = bbig_ref.shape
    width = two_width // 2
    ch_shift = group_ch.bit_length() - 1
    st_shift = group_states.bit_length() - 1
    assert group_ch == 1 << ch_shift and group_states == 1 << st_shift
    ch_group = lax.shift_right_logical(lax.broadcasted_iota(jnp.int32, (n_ch, LANES), 0), ch_shift)
    b_lane_group = lax.shift_right_logical(lax.broadcasted_iota(jnp.int32, (n_ch, LANES), 1),
                                           st_shift)
    st_group = lax.shift_right_logical(lax.broadcasted_iota(jnp.int32, (width, LANES), 0), st_shift)
    c_lane_group = lax.shift_right_logical(lax.broadcasted_iota(jnp.int32, (width, LANES), 1),
                                           ch_shift)
    for b in range(n_bundles):
        for part in range(2):
            src = b2_ref[part, b * n_ch:(b + 1) * n_ch, :]
            for j in range(width // LANES):
                keep = ch_group == b_lane_group + j * (LANES // group_states)
                lanes = slice(part * width + j * LANES, part * width + (j + 1) * LANES)
                bbig_ref[b, :, lanes] = jnp.where(keep, src, 0.0).astype(BF16)
            src = ct_ref[part, b * width:(b + 1) * width, :]
            for j in range(n_ch // LANES):
                keep = st_group == c_lane_group + j * (LANES // group_ch)
                cbig_ref[b, part * width:(part + 1) * width, j * LANES:(j + 1) * LANES] = (
                    jnp.where(keep, src, 0.0).astype(BF16))


def _ssm_kernel(u_ref, sza_ref, h0re_ref, h0im_ref, tabl_ref, tabq_ref, d_ref, b2_ref, ct_ref,
                wglu_ref, wouta_ref,
                ya_ref, spre_ref, spim_ref, ssre_ref, ssim_ref,
                bu_ref, uslab_ref, yslab_ref, carry_ref, bbig_ref, cbig_ref,
                *, n_prompt_tiles, tiles_per_seq, sample_steps, group_ch, group_states):
    i = pl.program_id(0)
    n_bundles = bbig_ref.shape[0]
    width = bbig_ref.shape[2] // 2
    is_prompt = i < n_prompt_tiles
    t_in_seq = i % tiles_per_seq

    @pl.when(i == 0)
    def _():
        _build_blockdiag(b2_ref, ct_ref, bbig_ref, cbig_ref, group_ch, group_states)

    @pl.when(jnp.logical_and(is_prompt, t_in_seq == 0))
    def _():
        carry_ref[...] = jnp.zeros_like(carry_ref)

    def run(n_groups, steps, enter_factory, leave_factory):
        u_sm = _to_stepmajor(u_ref[...].astype(F32), uslab_ref, n_groups, steps).astype(BF16)

        def b_matmul(b):
            ch = slice(b * SSM_BUNDLE_CH, (b + 1) * SSM_BUNDLE_CH)
            bu_ref[b] = jnp.dot(u_sm[:, ch], bbig_ref[b], preferred_element_type=F32)

        b_matmul(0)
        for b in range(n_bundles):
            if b + 1 < n_bundles:
                b_matmul(b + 1)
            for g in range(n_groups):
                _scan_group(bu_ref, b, g * SUBLANES * steps, steps, width, b * width, tabl_ref,
                            enter_factory(g), leave_factory(g))
            y_b = jnp.dot(bu_ref[b].astype(BF16), cbig_ref[b], preferred_element_type=F32)
            _store_stepmajor(y_b, yslab_ref, b * (SSM_BUNDLE_CH // LANES), n_groups, steps)
        y = _load_natural(yslab_ref, n_groups, steps) + d_ref[...] * u_ref[...].astype(F32)
        g_act = _gelu_tanh(y)
        z = jnp.dot(g_act.astype(BF16), wglu_ref[...], preferred_element_type=F32)
        o = g_act * _sigmoid(z) * sza_ref[...].astype(F32)
        ya = jnp.dot(o.astype(BF16), wouta_ref[...], preferred_element_type=F32)
        ya_ref[...] = ya.astype(BF16)

    def prompt_enter(_g):
        def enter(er, ei, tl):
            xr, xi = er, ei
            for step, shift in enumerate((1, 2, 4)):
                xr, xi = _cmuladd(tabq_ref[2 * step, :, tl], tabq_ref[2 * step + 1, :, tl],
                                  pltpu.roll(xr, shift, 0), pltpu.roll(xi, shift, 0), xr, xi)
            c0r = jnp.broadcast_to(carry_ref[0:1, tl], xr.shape)
            c0i = jnp.broadcast_to(carry_ref[1:2, tl], xi.shape)
            xr, xi = _cmuladd(tabq_ref[6, :, tl], tabq_ref[7, :, tl], c0r, c0i, xr, xi)
            carry_ref[0:1, tl] = xr[SUBLANES - 1:SUBLANES, :]
            carry_ref[1:2, tl] = xi[SUBLANES - 1:SUBLANES, :]
            first = lax.broadcasted_iota(jnp.int32, xr.shape, 0) == 0
            return (jnp.where(first, c0r, pltpu.roll(xr, 1, 0)),
                    jnp.where(first, c0i, pltpu.roll(xi, 1, 0)))
        return enter

    def prompt_leave(_g):
        return lambda hr, hi, tl: None

    def sample_enter(g):
        rows = slice(g * SUBLANES, (g + 1) * SUBLANES)
        return lambda er, ei, tl: (h0re_ref[rows, tl], h0im_ref[rows, tl])

    def sample_leave(g):
        rows = slice(g * SUBLANES, (g + 1) * SUBLANES)

        def leave(hr, hi, tl):
            ssre_ref[rows, tl] = hr
            ssim_ref[rows, tl] = hi
        return leave

    @pl.when(is_prompt)
    def _():
        run(1, TILE_M // SUBLANES, prompt_enter, prompt_leave)

    @pl.when(jnp.logical_not(is_prompt))
    def _():
        run(TILE_M // (SUBLANES * sample_steps), sample_steps, sample_enter, sample_leave)

    @pl.when(jnp.logical_and(is_prompt, t_in_seq == tiles_per_seq - 1))
    def _():
        row = pl.ds(i // tiles_per_seq, 1)
        spre_ref[row, :] = carry_ref[0:1, :]
        spim_ref[row, :] = carry_ref[1:2, :]


def _ssm_call(u, sza, h0re, h0im, tabl, tabq, d, b2, ct, w_glu, w_out_a, n_prompt_rows,
              seq_len, sample_steps, group_ch, group_states):
    m, d_ssm = u.shape
    d_model = w_out_a.shape[1]
    n_state = tabq.shape[2]
    n_tiles = m // TILE_M
    n_p = n_prompt_rows // TILE_M
    n_batch = n_prompt_rows // seq_len
    seqs_per_tile = TILE_M // sample_steps
    n_sample_seq = h0re.shape[0]
    slab_rows = SUBLANES * _pitch(TILE_M // SUBLANES)
    n_bundles = d_ssm // SSM_BUNDLE_CH
    width = n_state // n_bundles
    kernel = functools.partial(_ssm_kernel, n_prompt_tiles=n_p,
                               tiles_per_seq=seq_len // TILE_M, sample_steps=sample_steps,
                               group_ch=group_ch, group_states=group_states)
    sample_map = lambda i: (jnp.maximum(i - n_p, 0), 0)
    return pl.pallas_call(
        kernel,
        grid=(n_tiles,),
        in_specs=[
            pl.BlockSpec((TILE_M, d_ssm), lambda i: (i, 0)),
            pl.BlockSpec((TILE_M, d_ssm), lambda i: (i, 0)),
            pl.BlockSpec((seqs_per_tile, n_state), sample_map),
            pl.BlockSpec((seqs_per_tile, n_state), sample_map),
            _const_spec(tabl.shape),
            _const_spec(tabq.shape),
            _const_spec(d.shape),
            _const_spec(b2.shape),
            _const_spec(ct.shape),
            _const_spec(w_glu.shape),
            _const_spec(w_out_a.shape),
        ],
        out_specs=[
            pl.BlockSpec((TILE_M, d_model), lambda i: (i, 0)),
            pl.BlockSpec((n_batch, n_state), lambda i: (0, 0)),
            pl.BlockSpec((n_batch, n_state), lambda i: (0, 0)),
            pl.BlockSpec((seqs_per_tile, n_state), sample_map),
            pl.BlockSpec((seqs_per_tile, n_state), sample_map),
        ],
        out_shape=[
            jax.ShapeDtypeStruct((m, d_model), BF16),
            jax.ShapeDtypeStruct((n_batch, n_state), F32),
            jax.ShapeDtypeStruct((n_batch, n_state), F32),
            jax.ShapeDtypeStruct((n_sample_seq, n_state), F32),
            jax.ShapeDtypeStruct((n_sample_seq, n_state), F32),
        ],
        scratch_shapes=[
            pltpu.VMEM((n_bundles, TILE_M, 2 * width), F32),
            pltpu.VMEM((d_ssm // LANES, slab_rows, LANES), F32),
            pltpu.VMEM((d_ssm // LANES, slab_rows, LANES), F32),
            pltpu.VMEM((2, n_state), F32),
            pltpu.VMEM((n_bundles, SSM_BUNDLE_CH, 2 * width), BF16),
            pltpu.VMEM((n_bundles, 2 * width, SSM_BUNDLE_CH), BF16),
        ],
        compiler_params=pltpu.CompilerParams(
            dimension_semantics=("arbitrary",), vmem_limit_bytes=VMEM_LIMIT_BYTES),
        name="ssm_branch",
    )(u, sza, h0re, h0im, tabl, tabq, d, b2, ct, w_glu, w_out_a)


def _tail_kernel(q_ref, bz_ref, ya_ref, sga_ref, sgc_ref, xp_ref, xs_ref, hist_ref,
                 cw_ref, woutc_ref, wo_ref, lng_ref, lnb_ref,
                 yp_ref, ys_ref,
                 pad_ref,
                 *, n_prompt_tiles, tiles_per_seq, alpha):
    i = pl.program_id(0)
    is_prompt = i < n_prompt_tiles

    @pl.when(jnp.logical_and(is_prompt, i % tiles_per_seq == 0))
    def _():
        pad_ref[0:CONV_PAD, :] = jnp.zeros((CONV_PAD, pad_ref.shape[1]), F32)

    q = q_ref[...]
    pad_ref[CONV_PAD:CONV_PAD + TILE_M, :] = q
    w0 = cw_ref[0:1, :]
    w1 = cw_ref[1:2, :]
    w2 = cw_ref[2:3, :]

    def finish(q1, q2, x_ref, y_ref):
        conv = w0 * q2 + w1 * q1 + w2 * q
        yc_in = conv * bz_ref[...].astype(F32)
        yc = jnp.dot(yc_in.astype(BF16), woutc_ref[...], preferred_element_type=F32)
        merged = (sga_ref[...].astype(F32) * ya_ref[...].astype(F32)
                  + sgc_ref[...].astype(F32) * yc)
        out = jnp.dot(merged.astype(BF16), wo_ref[...], preferred_element_type=F32)
        r = alpha * x_ref[...] + out
        mu = jnp.mean(r, axis=-1, keepdims=True)
        rc = r - mu
        var = jnp.mean(rc * rc, axis=-1, keepdims=True)
        y_ref[...] = rc * lax.rsqrt(var + LN_EPS) * lng_ref[...] + lnb_ref[...]

    @pl.when(is_prompt)
    def _():
        q1 = pad_ref[CONV_PAD - 1:CONV_PAD - 1 + TILE_M, :]
        q2 = pad_ref[CONV_PAD - 2:CONV_PAD - 2 + TILE_M, :]
        finish(q1, q2, xp_ref, yp_ref)
        pad_ref[0:CONV_PAD, :] = pad_ref[TILE_M:TILE_M + CONV_PAD, :]

    @pl.when(jnp.logical_not(is_prompt))
    def _():
        t = lax.broadcasted_iota(jnp.int32, q.shape, 0) % SUBLANES
        d_conv = q.shape[1]

        def history(j):
            return jnp.concatenate(
                [jnp.broadcast_to(hist_ref[s:s + 1, j * d_conv:(j + 1) * d_conv],
                                  (SUBLANES, d_conv)) for s in range(TILE_M // SUBLANES)], axis=0)

        e0 = history(0)
        e1 = history(1)
        q1 = jnp.where(t == 0, e1, pad_ref[CONV_PAD - 1:CONV_PAD - 1 + TILE_M, :])
        q2 = jnp.where(t == 0, e0,
                       jnp.where(t == 1, e1, pad_ref[CONV_PAD - 2:CONV_PAD - 2 + TILE_M, :]))
        finish(q1, q2, xs_ref, ys_ref)


def _tail_call(q, bz, ya, sga, sgc, xp, xs, hist, conv_w, w_out_c, w_o, ln_g, ln_b,
               seq_len, alpha):
    m, d_conv = q.shape
    m_p, d_model = xp.shape
    m_s = xs.shape[0]
    n_p = m_p // TILE_M
    kernel = functools.partial(_tail_kernel, n_prompt_tiles=n_p,
                               tiles_per_seq=seq_len // TILE_M, alpha=alpha)
    prompt_map = lambda i: (jnp.minimum(i, n_p - 1), 0)
    sample_map = lambda i: (jnp.maximum(i - n_p, 0), 0)
    row_map = lambda i: (i, 0)
    return pl.pallas_call(
        kernel,
        grid=(m // TILE_M,),
        in_specs=[
            pl.BlockSpec((TILE_M, d_conv), row_map),
            pl.BlockSpec((TILE_M, d_conv), row_map),
            pl.BlockSpec((TILE_M, d_model), row_map),
            pl.BlockSpec((TILE_M, d_model), row_map),
            pl.BlockSpec((TILE_M, d_model), row_map),
            pl.BlockSpec((TILE_M, d_model), prompt_map),
            pl.BlockSpec((TILE_M, d_model), sample_map),
            pl.BlockSpec((TILE_M // SUBLANES, hist.shape[1]), sample_map),
            _const_spec(conv_w.shape),
            _const_spec(w_out_c.shape),
            _const_spec(w_o.shape),
            _const_spec(ln_g.shape),
            _const_spec(ln_b.shape),
        ],
        out_specs=[
            pl.BlockSpec((TILE_M, d_model), prompt_map),
            pl.BlockSpec((TILE_M, d_model), sample_map),
        ],
        out_shape=[
            jax.ShapeDtypeStruct((m_p, d_model), F32),
            jax.ShapeDtypeStruct((m_s, d_model), F32),
        ],
        scratch_shapes=[
            pltpu.VMEM((TILE_M + CONV_PAD, d_conv), F32),
        ],
        compiler_params=pltpu.CompilerParams(
            dimension_semantics=("arbitrary",), vmem_limit_bytes=VMEM_LIMIT_BYTES),
        name="tail",
    )(q, bz, ya, sga, sgc, xp, xs, hist, conv_w, w_out_c, w_o, ln_g, ln_b)


def _ssm_params(a_re, a_im, log_dt, b_re, b_im, c_re, c_im, prompt_steps):
    g, p, gc = b_re.shape
    dt = jnp.exp(log_dt)[:, None]
    mag = jnp.exp(a_re * dt)
    ang = a_im * dt
    lam_re = mag * jnp.cos(ang)
    lam_im = mag * jnp.sin(ang)
    den = a_re * a_re + a_im * a_im
    q_re = ((lam_re - 1.0) * a_re + lam_im * a_im) / den
    q_im = (lam_im * a_re - (lam_re - 1.0) * a_im) / den
    bb_re = q_re[..., None] * b_re - q_im[..., None] * b_im
    bb_im = q_re[..., None] * b_im + q_im[..., None] * b_re

    def channel_rows(bb):
        rows = bb.transpose(0, 2, 1).reshape(g * gc, p)
        return jnp.tile(rows, (1, LANES // p))

    def state_rows(cc):
        rows = cc.transpose(0, 2, 1).reshape(g * p, gc)
        return jnp.tile(rows, (1, LANES // gc))

    b2 = jnp.stack([channel_rows(bb_re), channel_rows(bb_im)])
    ct = jnp.stack([state_rows(c_re), -state_rows(c_im)])

    la = (a_re * dt).reshape(1, -1)
    an = ang.reshape(1, -1)

    def power(e):
        m_e = jnp.exp(e * la)
        return m_e * jnp.cos(e * an), m_e * jnp.sin(e * an)

    steps = jnp.arange(1, prompt_steps + 1, dtype=F32)[:, None]
    tabl = jnp.repeat(jnp.stack(power(steps)), SUBLANES, axis=1)

    row = jnp.arange(SUBLANES)[:, None]
    tabs = []
    for shift in (1, 2, 4):
        qr, qi = power(jnp.full((1, 1), float(shift * prompt_steps), F32))
        tabs.append(jnp.where(row >= shift, qr, 0.0))
        tabs.append(jnp.where(row >= shift, qi, 0.0))
    tabs += list(power((row + 1).astype(F32) * float(prompt_steps)))
    return b2, ct, tabl.astype(F32), jnp.stack(tabs).astype(F32)


def kernel(x_prompt, x_sample, state_ssm_re, state_ssm_im, state_conv, w_in, ssm_a_re, ssm_a_im, ssm_log_dt, ssm_b_re, ssm_b_im, ssm_c_re, ssm_c_im, ssm_d, w_glu, w_out_a, conv_w, w_out_c, w_o, ln_g, ln_b):
    depth = w_in.shape[0]
    assert depth == 1, "single-layer trunk"
    batch, seq, d_model = x_prompt.shape
    dec_batch, dec_seq, _ = x_sample.shape
    assert dec_seq == SUBLANES and seq % TILE_M == 0 and (dec_batch * dec_seq) % TILE_M == 0
    g, p, gc = ssm_b_re.shape[1:]
    d_ssm = g * gc
    d_conv = conv_w.shape[2]
    n_state = g * p
    alpha = (2 * depth) ** 0.25

    xp = x_prompt.reshape(batch * seq, d_model)
    xs = x_sample.reshape(dec_batch * dec_seq, d_model)
    m_p = xp.shape[0]
    w = w_in[0]

    u, sza = _proj_call(xp, xs, w, (0, 1), _epi_ssm_in, (d_ssm, d_ssm), (BF16, BF16), "proj_ssm")
    (bz,) = _proj_call(xp, xs, w, (2, 5), _epi_conv_gate, (d_conv,), (BF16,), "proj_conv_gate")
    (q,) = _proj_call(xp, xs, w, (3, 4), _epi_conv_in, (d_conv,), (F32,), "proj_conv_in")
    (sga,) = _proj_call(xp, xs, w, (6, 7), _epi_merge_gate, (d_model,), (BF16,), "proj_gate_a")
    (sgc,) = _proj_call(xp, xs, w, (8, 9), _epi_merge_gate, (d_model,), (BF16,), "proj_gate_c")

    b2, ct, tabl, tabq = _ssm_params(
        ssm_a_re[0], ssm_a_im[0], ssm_log_dt[0], ssm_b_re[0], ssm_b_im[0], ssm_c_re[0],
        ssm_c_im[0], TILE_M // SUBLANES)
    h0re = state_ssm_re[0].reshape(dec_batch, n_state)
    h0im = state_ssm_im[0].reshape(dec_batch, n_state)
    ya, spre, spim, ssre, ssim = _ssm_call(
        u, sza, h0re, h0im, tabl, tabq, ssm_d[0][None, :], b2, ct,
        w_glu[0].astype(BF16), w_out_a[0].astype(BF16), m_p, seq, dec_seq, gc, p)

    hist = state_conv[0].reshape(dec_batch, (state_conv.shape[2]) * d_conv)
    yp, ys = _tail_call(q, bz, ya, sga, sgc, xp, xs, hist, conv_w[0],
                        w_out_c[0].astype(BF16), w_o[0].astype(BF16),
                        ln_g[0][None, :], ln_b[0][None, :], seq, alpha)

    q8 = q.reshape(-1, SUBLANES, d_conv)
    q_p = q8[seq // SUBLANES - 1:m_p // SUBLANES:seq // SUBLANES, SUBLANES - 2:, :]
    q_s = q8[m_p // SUBLANES:, dec_seq - 2:, :]
    return (yp.reshape(batch, seq, d_model),
            ys.reshape(dec_batch, dec_seq, d_model),
            spre.reshape(1, batch, g, p),
            spim.reshape(1, batch, g, p),
            q_p[None],
            ssre.reshape(1, dec_batch, g, p),
            ssim.reshape(1, dec_batch, g, p),
            q_s[None])
```

```python
import functools
import math

import jax
import jax.numpy as jnp
from jax import lax
from jax.experimental import pallas as pl
from jax.experimental.pallas import tpu as pltpu

F32 = jnp.float32
BF16 = jnp.bfloat16

SUBLANES = 8
LANES = 128
VMEM_LIMIT_BYTES = 56 * 1024 * 1024

TILE_M = 256
SSM_BUNDLE_CH = 256
SCAN_CHUNK = 4 * LANES
CONV_PAD = SUBLANES

LN_EPS = 1e-5
GELU_C = math.sqrt(2.0 / math.pi)


def _sigmoid(x):
    return 0.5 * jnp.tanh(0.5 * x) + 0.5


def _silu(x):
    return x * _sigmoid(x)


def _gelu_tanh(x):
    return 0.5 * x * (1.0 + jnp.tanh(GELU_C * (x + 0.044715 * (x * x * x))))


def _const_spec(shape):
    nd = len(shape)
    return pl.BlockSpec(shape, lambda i: (0,) * nd, pipeline_mode=pl.Buffered(1))


def _proj_kernel(xp_ref, xs_ref, w0_ref, w1_ref, *rest, n_prompt_tiles, epilogue, n_out):
    out_refs = rest[:n_out]
    wbf_ref = rest[n_out]
    i = pl.program_id(0)

    @pl.when(i == 0)
    def _():
        wbf_ref[0] = w0_ref[...].astype(BF16)
        wbf_ref[1] = w1_ref[...].astype(BF16)

    def compute(x_ref):
        xb = x_ref[...].astype(BF16)
        a = jnp.dot(xb, wbf_ref[0], preferred_element_type=F32)
        b = jnp.dot(xb, wbf_ref[1], preferred_element_type=F32)
        epilogue(a, b, out_refs)

    @pl.when(i < n_prompt_tiles)
    def _():
        compute(xp_ref)

    @pl.when(i >= n_prompt_tiles)
    def _():
        compute(xs_ref)


def _epi_ssm_in(a, b, out_refs):
    out_refs[0][...] = a.astype(BF16)
    out_refs[1][...] = _silu(b).astype(BF16)


def _epi_conv_gate(a, b, out_refs):
    out_refs[0][...] = (a * _silu(b)).astype(BF16)


def _epi_conv_in(a, b, out_refs):
    out_refs[0][...] = a * b


def _epi_merge_gate(a, b, out_refs):
    half = a.shape[1]
    out_refs[0][:, :half] = _sigmoid(a).astype(BF16)
    out_refs[0][:, half:] = _sigmoid(b).astype(BF16)


def _proj_call(xp, xs, w_in, col_blocks, epilogue, out_widths, out_dtypes, name):
    m_p, d = xp.shape
    m_s = xs.shape[0]
    n_p, n_s = m_p // TILE_M, m_s // TILE_M
    wcol = 1024
    c0, c1 = col_blocks
    kernel = functools.partial(_proj_kernel, n_prompt_tiles=n_p, epilogue=epilogue,
                               n_out=len(out_widths))
    return pl.pallas_call(
        kernel,
        grid=(n_p + n_s,),
        in_specs=[
            pl.BlockSpec((TILE_M, d), lambda i: (jnp.minimum(i, n_p - 1), 0)),
            pl.BlockSpec((TILE_M, d), lambda i: (jnp.maximum(i - n_p, 0), 0)),
            pl.BlockSpec((d, wcol), lambda i: (0, c0), pipeline_mode=pl.Buffered(1)),
            pl.BlockSpec((d, wcol), lambda i: (0, c1), pipeline_mode=pl.Buffered(1)),
        ],
        out_specs=[pl.BlockSpec((TILE_M, w), lambda i: (i, 0)) for w in out_widths],
        out_shape=[jax.ShapeDtypeStruct((m_p + m_s, w), dt)
                   for w, dt in zip(out_widths, out_dtypes)],
        scratch_shapes=[pltpu.VMEM((2, d, wcol), BF16)],
        compiler_params=pltpu.CompilerParams(
            dimension_semantics=("arbitrary",), vmem_limit_bytes=VMEM_LIMIT_BYTES),
        name=name,
    )(xp, xs, w_in, w_in)


def _pitch(steps):
    return steps if (steps // SUBLANES) % 2 == 1 else steps + SUBLANES


def _to_stepmajor(val, slab_ref, n_groups, steps):
    pitch = _pitch(steps)
    n_slabs = val.shape[1] // LANES
    n_sub = n_groups * SUBLANES
    for j in range(n_slabs):
        lanes = slice(j * LANES, (j + 1) * LANES)
        if pitch == steps:
            slab_ref[j, 0:n_sub * steps, :] = val[:, lanes]
        else:
            for s in range(n_sub):
                slab_ref[j, s * pitch:s * pitch + steps, :] = val[s * steps:(s + 1) * steps, lanes]
    blocks = []
    for g in range(n_groups):
        for k in range(steps):
            rows = pl.ds(g * SUBLANES * pitch + k, SUBLANES, stride=pitch)
            blocks.append(jnp.concatenate([slab_ref[j, rows, :] for j in range(n_slabs)], axis=1))
    return jnp.concatenate(blocks, axis=0)


def _store_stepmajor(val, slab_ref, slab0, n_groups, steps):
    pitch = _pitch(steps)
    for g in range(n_groups):
        for k in range(steps):
            r0 = (g * steps + k) * SUBLANES
            rows = pl.ds(g * SUBLANES * pitch + k, SUBLANES, stride=pitch)
            for j in range(val.shape[1] // LANES):
                slab_ref[slab0 + j, rows, :] = val[r0:r0 + SUBLANES, j * LANES:(j + 1) * LANES]


def _load_natural(slab_ref, n_groups, steps):
    pitch = _pitch(steps)
    n_slabs = slab_ref.shape[0]
    n_sub = n_groups * SUBLANES
    if pitch == steps:
        return jnp.concatenate([slab_ref[j, 0:n_sub * steps, :] for j in range(n_slabs)], axis=1)
    return jnp.concatenate(
        [jnp.concatenate([slab_ref[j, s * pitch:s * pitch + steps, :] for j in range(n_slabs)],
                         axis=1) for s in range(n_sub)], axis=0)


def _cmuladd(ar, ai, br, bi, cr, ci):
    return ar * br - ai * bi + cr, ar * bi + ai * br + ci


def _scan_group(bu_ref, b, row0, steps, width, lane0, lam_ref, enter_fn, leave_fn):
    for c in range(width // SCAN_CHUNK):
        re = slice(c * SCAN_CHUNK, (c + 1) * SCAN_CHUNK)
        im = slice(width + c * SCAN_CHUNK, width + (c + 1) * SCAN_CHUNK)
        tl = slice(lane0 + c * SCAN_CHUNK, lane0 + (c + 1) * SCAN_CHUNK)
        lr, li = lam_ref[0, :, tl], lam_ref[1, :, tl]

        def x_block(k):
            rows = slice(row0 + k * SUBLANES, row0 + (k + 1) * SUBLANES)
            return rows, bu_ref[b, rows, re], bu_ref[b, rows, im]

        _, pr, pi = x_block(0)
        for k in range(1, steps):
            _, xr, xi = x_block(k)
            pr, pi = _cmuladd(lr, li, pr, pi, xr, xi)
        hr, hi = enter_fn(pr, pi, tl)
        for k in range(steps):
            rows, xr, xi = x_block(k)
            hr, hi = _cmuladd(lr, li, hr, hi, xr, xi)
            bu_ref[b, rows, re] = hr
            bu_ref[b, rows, im] = hi
        leave_fn(hr, hi, tl)


def _build_blockdiag(b2_ref, ct_ref, bbig_ref, cbig_ref, group_ch, group_states):
    n_bundles, n_ch, two_width = bbig_ref.shape
    width = two_width // 2
    ch_shift = group_ch.bit_length() - 1
    st_shift = group_states.bit_length() - 1
    assert group_ch == 1 << ch_shift and group_states == 1 << st_shift
    ch_group = lax.shift_right_logical(lax.broadcasted_iota(jnp.int32, (n_ch, LANES), 0), ch_shift)
    b_lane_group = lax.shift_right_logical(lax.broadcasted_iota(jnp.int32, (n_ch, LANES), 1),
                                           st_shift)
    st_group = lax.shift_right_logical(lax.broadcasted_iota(jnp.int32, (width, LANES), 0), st_shift)
    c_lane_group = lax.shift_right_logical(lax.broadcasted_iota(jnp.int32, (width, LANES), 1),
                                           ch_shift)
    for b in range(n_bundles):
        for part in range(2):
            src = b2_ref[part, b * n_ch:(b + 1) * n_ch, :]
            for j in range(width // LANES):
                keep = ch_group == b_lane_group + j * (LANES // group_states)
                lanes = slice(part * width + j * LANES, part * width + (j + 1) * LANES)
                bbig_ref[b, :, lanes] = jnp.where(keep, src, 0.0).astype(BF16)
            src = ct_ref[part, b * width:(b + 1) * width, :]
            for j in range(n_ch // LANES):
                keep = st_group == c_lane_group + j * (LANES // group_ch)
                cbig_ref[b, part * width:(part + 1) * width, j * LANES:(j + 1) * LANES] = (
                    jnp.where(keep, src, 0.0).astype(BF16))


def _ssm_kernel(u_ref, sza_ref, h0re_ref, h0im_ref, lam_ref, tabq_ref, d_ref, b2_ref, ct_ref,
                wglu_ref, wouta_ref,
                ya_ref, spre_ref, spim_ref, ssre_ref, ssim_ref,
                bu_ref, uslab_ref, yslab_ref, carry_ref, bbig_ref, cbig_ref,
                *, n_prompt_tiles, tiles_per_seq, sample_steps, group_ch, group_states):
    i = pl.program_id(0)
    n_bundles = bbig_ref.shape[0]
    width = bbig_ref.shape[2] // 2
    is_prompt = i < n_prompt_tiles
    t_in_seq = i % tiles_per_seq

    @pl.when(i == 0)
    def _():
        _build_blockdiag(b2_ref, ct_ref, bbig_ref, cbig_ref, group_ch, group_states)

    @pl.when(jnp.logical_and(is_prompt, t_in_seq == 0))
    def _():
        carry_ref[...] = jnp.zeros_like(carry_ref)

    def run(n_groups, steps, enter_factory, leave_factory):
        u_sm = _to_stepmajor(u_ref[...].astype(F32), uslab_ref, n_groups, steps).astype(BF16)

        def b_matmul(b):
            ch = slice(b * SSM_BUNDLE_CH, (b + 1) * SSM_BUNDLE_CH)
            bu_ref[b] = jnp.dot(u_sm[:, ch], bbig_ref[b], preferred_element_type=F32)

        b_matmul(0)
        for b in range(n_bundles):
            if b + 1 < n_bundles:
                b_matmul(b + 1)
            for g in range(n_groups):
                _scan_group(bu_ref, b, g * SUBLANES * steps, steps, width, b * width, lam_ref,
                            enter_factory(g), leave_factory(g))
            y_b = jnp.dot(bu_ref[b].astype(BF16), cbig_ref[b], preferred_element_type=F32)
            _store_stepmajor(y_b, yslab_ref, b * (SSM_BUNDLE_CH // LANES), n_groups, steps)
        y = _load_natural(yslab_ref, n_groups, steps) + d_ref[...] * u_ref[...].astype(F32)
        g_act = _gelu_tanh(y)
        z = jnp.dot(g_act.astype(BF16), wglu_ref[...], preferred_element_type=F32)
        o = g_act * _sigmoid(z) * sza_ref[...].astype(F32)
        ya = jnp.dot(o.astype(BF16), wouta_ref[...], preferred_element_type=F32)
        ya_ref[...] = ya.astype(BF16)

    def prompt_enter(_g):
        def enter(er, ei, tl):
            xr, xi = er, ei
            for step, shift in enumerate((1, 2, 4)):
                xr, xi = _cmuladd(tabq_ref[2 * step, :, tl], tabq_ref[2 * step + 1, :, tl],
                                  pltpu.roll(xr, shift, 0), pltpu.roll(xi, shift, 0), xr, xi)
            c0r = jnp.broadcast_to(carry_ref[0:1, tl], xr.shape)
            c0i = jnp.broadcast_to(carry_ref[1:2, tl], xi.shape)
            xr, xi = _cmuladd(tabq_ref[6, :, tl], tabq_ref[7, :, tl], c0r, c0i, xr, xi)
            carry_ref[0:1, tl] = xr[SUBLANES - 1:SUBLANES, :]
            carry_ref[1:2, tl] = xi[SUBLANES - 1:SUBLANES, :]
            first = lax.broadcasted_iota(jnp.int32, xr.shape, 0) == 0
            return (jnp.where(first, c0r, pltpu.roll(xr, 1, 0)),
                    jnp.where(first, c0i, pltpu.roll(xi, 1, 0)))
        return enter

    def prompt_leave(_g):
        return lambda hr, hi, tl: None

    def sample_enter(g):
        rows = slice(g * SUBLANES, (g + 1) * SUBLANES)
        return lambda er, ei, tl: (h0re_ref[rows, tl], h0im_ref[rows, tl])

    def sample_leave(g):
        rows = slice(g * SUBLANES, (g + 1) * SUBLANES)

        def leave(hr, hi, tl):
            ssre_ref[rows, tl] = hr
            ssim_ref[rows, tl] = hi
        return leave

    @pl.when(is_prompt)
    def _():
        run(1, TILE_M // SUBLANES, prompt_enter, prompt_leave)

    @pl.when(jnp.logical_not(is_prompt))
    def _():
        run(TILE_M // (SUBLANES * sample_steps), sample_steps, sample_enter, sample_leave)

    @pl.when(jnp.logical_and(is_prompt, t_in_seq == tiles_per_seq - 1))
    def _():
        row = pl.ds(i // tiles_per_seq, 1)
        spre_ref[row, :] = carry_ref[0:1, :]
        spim_ref[row, :] = carry_ref[1:2, :]


def _ssm_call(u, sza, h0re, h0im, lam8, tabq, d, b2, ct, w_glu, w_out_a, n_prompt_rows,
              seq_len, sample_steps, group_ch, group_states):
    m, d_ssm = u.shape
    d_model = w_out_a.shape[1]
    n_state = tabq.shape[2]
    n_tiles = m // TILE_M
    n_p = n_prompt_rows // TILE_M
    n_batch = n_prompt_rows // seq_len
    seqs_per_tile = TILE_M // sample_steps
    n_sample_seq = h0re.shape[0]
    slab_rows = SUBLANES * _pitch(TILE_M // SUBLANES)
    n_bundles = d_ssm // SSM_BUNDLE_CH
    width = n_state // n_bundles
    kernel = functools.partial(_ssm_kernel, n_prompt_tiles=n_p,
                               tiles_per_seq=seq_len // TILE_M, sample_steps=sample_steps,
                               group_ch=group_ch, group_states=group_states)
    sample_map = lambda i: (jnp.maximum(i - n_p, 0), 0)
    return pl.pallas_call(
        kernel,
        grid=(n_tiles,),
        in_specs=[
            pl.BlockSpec((TILE_M, d_ssm), lambda i: (i, 0)),
            pl.BlockSpec((TILE_M, d_ssm), lambda i: (i, 0)),
            pl.BlockSpec((seqs_per_tile, n_state), sample_map),
            pl.BlockSpec((seqs_per_tile, n_state), sample_map),
            _const_spec(lam8.shape),
            _const_spec(tabq.shape),
            _const_spec(d.shape),
            _const_spec(b2.shape),
            _const_spec(ct.shape),
            _const_spec(w_glu.shape),
            _const_spec(w_out_a.shape),
        ],
        out_specs=[
            pl.BlockSpec((TILE_M, d_model), lambda i: (i, 0)),
            pl.BlockSpec((n_batch, n_state), lambda i: (0, 0)),
            pl.BlockSpec((n_batch, n_state), lambda i: (0, 0)),
            pl.BlockSpec((seqs_per_tile, n_state), sample_map),
            pl.BlockSpec((seqs_per_tile, n_state), sample_map),
        ],
        out_shape=[
            jax.ShapeDtypeStruct((m, d_model), BF16),
            jax.ShapeDtypeStruct((n_batch, n_state), F32),
            jax.ShapeDtypeStruct((n_batch, n_state), F32),
            jax.ShapeDtypeStruct((n_sample_seq, n_state), F32),
            jax.ShapeDtypeStruct((n_sample_seq, n_state), F32),
        ],
        scratch_shapes=[
            pltpu.VMEM((n_bundles, TILE_M, 2 * width), F32),
            pltpu.VMEM((d_ssm // LANES, slab_rows, LANES), F32),
            pltpu.VMEM((d_ssm // LANES, slab_rows, LANES), F32),
            pltpu.VMEM((2, n_state), F32),
            pltpu.VMEM((n_bundles, SSM_BUNDLE_CH, 2 * width), BF16),
            pltpu.VMEM((n_bundles, 2 * width, SSM_BUNDLE_CH), BF16),
        ],
        compiler_params=pltpu.CompilerParams(
            dimension_semantics=("arbitrary",), vmem_limit_bytes=VMEM_LIMIT_BYTES),
        name="ssm_branch",
    )(u, sza, h0re, h0im, lam8, tabq, d, b2, ct, w_glu, w_out_a)


def _tail_kernel(q_ref, bz_ref, ya_ref, sga_ref, sgc_ref, xp_ref, xs_ref, hist_ref,
                 cw_ref, woutc_ref, wo_ref, lng_ref, lnb_ref,
                 yp_ref, ys_ref,
                 pad_ref,
                 *, n_prompt_tiles, tiles_per_seq, alpha):
    i = pl.program_id(0)
    is_prompt = i < n_prompt_tiles

    @pl.when(jnp.logical_and(is_prompt, i % tiles_per_seq == 0))
    def _():
        pad_ref[0:CONV_PAD, :] = jnp.zeros((CONV_PAD, pad_ref.shape[1]), F32)

    q = q_ref[...]
    pad_ref[CONV_PAD:CONV_PAD + TILE_M, :] = q
    w0 = cw_ref[0:1, :]
    w1 = cw_ref[1:2, :]
    w2 = cw_ref[2:3, :]

    def finish(q1, q2, x_ref, y_ref):
        conv = w0 * q2 + w1 * q1 + w2 * q
        yc_in = conv * bz_ref[...].astype(F32)
        yc = jnp.dot(yc_in.astype(BF16), woutc_ref[...], preferred_element_type=F32)
        merged = (sga_ref[...].astype(F32) * ya_ref[...].astype(F32)
                  + sgc_ref[...].astype(F32) * yc)
        out = jnp.dot(merged.astype(BF16), wo_ref[...], preferred_element_type=F32)
        r = alpha * x_ref[...] + out
        mu = jnp.mean(r, axis=-1, keepdims=True)
        rc = r - mu
        var = jnp.mean(rc * rc, axis=-1, keepdims=True)
        y_ref[...] = rc * lax.rsqrt(var + LN_EPS) * lng_ref[...] + lnb_ref[...]

    @pl.when(is_prompt)
    def _():
        q1 = pad_ref[CONV_PAD - 1:CONV_PAD - 1 + TILE_M, :]
        q2 = pad_ref[CONV_PAD - 2:CONV_PAD - 2 + TILE_M, :]
        finish(q1, q2, xp_ref, yp_ref)
        pad_ref[0:CONV_PAD, :] = pad_ref[TILE_M:TILE_M + CONV_PAD, :]

    @pl.when(jnp.logical_not(is_prompt))
    def _():
        t = lax.broadcasted_iota(jnp.int32, q.shape, 0) % SUBLANES
        d_conv = q.shape[1]

        def history(j):
            return jnp.concatenate(
                [jnp.broadcast_to(hist_ref[s:s + 1, j * d_conv:(j + 1) * d_conv],
                                  (SUBLANES, d_conv)) for s in range(TILE_M // SUBLANES)], axis=0)

        e0 = history(0)
        e1 = history(1)
        q1 = jnp.where(t == 0, e1, pad_ref[CONV_PAD - 1:CONV_PAD - 1 + TILE_M, :])
        q2 = jnp.where(t == 0, e0,
                       jnp.where(t == 1, e1, pad_ref[CONV_PAD - 2:CONV_PAD - 2 + TILE_M, :]))
        finish(q1, q2, xs_ref, ys_ref)


def _tail_call(q, bz, ya, sga, sgc, xp, xs, hist, conv_w, w_out_c, w_o, ln_g, ln_b,
               seq_len, alpha):
    m, d_conv = q.shape
    m_p, d_model = xp.shape
    m_s = xs.shape[0]
    n_p = m_p // TILE_M
    kernel = functools.partial(_tail_kernel, n_prompt_tiles=n_p,
                               tiles_per_seq=seq_len // TILE_M, alpha=alpha)
    prompt_map = lambda i: (jnp.minimum(i, n_p - 1), 0)
    sample_map = lambda i: (jnp.maximum(i - n_p, 0), 0)
    row_map = lambda i: (i, 0)
    return pl.pallas_call(
        kernel,
        grid=(m // TILE_M,),
        in_specs=[
            pl.BlockSpec((TILE_M, d_conv), row_map),
            pl.BlockSpec((TILE_M, d_conv), row_map),
            pl.BlockSpec((TILE_M, d_model), row_map),
            pl.BlockSpec((TILE_M, d_model), row_map),
            pl.BlockSpec((TILE_M, d_model), row_map),
            pl.BlockSpec((TILE_M, d_model), prompt_map),
            pl.BlockSpec((TILE_M, d_model), sample_map),
            pl.BlockSpec((TILE_M // SUBLANES, hist.shape[1]), sample_map),
            _const_spec(conv_w.shape),
            _const_spec(w_out_c.shape),
            _const_spec(w_o.shape),
            _const_spec(ln_g.shape),
            _const_spec(ln_b.shape),
        ],
        out_specs=[
            pl.BlockSpec((TILE_M, d_model), prompt_map),
            pl.BlockSpec((TILE_M, d_model), sample_map),
        ],
        out_shape=[
            jax.ShapeDtypeStruct((m_p, d_model), F32),
            jax.ShapeDtypeStruct((m_s, d_model), F32),
        ],
        scratch_shapes=[
            pltpu.VMEM((TILE_M + CONV_PAD, d_conv), F32),
        ],
        compiler_params=pltpu.CompilerParams(
            dimension_semantics=("arbitrary",), vmem_limit_bytes=VMEM_LIMIT_BYTES),
        name="tail",
    )(q, bz, ya, sga, sgc, xp, xs, hist, conv_w, w_out_c, w_o, ln_g, ln_b)


def _ssm_params(a_re, a_im, log_dt, b_re, b_im, c_re, c_im, prompt_steps):
    g, p, gc = b_re.shape
    dt = jnp.exp(log_dt)[:, None]
    mag = jnp.exp(a_re * dt)
    ang = a_im * dt
    lam_re = mag * jnp.cos(ang)
    lam_im = mag * jnp.sin(ang)
    den = a_re * a_re + a_im * a_im
    q_re = ((lam_re - 1.0) * a_re + lam_im * a_im) / den
    q_im = (lam_im * a_re - (lam_re - 1.0) * a_im) / den
    bb_re = q_re[..., None] * b_re - q_im[..., None] * b_im
    bb_im = q_re[..., None] * b_im + q_im[..., None] * b_re

    def channel_rows(bb):
        rows = bb.transpose(0, 2, 1).reshape(g * gc, p)
        return jnp.tile(rows, (1, LANES // p))

    def state_rows(cc):
        rows = cc.transpose(0, 2, 1).reshape(g * p, gc)
        return jnp.tile(rows, (1, LANES // gc))

    b2 = jnp.stack([channel_rows(bb_re), channel_rows(bb_im)])
    ct = jnp.stack([state_rows(c_re), -state_rows(c_im)])

    la = (a_re * dt).reshape(1, -1)
    an = ang.reshape(1, -1)

    def power(e):
        m_e = jnp.exp(e * la)
        return m_e * jnp.cos(e * an), m_e * jnp.sin(e * an)

    lam8 = jnp.stack([jnp.broadcast_to(lam_re.reshape(1, -1), (SUBLANES, g * p)),
                      jnp.broadcast_to(lam_im.reshape(1, -1), (SUBLANES, g * p))])

    row = jnp.arange(SUBLANES)[:, None]
    tabs = []
    for shift in (1, 2, 4):
        qr, qi = power(jnp.full((1, 1), float(shift * prompt_steps), F32))
        tabs.append(jnp.where(row >= shift, qr, 0.0))
        tabs.append(jnp.where(row >= shift, qi, 0.0))
    tabs += list(power((row + 1).astype(F32) * float(prompt_steps)))
    return b2, ct, lam8.astype(F32), jnp.stack(tabs).astype(F32)


def kernel(x_prompt, x_sample, state_ssm_re, state_ssm_im, state_conv, w_in, ssm_a_re, ssm_a_im, ssm_log_dt, ssm_b_re, ssm_b_im, ssm_c_re, ssm_c_im, ssm_d, w_glu, w_out_a, conv_w, w_out_c, w_o, ln_g, ln_b):
    depth = w_in.shape[0]
    assert depth == 1, "single-layer trunk"
    batch, seq, d_model = x_prompt.shape
    dec_batch, dec_seq, _ = x_sample.shape
    assert dec_seq == SUBLANES and seq % TILE_M == 0 and (dec_batch * dec_seq) % TILE_M == 0
    g, p, gc = ssm_b_re.shape[1:]
    d_ssm = g * gc
    d_conv = conv_w.shape[2]
    n_state = g * p
    alpha = (2 * depth) ** 0.25

    xp = x_prompt.reshape(batch * seq, d_model)
    xs = x_sample.reshape(dec_batch * dec_seq, d_model)
    m_p = xp.shape[0]
    w = w_in[0]

    u, sza = _proj_call(xp, xs, w, (0, 1), _epi_ssm_in, (d_ssm, d_ssm), (BF16, BF16), "proj_ssm")
    (bz,) = _proj_call(xp, xs, w, (2, 5), _epi_conv_gate, (d_conv,), (BF16,), "proj_conv_gate")
    (q,) = _proj_call(xp, xs, w, (3, 4), _epi_conv_in, (d_conv,), (F32,), "proj_conv_in")
    (sga,) = _proj_call(xp, xs, w, (6, 7), _epi_merge_gate, (d_model,), (BF16,), "proj_gate_a")
    (sgc,) = _proj_call(xp, xs, w, (8, 9), _epi_merge_gate, (d_model,), (BF16,), "proj_gate_c")

    b2, ct, lam8, tabq = _ssm_params(
        ssm_a_re[0], ssm_a_im[0], ssm_log_dt[0], ssm_b_re[0], ssm_b_im[0], ssm_c_re[0],
        ssm_c_im[0], TILE_M // SUBLANES)
    h0re = state_ssm_re[0].reshape(dec_batch, n_state)
    h0im = state_ssm_im[0].reshape(dec_batch, n_state)
    ya, spre, spim, ssre, ssim = _ssm_call(
        u, sza, h0re, h0im, lam8, tabq, ssm_d[0][None, :], b2, ct,
        w_glu[0].astype(BF16), w_out_a[0].astype(BF16), m_p, seq, dec_seq, gc, p)

    hist = state_conv[0].reshape(dec_batch, (state_conv.shape[2]) * d_conv)
    yp, ys = _tail_call(q, bz, ya, sga, sgc, xp, xs, hist, conv_w[0],
                        w_out_c[0].astype(BF16), w_o[0].astype(BF16),
                        ln_g[0][None, :], ln_b[0][None, :], seq, alpha)

    q8 = q.reshape(-1, SUBLANES, d_conv)
    q_p = q8[seq // SUBLANES - 1:m_p // SUBLANES:seq // SUBLANES, SUBLANES - 2:, :]
    q_s = q8[m_p // SUBLANES:, dec_seq - 2:, :]
    return (yp.reshape(batch, seq, d_model),
            ys.reshape(dec_batch, dec_seq, d_model),
            spre.reshape(1, batch, g, p),
            spim.reshape(1, batch, g, p),
            q_p[None],
            ssre.reshape(1, dec_batch, g, p),
            ssim.reshape(1, dec_batch, g, p),
            q_s[None])
```

```python
import functools
import math

import jax
import jax.numpy as jnp
from jax import lax
from jax.experimental import pallas as pl
from jax.experimental.pallas import tpu as pltpu

F32 = jnp.float32
BF16 = jnp.bfloat16

SUBLANES = 8
LANES = 128
VMEM_LIMIT_BYTES = 56 * 1024 * 1024

TILE_M = 256
PROJ_TILE_M = 512
SSM_BUNDLE_CH = 256
SCAN_CHUNK = 4 * LANES
CONV_PAD = SUBLANES
STAGE_ROWS = 128

LN_EPS = 1e-5
GELU_C = math.sqrt(2.0 / math.pi)


def _sigmoid(x):
    return 0.5 * jnp.tanh(0.5 * x) + 0.5


def _silu(x):
    return x * _sigmoid(x)


def _gelu_tanh(x):
    return 0.5 * x * (1.0 + jnp.tanh(GELU_C * (x + 0.044715 * (x * x * x))))


def _const_spec(shape):
    nd = len(shape)
    return pl.BlockSpec(shape, lambda i: (0,) * nd, pipeline_mode=pl.Buffered(1))


def _proj_kernel(xp_ref, xs_ref, w0_ref, w1_ref, *rest, n_prompt_tiles, epilogue, n_out):
    out_refs = rest[:n_out]
    wbf_ref = rest[n_out]
    i = pl.program_id(0)

    @pl.when(i == 0)
    def _():
        wbf_ref[0] = w0_ref[...].astype(BF16)
        wbf_ref[1] = w1_ref[...].astype(BF16)

    def compute(x_ref):
        xb = x_ref[...].astype(BF16)
        a = jnp.dot(xb, wbf_ref[0], preferred_element_type=F32)
        b = jnp.dot(xb, wbf_ref[1], preferred_element_type=F32)
        epilogue(a, b, out_refs)

    @pl.when(i < n_prompt_tiles)
    def _():
        compute(xp_ref)

    @pl.when(i >= n_prompt_tiles)
    def _():
        compute(xs_ref)


def _epi_ssm_in(a, b, out_refs):
    out_refs[0][...] = a.astype(BF16)
    out_refs[1][...] = _silu(b).astype(BF16)


def _epi_conv_gate(a, b, out_refs):
    out_refs[0][...] = (a * _silu(b)).astype(BF16)


def _epi_conv_in(a, b, out_refs):
    out_refs[0][...] = a * b


def _epi_merge_gate(a, b, out_refs):
    half = a.shape[1]
    out_refs[0][:, :half] = _sigmoid(a).astype(BF16)
    out_refs[0][:, half:] = _sigmoid(b).astype(BF16)


def _proj_call(xp, xs, w_in, col_blocks, epilogue, out_widths, out_dtypes, name):
    m_p, d = xp.shape
    m_s = xs.shape[0]
    tile = PROJ_TILE_M
    n_p, n_s = m_p // tile, m_s // tile
    wcol = 1024
    c0, c1 = col_blocks
    kernel = functools.partial(_proj_kernel, n_prompt_tiles=n_p, epilogue=epilogue,
                               n_out=len(out_widths))
    return pl.pallas_call(
        kernel,
        grid=(n_p + n_s,),
        in_specs=[
            pl.BlockSpec((tile, d), lambda i: (jnp.minimum(i, n_p - 1), 0)),
            pl.BlockSpec((tile, d), lambda i: (jnp.maximum(i - n_p, 0), 0)),
            pl.BlockSpec((d, wcol), lambda i: (0, c0), pipeline_mode=pl.Buffered(1)),
            pl.BlockSpec((d, wcol), lambda i: (0, c1), pipeline_mode=pl.Buffered(1)),
        ],
        out_specs=[pl.BlockSpec((tile, w), lambda i: (i, 0)) for w in out_widths],
        out_shape=[jax.ShapeDtypeStruct((m_p + m_s, w), dt)
                   for w, dt in zip(out_widths, out_dtypes)],
        scratch_shapes=[pltpu.VMEM((2, d, wcol), BF16)],
        compiler_params=pltpu.CompilerParams(
            dimension_semantics=("arbitrary",), vmem_limit_bytes=VMEM_LIMIT_BYTES),
        name=name,
    )(xp, xs, w_in, w_in)


def _pitch(steps):
    return steps if (steps // SUBLANES) % 2 == 1 else steps + SUBLANES


def _to_stepmajor(val, slab_ref, n_groups, steps):
    pitch = _pitch(steps)
    n_slabs = val.shape[1] // LANES
    n_sub = n_groups * SUBLANES
    for j in range(n_slabs):
        lanes = slice(j * LANES, (j + 1) * LANES)
        if pitch == steps:
            slab_ref[j, 0:n_sub * steps, :] = val[:, lanes]
        else:
            for s in range(n_sub):
                slab_ref[j, s * pitch:s * pitch + steps, :] = val[s * steps:(s + 1) * steps, lanes]
    blocks = []
    for g in range(n_groups):
        for k in range(steps):
            rows = pl.ds(g * SUBLANES * pitch + k, SUBLANES, stride=pitch)
            blocks.append(jnp.concatenate([slab_ref[j, rows, :] for j in range(n_slabs)], axis=1))
    return jnp.concatenate(blocks, axis=0)


def _store_stepmajor(val, slab_ref, slab0, steps, block0):
    pitch = _pitch(steps)
    for i in range(val.shape[0] // SUBLANES):
        g, k = divmod(block0 + i, steps)
        rows = pl.ds(g * SUBLANES * pitch + k, SUBLANES, stride=pitch)
        for j in range(val.shape[1] // LANES):
            slab_ref[slab0 + j, rows, :] = val[i * SUBLANES:(i + 1) * SUBLANES,
                                               j * LANES:(j + 1) * LANES]


def _load_natural(slab_ref, n_groups, steps):
    pitch = _pitch(steps)
    n_slabs = slab_ref.shape[0]
    n_sub = n_groups * SUBLANES
    if pitch == steps:
        return jnp.concatenate([slab_ref[j, 0:n_sub * steps, :] for j in range(n_slabs)], axis=1)
    return jnp.concatenate(
        [jnp.concatenate([slab_ref[j, s * pitch:s * pitch + steps, :] for j in range(n_slabs)],
                         axis=1) for s in range(n_sub)], axis=0)


def _cmuladd(ar, ai, br, bi, cr, ci):
    return ar * br - ai * bi + cr, ar * bi + ai * br + ci


def _scan_group(bu_ref, b, row0, steps, width, lane0, lam_ref, enter_fn, leave_fn):
    for c in range(width // SCAN_CHUNK):
        re = slice(c * SCAN_CHUNK, (c + 1) * SCAN_CHUNK)
        im = slice(width + c * SCAN_CHUNK, width + (c + 1) * SCAN_CHUNK)
        tl = slice(lane0 + c * SCAN_CHUNK, lane0 + (c + 1) * SCAN_CHUNK)
        lr, li = lam_ref[0][:, tl], lam_ref[1][:, tl]

        def x_block(k):
            rows = slice(row0 + k * SUBLANES, row0 + (k + 1) * SUBLANES)
            return rows, bu_ref[b, rows, re], bu_ref[b, rows, im]

        _, pr, pi = x_block(0)
        for k in range(1, steps):
            _, xr, xi = x_block(k)
            pr, pi = _cmuladd(lr, li, pr, pi, xr, xi)
        hr, hi = enter_fn(pr, pi, tl)
        for k in range(steps):
            rows, xr, xi = x_block(k)
            hr, hi = _cmuladd(lr, li, hr, hi, xr, xi)
            bu_ref[b, rows, re] = hr
            bu_ref[b, rows, im] = hi
        leave_fn(hr, hi, tl)


def _build_blockdiag(b2_ref, ct_ref, bbig_ref, cbig_ref, group_ch, group_states):
    n_bundles, n_ch, two_width = bbig_ref.shape
    width = two_width // 2
    ch_shift = group_ch.bit_length() - 1
    st_shift = group_states.bit_length() - 1
    assert group_ch == 1 << ch_shift and group_states == 1 << st_shift
    ch_group = lax.shift_right_logical(lax.broadcasted_iota(jnp.int32, (n_ch, LANES), 0), ch_shift)
    b_lane_group = lax.shift_right_logical(lax.broadcasted_iota(jnp.int32, (n_ch, LANES), 1),
                                           st_shift)
    st_group = lax.shift_right_logical(lax.broadcasted_iota(jnp.int32, (width, LANES), 0), st_shift)
    c_lane_group = lax.shift_right_logical(lax.broadcasted_iota(jnp.int32, (width, LANES), 1),
                                           ch_shift)
    for b in range(n_bundles):
        for part in range(2):
            src = b2_ref[part][b * n_ch:(b + 1) * n_ch, :]
            for j in range(width // LANES):
                keep = ch_group == b_lane_group + j * (LANES // group_states)
                lanes = slice(part * width + j * LANES, part * width + (j + 1) * LANES)
                bbig_ref[b, :, lanes] = jnp.where(keep, src, 0.0).astype(BF16)
            src = ct_ref[part][b * width:(b + 1) * width, :]
            for j in range(n_ch // LANES):
                keep = st_group == c_lane_group + j * (LANES // group_ch)
                cbig_ref[b, part * width:(part + 1) * width, j * LANES:(j + 1) * LANES] = (
                    jnp.where(keep, src, 0.0).astype(BF16))


def _stage_weight_bf16(w_hbm, col0, dst_ref, stage_ref, sem_ref):
    k, n = dst_ref.shape
    rows = stage_ref.shape[1]
    n_chunks = k // rows

    def copy(c):
        return pltpu.make_async_copy(w_hbm.at[pl.ds(c * rows, rows), pl.ds(col0, n)],
                                     stage_ref.at[c % 2], sem_ref.at[c % 2])

    copy(0).start()
    for c in range(n_chunks):
        if c + 1 < n_chunks:
            copy(c + 1).start()
        copy(c).wait()
        dst_ref[c * rows:(c + 1) * rows, :] = stage_ref[c % 2].astype(BF16)


def _ssm_kernel(u_ref, sza_ref, h0re_ref, h0im_ref, lamre_ref, lamim_ref, tqre_ref, tqim_ref, d_ref,
                b2re_ref, b2im_ref, ctre_ref, ctim_ref, wglu_hbm, wouta_hbm,
                ya_ref, spre_ref, spim_ref, ssre_ref, ssim_ref,
                bu_ref, uslab_ref, yslab_ref, carry_ref, bbig_ref, cbig_ref,
                wglu_ref, wouta_ref, stage_narrow_ref, sem_ref,
                *, n_prompt_tiles, tiles_per_seq, sample_steps, group_ch, group_states):
    i = pl.program_id(0)
    n_bundles = bbig_ref.shape[0]
    width = bbig_ref.shape[2] // 2
    is_prompt = i < n_prompt_tiles
    t_in_seq = i % tiles_per_seq

    @pl.when(i == 0)
    def _():
        _stage_weight_bf16(wouta_hbm, 0, wouta_ref, bu_ref, sem_ref)
        _stage_weight_bf16(wglu_hbm, 0, wglu_ref, stage_narrow_ref, sem_ref)
        _build_blockdiag((b2re_ref, b2im_ref), (ctre_ref, ctim_ref), bbig_ref, cbig_ref,
                         group_ch, group_states)

    @pl.when(jnp.logical_and(is_prompt, t_in_seq == 0))
    def _():
        carry_ref[...] = jnp.zeros_like(carry_ref)

    def run(n_groups, steps, enter_factory, leave_factory):
        u_sm = _to_stepmajor(u_ref[...].astype(F32), uslab_ref, n_groups, steps).astype(BF16)

        def b_matmul(b):
            ch = slice(b * SSM_BUNDLE_CH, (b + 1) * SSM_BUNDLE_CH)
            bu_ref[b] = jnp.dot(u_sm[:, ch], bbig_ref[b], preferred_element_type=F32)

        b_matmul(0)
        for b in range(n_bundles):
            if b + 1 < n_bundles:
                b_matmul(b + 1)
            for g in range(n_groups):
                _scan_group(bu_ref, b, g * SUBLANES * steps, steps, width, b * width,
                            (lamre_ref, lamim_ref), enter_factory(g), leave_factory(g))
            y_b = jnp.dot(bu_ref[b].astype(BF16), cbig_ref[b], preferred_element_type=F32)
            _store_stepmajor(y_b, yslab_ref, b * (SSM_BUNDLE_CH // LANES), steps, 0)
        y = _load_natural(yslab_ref, n_groups, steps) + d_ref[...] * u_ref[...].astype(F32)
        g_act = _gelu_tanh(y)
        z = jnp.dot(g_act.astype(BF16), wglu_ref[...], preferred_element_type=F32)
        o = g_act * _sigmoid(z) * sza_ref[...].astype(F32)
        ya = jnp.dot(o.astype(BF16), wouta_ref[...], preferred_element_type=F32)
        ya_ref[...] = ya.astype(BF16)

    def prompt_enter(_g):
        def enter(er, ei, tl):
            xr, xi = er, ei
            for step, shift in enumerate((1, 2, 4)):
                xr, xi = _cmuladd(tqre_ref[step, :, tl], tqim_ref[step, :, tl],
                                  pltpu.roll(xr, shift, 0), pltpu.roll(xi, shift, 0), xr, xi)
            c0r = jnp.broadcast_to(carry_ref[0:1, tl], xr.shape)
            c0i = jnp.broadcast_to(carry_ref[1:2, tl], xi.shape)
            xr, xi = _cmuladd(tqre_ref[3, :, tl], tqim_ref[3, :, tl], c0r, c0i, xr, xi)
            carry_ref[0:1, tl] = xr[SUBLANES - 1:SUBLANES, :]
            carry_ref[1:2, tl] = xi[SUBLANES - 1:SUBLANES, :]
            first = lax.broadcasted_iota(jnp.int32, xr.shape, 0) == 0
            return (jnp.where(first, c0r, pltpu.roll(xr, 1, 0)),
                    jnp.where(first, c0i, pltpu.roll(xi, 1, 0)))
        return enter

    def prompt_leave(_g):
        return lambda hr, hi, tl: None

    def sample_enter(g):
        rows = slice(g * SUBLANES, (g + 1) * SUBLANES)
        return lambda er, ei, tl: (h0re_ref[rows, tl], h0im_ref[rows, tl])

    def sample_leave(g):
        rows = slice(g * SUBLANES, (g + 1) * SUBLANES)

        def leave(hr, hi, tl):
            ssre_ref[rows, tl] = hr
            ssim_ref[rows, tl] = hi
        return leave

    @pl.when(is_prompt)
    def _():
        run(1, TILE_M // SUBLANES, prompt_enter, prompt_leave)

    @pl.when(jnp.logical_not(is_prompt))
    def _():
        run(TILE_M // (SUBLANES * sample_steps), sample_steps, sample_enter, sample_leave)

    @pl.when(jnp.logical_and(is_prompt, t_in_seq == tiles_per_seq - 1))
    def _():
        row = pl.ds(i // tiles_per_seq, 1)
        spre_ref[row, :] = carry_ref[0:1, :]
        spim_ref[row, :] = carry_ref[1:2, :]


def _ssm_call(u, sza, h0re, h0im, lam8, tabq, d, b2, ct, w_glu, w_out_a, n_prompt_rows, seq_len,
              sample_steps, group_ch, group_states):
    m, d_ssm = u.shape
    d_model = w_out_a.shape[1]
    n_state = lam8[0].shape[1]
    n_tiles = m // TILE_M
    n_p = n_prompt_rows // TILE_M
    n_batch = n_prompt_rows // seq_len
    seqs_per_tile = TILE_M // sample_steps
    n_sample_seq = h0re.shape[0]
    slab_rows = SUBLANES * _pitch(TILE_M // SUBLANES)
    n_bundles = d_ssm // SSM_BUNDLE_CH
    width = n_state // n_bundles
    assert 2 * width == d_model and n_bundles >= 2 and d_ssm % TILE_M == 0
    kernel = functools.partial(_ssm_kernel, n_prompt_tiles=n_p,
                               tiles_per_seq=seq_len // TILE_M, sample_steps=sample_steps,
                               group_ch=group_ch, group_states=group_states)
    tile_map = lambda i: (i, 0)
    sample_map = lambda i: (jnp.maximum(i - n_p, 0), 0)
    hbm = pl.BlockSpec(memory_space=pl.ANY)
    return pl.pallas_call(
        kernel,
        grid=(n_tiles,),
        in_specs=[
            pl.BlockSpec((TILE_M, d_ssm), tile_map),
            pl.BlockSpec((TILE_M, d_ssm), tile_map),
            pl.BlockSpec((seqs_per_tile, n_state), sample_map),
            pl.BlockSpec((seqs_per_tile, n_state), sample_map),
            _const_spec(lam8[0].shape), _const_spec(lam8[1].shape),
            _const_spec(tabq[0].shape), _const_spec(tabq[1].shape),
            _const_spec(d.shape),
            _const_spec(b2[0].shape), _const_spec(b2[1].shape),
            _const_spec(ct[0].shape), _const_spec(ct[1].shape),
            hbm, hbm,
        ],
        out_specs=[
            pl.BlockSpec((TILE_M, d_model), tile_map),
            pl.BlockSpec((n_batch, n_state), lambda i: (0, 0)),
            pl.BlockSpec((n_batch, n_state), lambda i: (0, 0)),
            pl.BlockSpec((seqs_per_tile, n_state), sample_map),
            pl.BlockSpec((seqs_per_tile, n_state), sample_map),
        ],
        out_shape=[
            jax.ShapeDtypeStruct((m, d_model), BF16),
            jax.ShapeDtypeStruct((n_batch, n_state), F32),
            jax.ShapeDtypeStruct((n_batch, n_state), F32),
            jax.ShapeDtypeStruct((n_sample_seq, n_state), F32),
            jax.ShapeDtypeStruct((n_sample_seq, n_state), F32),
        ],
        scratch_shapes=[
            pltpu.VMEM((n_bundles, TILE_M, 2 * width), F32),
            pltpu.VMEM((d_ssm // LANES, slab_rows, LANES), F32),
            pltpu.VMEM((d_ssm // LANES, slab_rows, LANES), F32),
            pltpu.VMEM((2, n_state), F32),
            pltpu.VMEM((n_bundles, SSM_BUNDLE_CH, 2 * width), BF16),
            pltpu.VMEM((n_bundles, 2 * width, SSM_BUNDLE_CH), BF16),
            pltpu.VMEM((d_ssm, d_ssm), BF16),
            pltpu.VMEM((d_ssm, d_model), BF16),
            pltpu.VMEM((2, STAGE_ROWS, d_ssm), F32),
            pltpu.SemaphoreType.DMA((2,)),
        ],
        compiler_params=pltpu.CompilerParams(
            dimension_semantics=("arbitrary",), vmem_limit_bytes=VMEM_LIMIT_BYTES),
        name="ssm_branch",
    )(u, sza, h0re, h0im, *lam8, *tabq, d, *b2, *ct, w_glu, w_out_a)


def _tail_kernel(q_ref, bz_ref, ya_ref, sga_ref, sgc_ref, xp_ref, xs_ref, hist_ref,
                 cw_ref, lng_ref, lnb_ref, woutc_hbm, wo_hbm,
                 yp_ref, ys_ref,
                 pad_ref, conv_ref, yc_ref, woutc_ref, wo_ref, stage_ref, sem_ref,
                 *, n_prompt_tiles, tiles_per_seq, alpha):
    i = pl.program_id(0)
    is_prompt = i < n_prompt_tiles

    @pl.when(i == 0)
    def _():
        _stage_weight_bf16(woutc_hbm, 0, woutc_ref, stage_ref, sem_ref)
        _stage_weight_bf16(wo_hbm, 0, wo_ref, stage_ref, sem_ref)

    @pl.when(jnp.logical_and(is_prompt, i % tiles_per_seq == 0))
    def _():
        pad_ref[0:CONV_PAD, :] = jnp.zeros((CONV_PAD, pad_ref.shape[1]), F32)

    q = q_ref[...]
    pad_ref[CONV_PAD:CONV_PAD + TILE_M, :] = q
    w0 = cw_ref[0:1, :]
    w1 = cw_ref[1:2, :]
    w2 = cw_ref[2:3, :]

    def finish(q1, q2, x_ref, y_ref):
        conv_ref[...] = w0 * q2 + w1 * q1 + w2 * q
        yc_in = conv_ref[...].astype(BF16) * bz_ref[...]
        yc_ref[...] = jnp.dot(yc_in, woutc_ref[...], preferred_element_type=F32)
        merged = sga_ref[...] * ya_ref[...] + sgc_ref[...] * yc_ref[...].astype(BF16)
        out = jnp.dot(merged, wo_ref[...], preferred_element_type=F32)
        r = alpha * x_ref[...] + out
        mu = jnp.mean(r, axis=-1, keepdims=True)
        rc = r - mu
        var = jnp.mean(rc * rc, axis=-1, keepdims=True)
        y_ref[...] = rc * lax.rsqrt(var + LN_EPS) * lng_ref[...] + lnb_ref[...]

    @pl.when(is_prompt)
    def _():
        q1 = pad_ref[CONV_PAD - 1:CONV_PAD - 1 + TILE_M, :]
        q2 = pad_ref[CONV_PAD - 2:CONV_PAD - 2 + TILE_M, :]
        finish(q1, q2, xp_ref, yp_ref)
        pad_ref[0:CONV_PAD, :] = pad_ref[TILE_M:TILE_M + CONV_PAD, :]

    @pl.when(jnp.logical_not(is_prompt))
    def _():
        t = lax.broadcasted_iota(jnp.int32, q.shape, 0) % SUBLANES
        d_conv = q.shape[1]

        def history(j):
            return jnp.concatenate(
                [jnp.broadcast_to(hist_ref[s:s + 1, j * d_conv:(j + 1) * d_conv],
                                  (SUBLANES, d_conv)) for s in range(TILE_M // SUBLANES)], axis=0)

        e0 = history(0)
        e1 = history(1)
        q1 = jnp.where(t == 0, e1, pad_ref[CONV_PAD - 1:CONV_PAD - 1 + TILE_M, :])
        q2 = jnp.where(t == 0, e0,
                       jnp.where(t == 1, e1, pad_ref[CONV_PAD - 2:CONV_PAD - 2 + TILE_M, :]))
        finish(q1, q2, xs_ref, ys_ref)


def _tail_call(q, bz, ya, sga, sgc, xp, xs, hist, conv_w, w_out_c, w_o, ln_g, ln_b,
               seq_len, alpha):
    m, d_conv = q.shape
    m_p, d_model = xp.shape
    m_s = xs.shape[0]
    n_p = m_p // TILE_M
    kernel = functools.partial(_tail_kernel, n_prompt_tiles=n_p,
                               tiles_per_seq=seq_len // TILE_M, alpha=alpha)
    prompt_map = lambda i: (jnp.minimum(i, n_p - 1), 0)
    sample_map = lambda i: (jnp.maximum(i - n_p, 0), 0)
    row_map = lambda i: (i, 0)
    hbm = pl.BlockSpec(memory_space=pl.ANY)
    assert w_out_c.shape[1] == w_o.shape[1] == d_model
    return pl.pallas_call(
        kernel,
        grid=(m // TILE_M,),
        in_specs=[
            pl.BlockSpec((TILE_M, d_conv), row_map),
            pl.BlockSpec((TILE_M, d_conv), row_map),
            pl.BlockSpec((TILE_M, d_model), row_map),
            pl.BlockSpec((TILE_M, d_model), row_map),
            pl.BlockSpec((TILE_M, d_model), row_map),
            pl.BlockSpec((TILE_M, d_model), prompt_map),
            pl.BlockSpec((TILE_M, d_model), sample_map),
            pl.BlockSpec((TILE_M // SUBLANES, hist.shape[1]), sample_map),
            _const_spec(conv_w.shape),
            _const_spec(ln_g.shape),
            _const_spec(ln_b.shape),
            hbm, hbm,
        ],
        out_specs=[
            pl.BlockSpec((TILE_M, d_model), prompt_map),
            pl.BlockSpec((TILE_M, d_model), sample_map),
        ],
        out_shape=[
            jax.ShapeDtypeStruct((m_p, d_model), F32),
            jax.ShapeDtypeStruct((m_s, d_model), F32),
        ],
        scratch_shapes=[
            pltpu.VMEM((TILE_M + CONV_PAD, d_conv), F32),
            pltpu.VMEM((TILE_M, d_conv), F32),
            pltpu.VMEM((TILE_M, d_model), F32),
            pltpu.VMEM(w_out_c.shape, BF16),
            pltpu.VMEM(w_o.shape, BF16),
            pltpu.VMEM((2, STAGE_ROWS, d_model), F32),
            pltpu.SemaphoreType.DMA((2,)),
        ],
        compiler_params=pltpu.CompilerParams(
            dimension_semantics=("arbitrary",), vmem_limit_bytes=VMEM_LIMIT_BYTES),
        name="tail",
    )(q, bz, ya, sga, sgc, xp, xs, hist, conv_w, ln_g, ln_b, w_out_c, w_o)


def _ssm_params(a_re, a_im, log_dt, b_re, b_im, c_re, c_im, prompt_steps):
    g, p, gc = b_re.shape
    dt = jnp.exp(log_dt)[:, None]
    mag = jnp.exp(a_re * dt)
    ang = a_im * dt
    lam_re = mag * jnp.cos(ang)
    lam_im = mag * jnp.sin(ang)
    den = a_re * a_re + a_im * a_im
    q_re = ((lam_re - 1.0) * a_re + lam_im * a_im) / den
    q_im = (lam_im * a_re - (lam_re - 1.0) * a_im) / den
    bb_re = q_re[..., None] * b_re - q_im[..., None] * b_im
    bb_im = q_re[..., None] * b_im + q_im[..., None] * b_re

    def channel_rows(bb):
        rows = bb.transpose(0, 2, 1).reshape(g * gc, p)
        return jnp.tile(rows, (1, LANES // p))

    def state_rows(cc):
        rows = cc.transpose(0, 2, 1).reshape(g * p, gc)
        return jnp.tile(rows, (1, LANES // gc))

    b2 = (channel_rows(bb_re), channel_rows(bb_im))
    ct = (state_rows(c_re), -state_rows(c_im))

    lam8 = (jnp.broadcast_to(lam_re.reshape(1, -1), (SUBLANES, g * p)),
            jnp.broadcast_to(lam_im.reshape(1, -1), (SUBLANES, g * p)))

    row = jnp.arange(SUBLANES, dtype=F32)[None, :, None]
    shift = jnp.array([1.0, 2.0, 4.0], F32)[:, None, None]
    exponent = jnp.concatenate([jnp.broadcast_to(shift, (3, SUBLANES, 1)), row + 1.0]) * prompt_steps
    keep = jnp.concatenate([row >= shift, jnp.ones((1, SUBLANES, 1), bool)])
    mag_e = jnp.where(keep, jnp.exp(exponent * (a_re * dt).reshape(1, 1, -1)), 0.0)
    ang_e = exponent * ang.reshape(1, 1, -1)
    tabq = (mag_e * jnp.cos(ang_e), mag_e * jnp.sin(ang_e))
    return b2, ct, lam8, tabq


def kernel(x_prompt, x_sample, state_ssm_re, state_ssm_im, state_conv, w_in, ssm_a_re, ssm_a_im, ssm_log_dt, ssm_b_re, ssm_b_im, ssm_c_re, ssm_c_im, ssm_d, w_glu, w_out_a, conv_w, w_out_c, w_o, ln_g, ln_b):
    depth = w_in.shape[0]
    assert depth == 1, "single-layer trunk"
    batch, seq, d_model = x_prompt.shape
    dec_batch, dec_seq, _ = x_sample.shape
    assert dec_seq == SUBLANES and seq % TILE_M == 0 and (dec_batch * dec_seq) % TILE_M == 0
    g, p, gc = ssm_b_re.shape[1:]
    d_ssm = g * gc
    d_conv = conv_w.shape[2]
    n_state = g * p
    alpha = (2 * depth) ** 0.25

    xp = x_prompt.reshape(batch * seq, d_model)
    xs = x_sample.reshape(dec_batch * dec_seq, d_model)
    m_p = xp.shape[0]
    w = w_in[0]

    u, sza = _proj_call(xp, xs, w, (0, 1), _epi_ssm_in, (d_ssm, d_ssm), (BF16, BF16), "proj_ssm")
    (bz,) = _proj_call(xp, xs, w, (2, 5), _epi_conv_gate, (d_conv,), (BF16,), "proj_conv_gate")
    (q,) = _proj_call(xp, xs, w, (3, 4), _epi_conv_in, (d_conv,), (F32,), "proj_conv_in")
    (sga,) = _proj_call(xp, xs, w, (6, 7), _epi_merge_gate, (d_model,), (BF16,), "proj_gate_a")
    (sgc,) = _proj_call(xp, xs, w, (8, 9), _epi_merge_gate, (d_model,), (BF16,), "proj_gate_c")

    b2, ct, lam8, tabq = _ssm_params(
        ssm_a_re[0], ssm_a_im[0], ssm_log_dt[0], ssm_b_re[0], ssm_b_im[0], ssm_c_re[0],
        ssm_c_im[0], TILE_M // SUBLANES)
    h0re = state_ssm_re[0].reshape(dec_batch, n_state)
    h0im = state_ssm_im[0].reshape(dec_batch, n_state)
    ya, spre, spim, ssre, ssim = _ssm_call(
        u, sza, h0re, h0im, lam8, tabq, ssm_d[0][None, :], b2, ct, w_glu[0], w_out_a[0],
        m_p, seq, dec_seq, gc, p)

    hist = state_conv[0].reshape(dec_batch, (state_conv.shape[2]) * d_conv)
    yp, ys = _tail_call(q, bz, ya, sga, sgc, xp, xs, hist, conv_w[0],
                        w_out_c[0], w_o[0],
                        ln_g[0][None, :], ln_b[0][None, :], seq, alpha)

    q8 = q.reshape(-1, SUBLANES, d_conv)
    q_p = q8[seq // SUBLANES - 1:m_p // SUBLANES:seq // SUBLANES, SUBLANES - 2:, :]
    q_s = q8[m_p // SUBLANES:, dec_seq - 2:, :]
    return (yp.reshape(batch, seq, d_model),
            ys.reshape(dec_batch, dec_seq, d_model),
            spre.reshape(1, batch, g, p),
            spim.reshape(1, batch, g, p),
            q_p[None],
            ssre.reshape(1, dec_batch, g, p),
            ssim.reshape(1, dec_batch, g, p),
            q_s[None])
```

```python
import functools
import math

import jax
import jax.numpy as jnp
from jax import lax
from jax.experimental import pallas as pl
from jax.experimental.pallas import tpu as pltpu

F32 = jnp.float32
BF16 = jnp.bfloat16

SUBLANES = 8
LANES = 128
VMEM_LIMIT_BYTES = 56 * 1024 * 1024

TILE_M = 256
PROJ_TILE_M = 512
SSM_BUNDLE_CH = 256
SCAN_CHUNK = 4 * LANES
CONV_PAD = SUBLANES
STAGE_ROWS = 512

LN_EPS = 1e-5
GELU_C = math.sqrt(2.0 / math.pi)


def _sigmoid(x):
    return 0.5 * jnp.tanh(0.5 * x) + 0.5


def _silu(x):
    return x * _sigmoid(x)


def _gelu_tanh(x):
    return 0.5 * x * (1.0 + jnp.tanh(GELU_C * (x + 0.044715 * (x * x * x))))


def _const_spec(shape):
    nd = len(shape)
    return pl.BlockSpec(shape, lambda i: (0,) * nd, pipeline_mode=pl.Buffered(1))


def _proj_kernel(xp_ref, xs_ref, w0_ref, w1_ref, *rest, n_prompt_tiles, epilogue, n_out):
    out_refs = rest[:n_out]
    wbf_ref = rest[n_out]
    i = pl.program_id(0)

    @pl.when(i == 0)
    def _():
        wbf_ref[0] = w0_ref[...].astype(BF16)
        wbf_ref[1] = w1_ref[...].astype(BF16)

    def compute(x_ref):
        xb = x_ref[...].astype(BF16)
        a = jnp.dot(xb, wbf_ref[0], preferred_element_type=F32)
        b = jnp.dot(xb, wbf_ref[1], preferred_element_type=F32)
        epilogue(a, b, out_refs)

    @pl.when(i < n_prompt_tiles)
    def _():
        compute(xp_ref)

    @pl.when(i >= n_prompt_tiles)
    def _():
        compute(xs_ref)


def _epi_ssm_in(a, b, out_refs):
    out_refs[0][...] = a.astype(BF16)
    out_refs[1][...] = _silu(b).astype(BF16)


def _epi_conv_gate(a, b, out_refs):
    out_refs[0][...] = (a * _silu(b)).astype(BF16)


def _epi_conv_in(a, b, out_refs):
    out_refs[0][...] = a * b


def _epi_merge_gate(a, b, out_refs):
    half = a.shape[1]
    out_refs[0][:, :half] = _sigmoid(a).astype(BF16)
    out_refs[0][:, half:] = _sigmoid(b).astype(BF16)


def _proj_call(xp, xs, w_in, col_blocks, epilogue, out_widths, out_dtypes, name):
    m_p, d = xp.shape
    m_s = xs.shape[0]
    tile = PROJ_TILE_M
    n_p, n_s = m_p // tile, m_s // tile
    wcol = 1024
    c0, c1 = col_blocks
    kernel = functools.partial(_proj_kernel, n_prompt_tiles=n_p, epilogue=epilogue,
                               n_out=len(out_widths))
    return pl.pallas_call(
        kernel,
        grid=(n_p + n_s,),
        in_specs=[
            pl.BlockSpec((tile, d), lambda i: (jnp.minimum(i, n_p - 1), 0)),
            pl.BlockSpec((tile, d), lambda i: (jnp.maximum(i - n_p, 0), 0)),
            pl.BlockSpec((d, wcol), lambda i: (0, c0), pipeline_mode=pl.Buffered(1)),
            pl.BlockSpec((d, wcol), lambda i: (0, c1), pipeline_mode=pl.Buffered(1)),
        ],
        out_specs=[pl.BlockSpec((tile, w), lambda i: (i, 0)) for w in out_widths],
        out_shape=[jax.ShapeDtypeStruct((m_p + m_s, w), dt)
                   for w, dt in zip(out_widths, out_dtypes)],
        scratch_shapes=[pltpu.VMEM((2, d, wcol), BF16)],
        compiler_params=pltpu.CompilerParams(
            dimension_semantics=("arbitrary",), vmem_limit_bytes=VMEM_LIMIT_BYTES),
        name=name,
    )(xp, xs, w_in, w_in)


def _pitch(steps):
    return steps if (steps // SUBLANES) % 2 == 1 else steps + SUBLANES


def _to_stepmajor(val, slab_ref, n_groups, steps):
    pitch = _pitch(steps)
    n_slabs = val.shape[1] // LANES
    n_sub = n_groups * SUBLANES
    for j in range(n_slabs):
        lanes = slice(j * LANES, (j + 1) * LANES)
        if pitch == steps:
            slab_ref[j, 0:n_sub * steps, :] = val[:, lanes]
        else:
            for s in range(n_sub):
                slab_ref[j, s * pitch:s * pitch + steps, :] = val[s * steps:(s + 1) * steps, lanes]
    blocks = []
    for g in range(n_groups):
        for k in range(steps):
            rows = pl.ds(g * SUBLANES * pitch + k, SUBLANES, stride=pitch)
            blocks.append(jnp.concatenate([slab_ref[j, rows, :] for j in range(n_slabs)], axis=1))
    return jnp.concatenate(blocks, axis=0)


def _store_stepmajor(val, slab_ref, slab0, steps, block0):
    pitch = _pitch(steps)
    for i in range(val.shape[0] // SUBLANES):
        g, k = divmod(block0 + i, steps)
        rows = pl.ds(g * SUBLANES * pitch + k, SUBLANES, stride=pitch)
        for j in range(val.shape[1] // LANES):
            slab_ref[slab0 + j, rows, :] = val[i * SUBLANES:(i + 1) * SUBLANES,
                                               j * LANES:(j + 1) * LANES]


def _load_natural(slab_ref, n_groups, steps):
    pitch = _pitch(steps)
    n_slabs = slab_ref.shape[0]
    n_sub = n_groups * SUBLANES
    if pitch == steps:
        return jnp.concatenate([slab_ref[j, 0:n_sub * steps, :] for j in range(n_slabs)], axis=1)
    return jnp.concatenate(
        [jnp.concatenate([slab_ref[j, s * pitch:s * pitch + steps, :] for j in range(n_slabs)],
                         axis=1) for s in range(n_sub)], axis=0)


def _cmuladd(ar, ai, br, bi, cr, ci):
    return ar * br - ai * bi + cr, ar * bi + ai * br + ci


def _scan_group(bu_ref, b, row0, steps, width, lane0, lam_ref, enter_fn, leave_fn):
    for c in range(width // SCAN_CHUNK):
        re = slice(c * SCAN_CHUNK, (c + 1) * SCAN_CHUNK)
        im = slice(width + c * SCAN_CHUNK, width + (c + 1) * SCAN_CHUNK)
        tl = slice(lane0 + c * SCAN_CHUNK, lane0 + (c + 1) * SCAN_CHUNK)
        lr, li = lam_ref[0][:, tl], lam_ref[1][:, tl]

        def x_block(k):
            rows = slice(row0 + k * SUBLANES, row0 + (k + 1) * SUBLANES)
            return rows, bu_ref[b, rows, re], bu_ref[b, rows, im]

        _, pr, pi = x_block(0)
        for k in range(1, steps):
            _, xr, xi = x_block(k)
            pr, pi = _cmuladd(lr, li, pr, pi, xr, xi)
        hr, hi = enter_fn(pr, pi, tl)
        for k in range(steps):
            rows, xr, xi = x_block(k)
            hr, hi = _cmuladd(lr, li, hr, hi, xr, xi)
            bu_ref[b, rows, re] = hr
            bu_ref[b, rows, im] = hi
        leave_fn(hr, hi, tl)


def _build_blockdiag(b2_ref, ct_ref, bbig_ref, cbig_ref, group_ch, group_states):
    n_bundles, n_ch, two_width = bbig_ref.shape
    width = two_width // 2
    ch_shift = group_ch.bit_length() - 1
    st_shift = group_states.bit_length() - 1
    assert group_ch == 1 << ch_shift and group_states == 1 << st_shift
    ch_group = lax.shift_right_logical(lax.broadcasted_iota(jnp.int32, (n_ch, LANES), 0), ch_shift)
    b_lane_group = lax.shift_right_logical(lax.broadcasted_iota(jnp.int32, (n_ch, LANES), 1),
                                           st_shift)
    st_group = lax.shift_right_logical(lax.broadcasted_iota(jnp.int32, (width, LANES), 0), st_shift)
    c_lane_group = lax.shift_right_logical(lax.broadcasted_iota(jnp.int32, (width, LANES), 1),
                                           ch_shift)
    for b in range(n_bundles):
        for part in range(2):
            src = b2_ref[part][b * n_ch:(b + 1) * n_ch, :]
            for j in range(width // LANES):
                keep = ch_group == b_lane_group + j * (LANES // group_states)
                lanes = slice(part * width + j * LANES, part * width + (j + 1) * LANES)
                bbig_ref[b, :, lanes] = jnp.where(keep, src, 0.0).astype(BF16)
            src = ct_ref[part][b * width:(b + 1) * width, :]
            for j in range(n_ch // LANES):
                keep = st_group == c_lane_group + j * (LANES // group_ch)
                cbig_ref[b, part * width:(part + 1) * width, j * LANES:(j + 1) * LANES] = (
                    jnp.where(keep, src, 0.0).astype(BF16))


def _stage_weight_bf16(w_hbm, col0, dst_ref, stage_ref, sem_ref):
    k, n = dst_ref.shape
    rows = stage_ref.shape[1]
    n_chunks = k // rows

    def copy(c):
        return pltpu.make_async_copy(w_hbm.at[pl.ds(c * rows, rows), pl.ds(col0, n)],
                                     stage_ref.at[c % 2], sem_ref.at[c % 2])

    copy(0).start()
    for c in range(n_chunks):
        if c + 1 < n_chunks:
            copy(c + 1).start()
        copy(c).wait()
        dst_ref[c * rows:(c + 1) * rows, :] = stage_ref[c % 2].astype(BF16)


def _ssm_kernel(u_ref, sza_ref, h0re_ref, h0im_ref, lamre_ref, lamim_ref, tqre_ref, tqim_ref, d_ref,
                b2re_ref, b2im_ref, ctre_ref, ctim_ref, wglu_hbm, wouta_hbm,
                ya_ref, spre_ref, spim_ref, ssre_ref, ssim_ref,
                bu_ref, uslab_ref, yslab_ref, carry_ref, bbig_ref, cbig_ref,
                wglu_ref, wouta_ref, stage_narrow_ref, sem_ref,
                *, n_prompt_tiles, tiles_per_seq, sample_steps, group_ch, group_states):
    i = pl.program_id(0)
    n_bundles = bbig_ref.shape[0]
    width = bbig_ref.shape[2] // 2
    is_prompt = i < n_prompt_tiles
    t_in_seq = i % tiles_per_seq

    @pl.when(i == 0)
    def _():
        _stage_weight_bf16(wouta_hbm, 0, wouta_ref, bu_ref, sem_ref)
        _stage_weight_bf16(wglu_hbm, 0, wglu_ref, stage_narrow_ref, sem_ref)
        _build_blockdiag((b2re_ref, b2im_ref), (ctre_ref, ctim_ref), bbig_ref, cbig_ref,
                         group_ch, group_states)

    @pl.when(jnp.logical_and(is_prompt, t_in_seq == 0))
    def _():
        carry_ref[...] = jnp.zeros_like(carry_ref)

    def run(n_groups, steps, enter_factory, leave_factory):
        u_sm = _to_stepmajor(u_ref[...].astype(F32), uslab_ref, n_groups, steps).astype(BF16)

        def b_matmul(b):
            ch = slice(b * SSM_BUNDLE_CH, (b + 1) * SSM_BUNDLE_CH)
            bu_ref[b] = jnp.dot(u_sm[:, ch], bbig_ref[b], preferred_element_type=F32)

        b_matmul(0)
        for b in range(n_bundles):
            if b + 1 < n_bundles:
                b_matmul(b + 1)
            for g in range(n_groups):
                _scan_group(bu_ref, b, g * SUBLANES * steps, steps, width, b * width,
                            (lamre_ref, lamim_ref), enter_factory(g), leave_factory(g))
            y_b = jnp.dot(bu_ref[b].astype(BF16), cbig_ref[b], preferred_element_type=F32)
            _store_stepmajor(y_b, yslab_ref, b * (SSM_BUNDLE_CH // LANES), steps, 0)
        y = _load_natural(yslab_ref, n_groups, steps) + d_ref[...] * u_ref[...].astype(F32)
        g_act = _gelu_tanh(y)
        z = jnp.dot(g_act.astype(BF16), wglu_ref[...], preferred_element_type=F32)
        o = g_act * _sigmoid(z) * sza_ref[...].astype(F32)
        ya = jnp.dot(o.astype(BF16), wouta_ref[...], preferred_element_type=F32)
        ya_ref[...] = ya.astype(BF16)

    def prompt_enter(_g):
        def enter(er, ei, tl):
            xr, xi = er, ei
            for step, shift in enumerate((1, 2, 4)):
                xr, xi = _cmuladd(tqre_ref[step, :, tl], tqim_ref[step, :, tl],
                                  pltpu.roll(xr, shift, 0), pltpu.roll(xi, shift, 0), xr, xi)
            c0r = jnp.broadcast_to(carry_ref[0:1, tl], xr.shape)
            c0i = jnp.broadcast_to(carry_ref[1:2, tl], xi.shape)
            xr, xi = _cmuladd(tqre_ref[3, :, tl], tqim_ref[3, :, tl], c0r, c0i, xr, xi)
            carry_ref[0:1, tl] = xr[SUBLANES - 1:SUBLANES, :]
            carry_ref[1:2, tl] = xi[SUBLANES - 1:SUBLANES, :]
            first = lax.broadcasted_iota(jnp.int32, xr.shape, 0) == 0
            return (jnp.where(first, c0r, pltpu.roll(xr, 1, 0)),
                    jnp.where(first, c0i, pltpu.roll(xi, 1, 0)))
        return enter

    def prompt_leave(_g):
        return lambda hr, hi, tl: None

    def sample_enter(g):
        rows = slice(g * SUBLANES, (g + 1) * SUBLANES)
        return lambda er, ei, tl: (h0re_ref[rows, tl], h0im_ref[rows, tl])

    def sample_leave(g):
        rows = slice(g * SUBLANES, (g + 1) * SUBLANES)

        def leave(hr, hi, tl):
            ssre_ref[rows, tl] = hr
            ssim_ref[rows, tl] = hi
        return leave

    @pl.when(is_prompt)
    def _():
        run(1, TILE_M // SUBLANES, prompt_enter, prompt_leave)

    @pl.when(jnp.logical_not(is_prompt))
    def _():
        run(TILE_M // (SUBLANES * sample_steps), sample_steps, sample_enter, sample_leave)

    @pl.when(jnp.logical_and(is_prompt, t_in_seq == tiles_per_seq - 1))
    def _():
        row = pl.ds(i // tiles_per_seq, 1)
        spre_ref[row, :] = carry_ref[0:1, :]
        spim_ref[row, :] = carry_ref[1:2, :]


def _ssm_call(u, sza, h0re, h0im, lam8, tabq, d, b2, ct, w_glu, w_out_a, n_prompt_rows, seq_len,
              sample_steps, group_ch, group_states):
    m, d_ssm = u.shape
    d_model = w_out_a.shape[1]
    n_state = lam8[0].shape[1]
    n_tiles = m // TILE_M
    n_p = n_prompt_rows // TILE_M
    n_batch = n_prompt_rows // seq_len
    seqs_per_tile = TILE_M // sample_steps
    n_sample_seq = h0re.shape[0]
    slab_rows = SUBLANES * _pitch(TILE_M // SUBLANES)
    n_bundles = d_ssm // SSM_BUNDLE_CH
    width = n_state // n_bundles
    assert 2 * width == d_model and n_bundles >= 2 and d_ssm % TILE_M == 0
    kernel = functools.partial(_ssm_kernel, n_prompt_tiles=n_p,
                               tiles_per_seq=seq_len // TILE_M, sample_steps=sample_steps,
                               group_ch=group_ch, group_states=group_states)
    tile_map = lambda i: (i, 0)
    sample_map = lambda i: (jnp.maximum(i - n_p, 0), 0)
    hbm = pl.BlockSpec(memory_space=pl.ANY)
    return pl.pallas_call(
        kernel,
        grid=(n_tiles,),
        in_specs=[
            pl.BlockSpec((TILE_M, d_ssm), tile_map),
            pl.BlockSpec((TILE_M, d_ssm), tile_map),
            pl.BlockSpec((seqs_per_tile, n_state), sample_map),
            pl.BlockSpec((seqs_per_tile, n_state), sample_map),
            _const_spec(lam8[0].shape), _const_spec(lam8[1].shape),
            _const_spec(tabq[0].shape), _const_spec(tabq[1].shape),
            _const_spec(d.shape),
            _const_spec(b2[0].shape), _const_spec(b2[1].shape),
            _const_spec(ct[0].shape), _const_spec(ct[1].shape),
            hbm, hbm,
        ],
        out_specs=[
            pl.BlockSpec((TILE_M, d_model), tile_map),
            pl.BlockSpec((n_batch, n_state), lambda i: (0, 0)),
            pl.BlockSpec((n_batch, n_state), lambda i: (0, 0)),
            pl.BlockSpec((seqs_per_tile, n_state), sample_map),
            pl.BlockSpec((seqs_per_tile, n_state), sample_map),
        ],
        out_shape=[
            jax.ShapeDtypeStruct((m, d_model), BF16),
            jax.ShapeDtypeStruct((n_batch, n_state), F32),
            jax.ShapeDtypeStruct((n_batch, n_state), F32),
            jax.ShapeDtypeStruct((n_sample_seq, n_state), F32),
            jax.ShapeDtypeStruct((n_sample_seq, n_state), F32),
        ],
        scratch_shapes=[
            pltpu.VMEM((n_bundles, TILE_M, 2 * width), F32),
            pltpu.VMEM((d_ssm // LANES, slab_rows, LANES), F32),
            pltpu.VMEM((d_ssm // LANES, slab_rows, LANES), F32),
            pltpu.VMEM((2, n_state), F32),
            pltpu.VMEM((n_bundles, SSM_BUNDLE_CH, 2 * width), BF16),
            pltpu.VMEM((n_bundles, 2 * width, SSM_BUNDLE_CH), BF16),
            pltpu.VMEM((d_ssm, d_ssm), BF16),
            pltpu.VMEM((d_ssm, d_model), BF16),
            pltpu.VMEM((2, STAGE_ROWS, d_ssm), F32),
            pltpu.SemaphoreType.DMA((2,)),
        ],
        compiler_params=pltpu.CompilerParams(
            dimension_semantics=("arbitrary",), vmem_limit_bytes=VMEM_LIMIT_BYTES),
        name="ssm_branch",
    )(u, sza, h0re, h0im, *lam8, *tabq, d, *b2, *ct, w_glu, w_out_a)


def _tail_kernel(q_ref, bz_ref, ya_ref, sga_ref, sgc_ref, xp_ref, xs_ref, hist_ref,
                 cw_ref, lng_ref, lnb_ref, woutc_hbm, wo_hbm,
                 yp_ref, ys_ref,
                 pad_ref, conv_ref, yc_ref, woutc_ref, wo_ref, stage_ref, sem_ref,
                 *, n_prompt_tiles, tiles_per_seq, alpha):
    i = pl.program_id(0)
    is_prompt = i < n_prompt_tiles

    @pl.when(i == 0)
    def _():
        _stage_weight_bf16(woutc_hbm, 0, woutc_ref, stage_ref, sem_ref)
        _stage_weight_bf16(wo_hbm, 0, wo_ref, stage_ref, sem_ref)

    @pl.when(jnp.logical_and(is_prompt, i % tiles_per_seq == 0))
    def _():
        pad_ref[0:CONV_PAD, :] = jnp.zeros((CONV_PAD, pad_ref.shape[1]), F32)

    q = q_ref[...]
    pad_ref[CONV_PAD:CONV_PAD + TILE_M, :] = q
    w0 = cw_ref[0:1, :]
    w1 = cw_ref[1:2, :]
    w2 = cw_ref[2:3, :]

    def finish(q1, q2, x_ref, y_ref):
        conv_ref[...] = w0 * q2 + w1 * q1 + w2 * q
        yc_in = conv_ref[...].astype(BF16) * bz_ref[...]
        yc_ref[...] = jnp.dot(yc_in, woutc_ref[...], preferred_element_type=F32)
        merged = sga_ref[...] * ya_ref[...] + sgc_ref[...] * yc_ref[...].astype(BF16)
        out = jnp.dot(merged, wo_ref[...], preferred_element_type=F32)
        r = alpha * x_ref[...] + out
        mu = jnp.mean(r, axis=-1, keepdims=True)
        rc = r - mu
        var = jnp.mean(rc * rc, axis=-1, keepdims=True)
        y_ref[...] = rc * lax.rsqrt(var + LN_EPS) * lng_ref[...] + lnb_ref[...]

    @pl.when(is_prompt)
    def _():
        q1 = pad_ref[CONV_PAD - 1:CONV_PAD - 1 + TILE_M, :]
        q2 = pad_ref[CONV_PAD - 2:CONV_PAD - 2 + TILE_M, :]
        finish(q1, q2, xp_ref, yp_ref)
        pad_ref[0:CONV_PAD, :] = pad_ref[TILE_M:TILE_M + CONV_PAD, :]

    @pl.when(jnp.logical_not(is_prompt))
    def _():
        t = lax.broadcasted_iota(jnp.int32, q.shape, 0) % SUBLANES
        d_conv = q.shape[1]

        def history(j):
            return jnp.concatenate(
                [jnp.broadcast_to(hist_ref[s:s + 1, j * d_conv:(j + 1) * d_conv],
                                  (SUBLANES, d_conv)) for s in range(TILE_M // SUBLANES)], axis=0)

        e0 = history(0)
        e1 = history(1)
        q1 = jnp.where(t == 0, e1, pad_ref[CONV_PAD - 1:CONV_PAD - 1 + TILE_M, :])
        q2 = jnp.where(t == 0, e0,
                       jnp.where(t == 1, e1, pad_ref[CONV_PAD - 2:CONV_PAD - 2 + TILE_M, :]))
        finish(q1, q2, xs_ref, ys_ref)


def _tail_call(q, bz, ya, sga, sgc, xp, xs, hist, conv_w, w_out_c, w_o, ln_g, ln_b,
               seq_len, alpha):
    m, d_conv = q.shape
    m_p, d_model = xp.shape
    m_s = xs.shape[0]
    n_p = m_p // TILE_M
    kernel = functools.partial(_tail_kernel, n_prompt_tiles=n_p,
                               tiles_per_seq=seq_len // TILE_M, alpha=alpha)
    prompt_map = lambda i: (jnp.minimum(i, n_p - 1), 0)
    sample_map = lambda i: (jnp.maximum(i - n_p, 0), 0)
    row_map = lambda i: (i, 0)
    hbm = pl.BlockSpec(memory_space=pl.ANY)
    assert w_out_c.shape[1] == w_o.shape[1] == d_model
    return pl.pallas_call(
        kernel,
        grid=(m // TILE_M,),
        in_specs=[
            pl.BlockSpec((TILE_M, d_conv), row_map),
            pl.BlockSpec((TILE_M, d_conv), row_map),
            pl.BlockSpec((TILE_M, d_model), row_map),
            pl.BlockSpec((TILE_M, d_model), row_map),
            pl.BlockSpec((TILE_M, d_model), row_map),
            pl.BlockSpec((TILE_M, d_model), prompt_map),
            pl.BlockSpec((TILE_M, d_model), sample_map),
            pl.BlockSpec((TILE_M // SUBLANES, hist.shape[1]), sample_map),
            _const_spec(conv_w.shape),
            _const_spec(ln_g.shape),
            _const_spec(ln_b.shape),
            hbm, hbm,
        ],
        out_specs=[
            pl.BlockSpec((TILE_M, d_model), prompt_map),
            pl.BlockSpec((TILE_M, d_model), sample_map),
        ],
        out_shape=[
            jax.ShapeDtypeStruct((m_p, d_model), F32),
            jax.ShapeDtypeStruct((m_s, d_model), F32),
        ],
        scratch_shapes=[
            pltpu.VMEM((TILE_M + CONV_PAD, d_conv), F32),
            pltpu.VMEM((TILE_M, d_conv), F32),
            pltpu.VMEM((TILE_M, d_model), F32),
            pltpu.VMEM(w_out_c.shape, BF16),
            pltpu.VMEM(w_o.shape, BF16),
            pltpu.VMEM((2, STAGE_ROWS, d_model), F32),
            pltpu.SemaphoreType.DMA((2,)),
        ],
        compiler_params=pltpu.CompilerParams(
            dimension_semantics=("arbitrary",), vmem_limit_bytes=VMEM_LIMIT_BYTES),
        name="tail",
    )(q, bz, ya, sga, sgc, xp, xs, hist, conv_w, ln_g, ln_b, w_out_c, w_o)


def _ssm_params(a_re, a_im, log_dt, b_re, b_im, c_re, c_im, prompt_steps):
    g, p, gc = b_re.shape
    dt = jnp.exp(log_dt)[:, None]
    mag = jnp.exp(a_re * dt)
    ang = a_im * dt
    lam_re = mag * jnp.cos(ang)
    lam_im = mag * jnp.sin(ang)
    den = a_re * a_re + a_im * a_im
    q_re = ((lam_re - 1.0) * a_re + lam_im * a_im) / den
    q_im = (lam_im * a_re - (lam_re - 1.0) * a_im) / den
    bb_re = q_re[..., None] * b_re - q_im[..., None] * b_im
    bb_im = q_re[..., None] * b_im + q_im[..., None] * b_re

    def channel_rows(bb):
        rows = bb.transpose(0, 2, 1).reshape(g * gc, p)
        return jnp.tile(rows, (1, LANES // p))

    def state_rows(cc):
        rows = cc.transpose(0, 2, 1).reshape(g * p, gc)
        return jnp.tile(rows, (1, LANES // gc))

    b2 = (channel_rows(bb_re), channel_rows(bb_im))
    ct = (state_rows(c_re), -state_rows(c_im))

    lam8 = (jnp.broadcast_to(lam_re.reshape(1, -1), (SUBLANES, g * p)),
            jnp.broadcast_to(lam_im.reshape(1, -1), (SUBLANES, g * p)))

    row = jnp.arange(SUBLANES, dtype=F32)[None, :, None]
    shift = jnp.array([1.0, 2.0, 4.0], F32)[:, None, None]
    exponent = jnp.concatenate([jnp.broadcast_to(shift, (3, SUBLANES, 1)), row + 1.0]) * prompt_steps
    keep = jnp.concatenate([row >= shift, jnp.ones((1, SUBLANES, 1), bool)])
    mag_e = jnp.where(keep, jnp.exp(exponent * (a_re * dt).reshape(1, 1, -1)), 0.0)
    ang_e = exponent * ang.reshape(1, 1, -1)
    tabq = (mag_e * jnp.cos(ang_e), mag_e * jnp.sin(ang_e))
    return b2, ct, lam8, tabq


def kernel(x_prompt, x_sample, state_ssm_re, state_ssm_im, state_conv, w_in, ssm_a_re, ssm_a_im, ssm_log_dt, ssm_b_re, ssm_b_im, ssm_c_re, ssm_c_im, ssm_d, w_glu, w_out_a, conv_w, w_out_c, w_o, ln_g, ln_b):
    depth = w_in.shape[0]
    assert depth == 1, "single-layer trunk"
    batch, seq, d_model = x_prompt.shape
    dec_batch, dec_seq, _ = x_sample.shape
    assert dec_seq == SUBLANES and seq % TILE_M == 0 and (dec_batch * dec_seq) % TILE_M == 0
    g, p, gc = ssm_b_re.shape[1:]
    d_ssm = g * gc
    d_conv = conv_w.shape[2]
    n_state = g * p
    alpha = (2 * depth) ** 0.25

    xp = x_prompt.reshape(batch * seq, d_model)
    xs = x_sample.reshape(dec_batch * dec_seq, d_model)
    m_p = xp.shape[0]
    w = w_in[0]

    u, sza = _proj_call(xp, xs, w, (0, 1), _epi_ssm_in, (d_ssm, d_ssm), (BF16, BF16), "proj_ssm")
    (bz,) = _proj_call(xp, xs, w, (2, 5), _epi_conv_gate, (d_conv,), (BF16,), "proj_conv_gate")
    (q,) = _proj_call(xp, xs, w, (3, 4), _epi_conv_in, (d_conv,), (F32,), "proj_conv_in")
    (sga,) = _proj_call(xp, xs, w, (6, 7), _epi_merge_gate, (d_model,), (BF16,), "proj_gate_a")
    (sgc,) = _proj_call(xp, xs, w, (8, 9), _epi_merge_gate, (d_model,), (BF16,), "proj_gate_c")

    b2, ct, lam8, tabq = _ssm_params(
        ssm_a_re[0], ssm_a_im[0], ssm_log_dt[0], ssm_b_re[0], ssm_b_im[0], ssm_c_re[0],
        ssm_c_im[0], TILE_M // SUBLANES)
    h0re = state_ssm_re[0].reshape(dec_batch, n_state)
    h0im = state_ssm_im[0].reshape(dec_batch, n_state)
    ya, spre, spim, ssre, ssim = _ssm_call(
        u, sza, h0re, h0im, lam8, tabq, ssm_d[0][None, :], b2, ct, w_glu[0], w_out_a[0],
        m_p, seq, dec_seq, gc, p)

    hist = state_conv[0].reshape(dec_batch, (state_conv.shape[2]) * d_conv)
    yp, ys = _tail_call(q, bz, ya, sga, sgc, xp, xs, hist, conv_w[0],
                        w_out_c[0], w_o[0],
                        ln_g[0][None, :], ln_b[0][None, :], seq, alpha)

    q8 = q.reshape(-1, SUBLANES, d_conv)
    q_p = q8[seq // SUBLANES - 1:m_p // SUBLANES:seq // SUBLANES, SUBLANES - 2:, :]
    q_s = q8[m_p // SUBLANES:, dec_seq - 2:, :]
    return (yp.reshape(batch, seq, d_model),
            ys.reshape(dec_batch, dec_seq, d_model),
            spre.reshape(1, batch, g, p),
            spim.reshape(1, batch, g, p),
            q_p[None],
            ssre.reshape(1, dec_batch, g, p),
            ssim.reshape(1, dec_batch, g, p),
            q_s[None])
```

```python
import functools
import math

import jax
import jax.numpy as jnp
from jax import lax
from jax.experimental import pallas as pl
from jax.experimental.pallas import tpu as pltpu

F32 = jnp.float32
BF16 = jnp.bfloat16

SUBLANES = 8
LANES = 128
VMEM_LIMIT_BYTES = 56 * 1024 * 1024

TILE_M = 256
PROJ_TILE_M = 512
SSM_BUNDLE_CH = 256
SCAN_CHUNK = 4 * LANES
CONV_PAD = SUBLANES
STAGE_ROWS = 512
NEXT_STAGE_ROWS = 128

LN_EPS = 1e-5
GELU_C = math.sqrt(2.0 / math.pi)


def _sigmoid(x):
    return 0.5 * jnp.tanh(0.5 * x) + 0.5


def _silu(x):
    return x * _sigmoid(x)


def _gelu_tanh(x):
    return 0.5 * x * (1.0 + jnp.tanh(GELU_C * (x + 0.044715 * (x * x * x))))


def _const_spec(shape):
    nd = len(shape)
    return pl.BlockSpec(shape, lambda i: (0,) * nd, pipeline_mode=pl.Buffered(1))


PROJ_GROUPS = ((0, 1), (2, 5), (3, 4), (6, 7), (8, 9))
PROJ_COLS = 1024
Q_GROUP = 2


def _epi_ssm_in(a, b, act_ref, q_ref):
    act_ref[:, :PROJ_COLS] = a.astype(BF16)
    act_ref[:, PROJ_COLS:] = _silu(b).astype(BF16)


def _epi_conv_gate(a, b, act_ref, q_ref):
    act_ref[:, :PROJ_COLS] = (a * _silu(b)).astype(BF16)


def _epi_conv_in(a, b, act_ref, q_ref):
    q_ref[...] = a * b


def _epi_merge_gate(a, b, act_ref, q_ref):
    act_ref[:, :PROJ_COLS] = _sigmoid(a).astype(BF16)
    act_ref[:, PROJ_COLS:] = _sigmoid(b).astype(BF16)


PROJ_EPILOGUES = (_epi_ssm_in, _epi_conv_gate, _epi_conv_in, _epi_merge_gate, _epi_merge_gate)


def _proj_kernel(xp_ref, xs_ref, w_hbm, act_ref, q_ref, wbf_ref, stage_ref, next_ref, sem_ref,
                 *, n_prompt_tiles):
    g = pl.program_id(0)
    i = pl.program_id(1)
    next_rows = next_ref.shape[1]
    n_next = xp_ref.shape[1] // next_rows

    @pl.when(jnp.logical_and(g == 0, i == 0))
    def _():
        for blk, col_block in enumerate(PROJ_GROUPS[0]):
            _stage_weight_bf16(w_hbm, col_block * PROJ_COLS, wbf_ref.at[0, blk], stage_ref, sem_ref)

    rows = pl.ds(pl.multiple_of(i * next_rows, next_rows), next_rows)

    def next_copy(gi, blk):
        col0 = PROJ_GROUPS[gi + 1][blk] * PROJ_COLS
        return pltpu.make_async_copy(w_hbm.at[rows, pl.ds(col0, PROJ_COLS)], next_ref.at[blk],
                                     sem_ref.at[blk])

    for gi, epilogue in enumerate(PROJ_EPILOGUES):
        @pl.when(g == gi)
        def _():
            has_next = gi + 1 < len(PROJ_GROUPS)
            if has_next:
                @pl.when(i < n_next)
                def _():
                    for blk in range(2):
                        next_copy(gi, blk).start()

            slot = gi % 2

            def compute(x_ref):
                if epilogue is _epi_conv_gate:
                    act_ref[:, PROJ_COLS:] = jnp.zeros((act_ref.shape[0], PROJ_COLS), BF16)
                xb = x_ref[...].astype(BF16)
                a = jnp.dot(xb, wbf_ref[slot, 0], preferred_element_type=F32)
                b = jnp.dot(xb, wbf_ref[slot, 1], preferred_element_type=F32)
                epilogue(a, b, act_ref, q_ref)

            @pl.when(i < n_prompt_tiles)
            def _():
                compute(xp_ref)

            @pl.when(i >= n_prompt_tiles)
            def _():
                compute(xs_ref)

            if has_next:
                @pl.when(i < n_next)
                def _():
                    for blk in range(2):
                        next_copy(gi, blk).wait()
                        wbf_ref[1 - slot, blk, rows, :] = next_ref[blk].astype(BF16)


def _proj_call(xp, xs, w_in):
    m_p, d = xp.shape
    m_s = xs.shape[0]
    tile = PROJ_TILE_M
    n_p, n_s = m_p // tile, m_s // tile
    n_tiles = n_p + n_s
    n_groups = len(PROJ_GROUPS)
    act_block = 2 * PROJ_COLS
    assert m_p % tile == 0 and m_s % tile == 0
    assert n_tiles * NEXT_STAGE_ROWS >= d and d % NEXT_STAGE_ROWS == 0
    kernel = functools.partial(_proj_kernel, n_prompt_tiles=n_p)

    def act_map(g, i):
        col = jnp.where(g < Q_GROUP, g, jnp.where(g == Q_GROUP, Q_GROUP - 1, g - 1))
        return jnp.where(g == Q_GROUP, n_tiles - 1, i), col

    def q_map(g, i):
        return jnp.where(g < Q_GROUP, 0, jnp.where(g == Q_GROUP, i, n_tiles - 1)), 0

    return pl.pallas_call(
        kernel,
        grid=(n_groups, n_tiles),
        in_specs=[
            pl.BlockSpec((tile, d), lambda g, i: (jnp.minimum(i, n_p - 1), 0)),
            pl.BlockSpec((tile, d), lambda g, i: (jnp.clip(i - n_p, 0, n_s - 1), 0)),
            pl.BlockSpec(memory_space=pl.ANY),
        ],
        out_specs=[pl.BlockSpec((tile, act_block), act_map),
                   pl.BlockSpec((tile, PROJ_COLS), q_map)],
        out_shape=[jax.ShapeDtypeStruct((m_p + m_s, (n_groups - 1) * act_block), BF16),
                   jax.ShapeDtypeStruct((m_p + m_s, PROJ_COLS), F32)],
        scratch_shapes=[
            pltpu.VMEM((2, 2, d, PROJ_COLS), BF16),
            pltpu.VMEM((2, STAGE_ROWS, PROJ_COLS), F32),
            pltpu.VMEM((2, NEXT_STAGE_ROWS, PROJ_COLS), F32),
            pltpu.SemaphoreType.DMA((2,)),
        ],
        compiler_params=pltpu.CompilerParams(
            dimension_semantics=("arbitrary", "arbitrary"), vmem_limit_bytes=VMEM_LIMIT_BYTES),
        name="proj",
    )(xp, xs, w_in)


def _pitch(steps):
    return steps if (steps // SUBLANES) % 2 == 1 else steps + SUBLANES


def _to_stepmajor(val, slab_ref, n_groups, steps):
    pitch = _pitch(steps)
    n_slabs = val.shape[1] // LANES
    n_sub = n_groups * SUBLANES
    for j in range(n_slabs):
        lanes = slice(j * LANES, (j + 1) * LANES)
        if pitch == steps:
            slab_ref[j, 0:n_sub * steps, :] = val[:, lanes]
        else:
            for s in range(n_sub):
                slab_ref[j, s * pitch:s * pitch + steps, :] = val[s * steps:(s + 1) * steps, lanes]
    blocks = []
    for g in range(n_groups):
        for k in range(steps):
            rows = pl.ds(g * SUBLANES * pitch + k, SUBLANES, stride=pitch)
            blocks.append(jnp.concatenate([slab_ref[j, rows, :] for j in range(n_slabs)], axis=1))
    return jnp.concatenate(blocks, axis=0)


def _store_stepmajor(val, slab_ref, slab0, steps, block0):
    pitch = _pitch(steps)
    for i in range(val.shape[0] // SUBLANES):
        g, k = divmod(block0 + i, steps)
        rows = pl.ds(g * SUBLANES * pitch + k, SUBLANES, stride=pitch)
        for j in range(val.shape[1] // LANES):
            slab_ref[slab0 + j, rows, :] = val[i * SUBLANES:(i + 1) * SUBLANES,
                                               j * LANES:(j + 1) * LANES]


def _load_natural(slab_ref, n_groups, steps):
    pitch = _pitch(steps)
    n_slabs = slab_ref.shape[0]
    n_sub = n_groups * SUBLANES
    if pitch == steps:
        return jnp.concatenate([slab_ref[j, 0:n_sub * steps, :] for j in range(n_slabs)], axis=1)
    return jnp.concatenate(
        [jnp.concatenate([slab_ref[j, s * pitch:s * pitch + steps, :] for j in range(n_slabs)],
                         axis=1) for s in range(n_sub)], axis=0)


def _cmuladd(ar, ai, br, bi, cr, ci):
    return ar * br - ai * bi + cr, ar * bi + ai * br + ci


def _scan_group(bu_ref, b, row0, steps, width, lane0, lam_ref, enter_fn, leave_fn):
    for c in range(width // SCAN_CHUNK):
        re = slice(c * SCAN_CHUNK, (c + 1) * SCAN_CHUNK)
        im = slice(width + c * SCAN_CHUNK, width + (c + 1) * SCAN_CHUNK)
        tl = slice(lane0 + c * SCAN_CHUNK, lane0 + (c + 1) * SCAN_CHUNK)
        lr, li = lam_ref[0][:, tl], lam_ref[1][:, tl]

        def x_block(k):
            rows = slice(row0 + k * SUBLANES, row0 + (k + 1) * SUBLANES)
            return rows, bu_ref[b, rows, re], bu_ref[b, rows, im]

        _, pr, pi = x_block(0)
        for k in range(1, steps):
            _, xr, xi = x_block(k)
            pr, pi = _cmuladd(lr, li, pr, pi, xr, xi)
        hr, hi = enter_fn(pr, pi, tl)
        for k in range(steps):
            rows, xr, xi = x_block(k)
            hr, hi = _cmuladd(lr, li, hr, hi, xr, xi)
            bu_ref[b, rows, re] = hr
            bu_ref[b, rows, im] = hi
        leave_fn(hr, hi, tl)


def _build_blockdiag(b2_ref, ct_ref, bbig_ref, cbig_ref, group_ch, group_states):
    n_bundles, n_ch, two_width = bbig_ref.shape
    width = two_width // 2
    ch_shift = group_ch.bit_length() - 1
    st_shift = group_states.bit_length() - 1
    assert group_ch == 1 << ch_shift and group_states == 1 << st_shift
    ch_group = lax.shift_right_logical(lax.broadcasted_iota(jnp.int32, (n_ch, LANES), 0), ch_shift)
    b_lane_group = lax.shift_right_logical(lax.broadcasted_iota(jnp.int32, (n_ch, LANES), 1),
                                           st_shift)
    st_group = lax.shift_right_logical(lax.broadcasted_iota(jnp.int32, (width, LANES), 0), st_shift)
    c_lane_group = lax.shift_right_logical(lax.broadcasted_iota(jnp.int32, (width, LANES), 1),
                                           ch_shift)
    for b in range(n_bundles):
        for part in range(2):
            src = b2_ref[part][b * n_ch:(b + 1) * n_ch, :]
            for j in range(width // LANES):
                keep = ch_group == b_lane_group + j * (LANES // group_states)
                lanes = slice(part * width + j * LANES, part * width + (j + 1) * LANES)
                bbig_ref[b, :, lanes] = jnp.where(keep, src, 0.0).astype(BF16)
            src = ct_ref[part][b * width:(b + 1) * width, :]
            for j in range(n_ch // LANES):
                keep = st_group == c_lane_group + j * (LANES // group_ch)
                cbig_ref[b, part * width:(part + 1) * width, j * LANES:(j + 1) * LANES] = (
                    jnp.where(keep, src, 0.0).astype(BF16))


def _stage_weight_bf16(w_hbm, col0, dst_ref, stage_ref, sem_ref):
    k, n = dst_ref.shape
    rows = stage_ref.shape[1]
    n_chunks = k // rows

    def copy(c):
        return pltpu.make_async_copy(w_hbm.at[pl.ds(c * rows, rows), pl.ds(col0, n)],
                                     stage_ref.at[c % 2], sem_ref.at[c % 2])

    copy(0).start()
    for c in range(n_chunks):
        if c + 1 < n_chunks:
            copy(c + 1).start()
        copy(c).wait()
        dst_ref[c * rows:(c + 1) * rows, :] = stage_ref[c % 2].astype(BF16)


def _ssm_kernel(u_ref, sza_ref, h0re_ref, h0im_ref, lamre_ref, lamim_ref, tqre_ref, tqim_ref, d_ref,
                b2re_ref, b2im_ref, ctre_ref, ctim_ref, wglu_hbm, wouta_hbm,
                ya_ref, spre_ref, spim_ref, ssre_ref, ssim_ref,
                bu_ref, uslab_ref, yslab_ref, carry_ref, bbig_ref, cbig_ref,
                wglu_ref, wouta_ref, stage_narrow_ref, sem_ref,
                *, n_prompt_tiles, tiles_per_seq, sample_steps, group_ch, group_states):
    i = pl.program_id(0)
    n_bundles = bbig_ref.shape[0]
    width = bbig_ref.shape[2] // 2
    is_prompt = i < n_prompt_tiles
    t_in_seq = i % tiles_per_seq

    @pl.when(i == 0)
    def _():
        _stage_weight_bf16(wouta_hbm, 0, wouta_ref, bu_ref, sem_ref)
        _stage_weight_bf16(wglu_hbm, 0, wglu_ref, stage_narrow_ref, sem_ref)
        _build_blockdiag((b2re_ref, b2im_ref), (ctre_ref, ctim_ref), bbig_ref, cbig_ref,
                         group_ch, group_states)

    @pl.when(jnp.logical_and(is_prompt, t_in_seq == 0))
    def _():
        carry_ref[...] = jnp.zeros_like(carry_ref)

    def run(n_groups, steps, enter_factory, leave_factory):
        u_sm = _to_stepmajor(u_ref[...].astype(F32), uslab_ref, n_groups, steps).astype(BF16)

        def b_matmul(b):
            ch = slice(b * SSM_BUNDLE_CH, (b + 1) * SSM_BUNDLE_CH)
            bu_ref[b] = jnp.dot(u_sm[:, ch], bbig_ref[b], preferred_element_type=F32)

        b_matmul(0)
        for b in range(n_bundles):
            if b + 1 < n_bundles:
                b_matmul(b + 1)
            for g in range(n_groups):
                _scan_group(bu_ref, b, g * SUBLANES * steps, steps, width, b * width,
                            (lamre_ref, lamim_ref), enter_factory(g), leave_factory(g))
            y_b = jnp.dot(bu_ref[b].astype(BF16), cbig_ref[b], preferred_element_type=F32)
            _store_stepmajor(y_b, yslab_ref, b * (SSM_BUNDLE_CH // LANES), steps, 0)
        y = _load_natural(yslab_ref, n_groups, steps) + d_ref[...] * u_ref[...].astype(F32)
        g_act = _gelu_tanh(y)
        z = jnp.dot(g_act.astype(BF16), wglu_ref[...], preferred_element_type=F32)
        o = g_act * _sigmoid(z) * sza_ref[...].astype(F32)
        ya = jnp.dot(o.astype(BF16), wouta_ref[...], preferred_element_type=F32)
        ya_ref[...] = ya.astype(BF16)

    def prompt_enter(_g):
        def enter(er, ei, tl):
            xr, xi = er, ei
            for step, shift in enumerate((1, 2, 4)):
                xr, xi = _cmuladd(tqre_ref[step, :, tl], tqim_ref[step, :, tl],
                                  pltpu.roll(xr, shift, 0), pltpu.roll(xi, shift, 0), xr, xi)
            c0r = jnp.broadcast_to(carry_ref[0:1, tl], xr.shape)
            c0i = jnp.broadcast_to(carry_ref[1:2, tl], xi.shape)
            xr, xi = _cmuladd(tqre_ref[3, :, tl], tqim_ref[3, :, tl], c0r, c0i, xr, xi)
            carry_ref[0:1, tl] = xr[SUBLANES - 1:SUBLANES, :]
            carry_ref[1:2, tl] = xi[SUBLANES - 1:SUBLANES, :]
            first = lax.broadcasted_iota(jnp.int32, xr.shape, 0) == 0
            return (jnp.where(first, c0r, pltpu.roll(xr, 1, 0)),
                    jnp.where(first, c0i, pltpu.roll(xi, 1, 0)))
        return enter

    def prompt_leave(_g):
        return lambda hr, hi, tl: None

    def sample_enter(g):
        rows = slice(g * SUBLANES, (g + 1) * SUBLANES)
        return lambda er, ei, tl: (h0re_ref[rows, tl], h0im_ref[rows, tl])

    def sample_leave(g):
        rows = slice(g * SUBLANES, (g + 1) * SUBLANES)

        def leave(hr, hi, tl):
            ssre_ref[rows, tl] = hr
            ssim_ref[rows, tl] = hi
        return leave

    @pl.when(is_prompt)
    def _():
        run(1, TILE_M // SUBLANES, prompt_enter, prompt_leave)

    @pl.when(jnp.logical_not(is_prompt))
    def _():
        run(TILE_M // (SUBLANES * sample_steps), sample_steps, sample_enter, sample_leave)

    @pl.when(jnp.logical_and(is_prompt, t_in_seq == tiles_per_seq - 1))
    def _():
        row = pl.ds(i // tiles_per_seq, 1)
        spre_ref[row, :] = carry_ref[0:1, :]
        spim_ref[row, :] = carry_ref[1:2, :]


def _ssm_call(act, h0re, h0im, lam8, tabq, d, b2, ct, w_glu, w_out_a, n_prompt_rows, seq_len,
              sample_steps, group_ch, group_states):
    m = act.shape[0]
    d_ssm, d_model = w_out_a.shape
    n_state = lam8[0].shape[1]
    n_tiles = m // TILE_M
    n_p = n_prompt_rows // TILE_M
    n_batch = n_prompt_rows // seq_len
    seqs_per_tile = TILE_M // sample_steps
    n_sample_seq = h0re.shape[0]
    slab_rows = SUBLANES * _pitch(TILE_M // SUBLANES)
    n_bundles = d_ssm // SSM_BUNDLE_CH
    width = n_state // n_bundles
    assert 2 * width == d_model and n_bundles >= 2 and d_ssm % TILE_M == 0
    kernel = functools.partial(_ssm_kernel, n_prompt_tiles=n_p,
                               tiles_per_seq=seq_len // TILE_M, sample_steps=sample_steps,
                               group_ch=group_ch, group_states=group_states)
    tile_map = lambda i: (i, 0)
    sample_map = lambda i: (jnp.maximum(i - n_p, 0), 0)
    hbm = pl.BlockSpec(memory_space=pl.ANY)
    return pl.pallas_call(
        kernel,
        grid=(n_tiles,),
        in_specs=[
            pl.BlockSpec((TILE_M, d_ssm), tile_map),
            pl.BlockSpec((TILE_M, d_ssm), lambda i: (i, 1)),
            pl.BlockSpec((seqs_per_tile, n_state), sample_map),
            pl.BlockSpec((seqs_per_tile, n_state), sample_map),
            _const_spec(lam8[0].shape), _const_spec(lam8[1].shape),
            _const_spec(tabq[0].shape), _const_spec(tabq[1].shape),
            _const_spec(d.shape),
            _const_spec(b2[0].shape), _const_spec(b2[1].shape),
            _const_spec(ct[0].shape), _const_spec(ct[1].shape),
            hbm, hbm,
        ],
        out_specs=[
            pl.BlockSpec((TILE_M, d_model), tile_map),
            pl.BlockSpec((n_batch, n_state), lambda i: (0, 0)),
            pl.BlockSpec((n_batch, n_state), lambda i: (0, 0)),
            pl.BlockSpec((seqs_per_tile, n_state), sample_map),
            pl.BlockSpec((seqs_per_tile, n_state), sample_map),
        ],
        out_shape=[
            jax.ShapeDtypeStruct((m, d_model), BF16),
            jax.ShapeDtypeStruct((n_batch, n_state), F32),
            jax.ShapeDtypeStruct((n_batch, n_state), F32),
            jax.ShapeDtypeStruct((n_sample_seq, n_state), F32),
            jax.ShapeDtypeStruct((n_sample_seq, n_state), F32),
        ],
        scratch_shapes=[
            pltpu.VMEM((n_bundles, TILE_M, 2 * width), F32),
            pltpu.VMEM((d_ssm // LANES, slab_rows, LANES), F32),
            pltpu.VMEM((d_ssm // LANES, slab_rows, LANES), F32),
            pltpu.VMEM((2, n_state), F32),
            pltpu.VMEM((n_bundles, SSM_BUNDLE_CH, 2 * width), BF16),
            pltpu.VMEM((n_bundles, 2 * width, SSM_BUNDLE_CH), BF16),
            pltpu.VMEM((d_ssm, d_ssm), BF16),
            pltpu.VMEM((d_ssm, d_model), BF16),
            pltpu.VMEM((2, STAGE_ROWS, d_ssm), F32),
            pltpu.SemaphoreType.DMA((2,)),
        ],
        compiler_params=pltpu.CompilerParams(
            dimension_semantics=("arbitrary",), vmem_limit_bytes=VMEM_LIMIT_BYTES),
        name="ssm_branch",
    )(act, act, h0re, h0im, *lam8, *tabq, d, *b2, *ct, w_glu, w_out_a)


def _tail_kernel(q_ref, bz_ref, ya_ref, sga_ref, sgc_ref, xp_ref, xs_ref, hist_ref,
                 cw_ref, lng_ref, lnb_ref, woutc_hbm, wo_hbm,
                 yp_ref, ys_ref,
                 pad_ref, conv_ref, yc_ref, woutc_ref, wo_ref, stage_ref, sem_ref,
                 *, n_prompt_tiles, tiles_per_seq, alpha):
    i = pl.program_id(0)
    is_prompt = i < n_prompt_tiles

    @pl.when(i == 0)
    def _():
        _stage_weight_bf16(woutc_hbm, 0, woutc_ref, stage_ref, sem_ref)
        _stage_weight_bf16(wo_hbm, 0, wo_ref, stage_ref, sem_ref)

    @pl.when(jnp.logical_and(is_prompt, i % tiles_per_seq == 0))
    def _():
        pad_ref[0:CONV_PAD, :] = jnp.zeros((CONV_PAD, pad_ref.shape[1]), F32)

    q = q_ref[...]
    pad_ref[CONV_PAD:CONV_PAD + TILE_M, :] = q
    w0 = cw_ref[0:1, :]
    w1 = cw_ref[1:2, :]
    w2 = cw_ref[2:3, :]

    def finish(q1, q2, x_ref, y_ref):
        conv_ref[...] = w0 * q2 + w1 * q1 + w2 * q
        yc_in = conv_ref[...].astype(BF16) * bz_ref[...]
        yc_ref[...] = jnp.dot(yc_in, woutc_ref[...], preferred_element_type=F32)
        merged = sga_ref[...] * ya_ref[...] + sgc_ref[...] * yc_ref[...].astype(BF16)
        out = jnp.dot(merged, wo_ref[...], preferred_element_type=F32)
        r = alpha * x_ref[...] + out
        mu = jnp.mean(r, axis=-1, keepdims=True)
        rc = r - mu
        var = jnp.mean(rc * rc, axis=-1, keepdims=True)
        y_ref[...] = rc * lax.rsqrt(var + LN_EPS) * lng_ref[...] + lnb_ref[...]

    @pl.when(is_prompt)
    def _():
        q1 = pad_ref[CONV_PAD - 1:CONV_PAD - 1 + TILE_M, :]
        q2 = pad_ref[CONV_PAD - 2:CONV_PAD - 2 + TILE_M, :]
        finish(q1, q2, xp_ref, yp_ref)
        pad_ref[0:CONV_PAD, :] = pad_ref[TILE_M:TILE_M + CONV_PAD, :]

    @pl.when(jnp.logical_not(is_prompt))
    def _():
        t = lax.broadcasted_iota(jnp.int32, q.shape, 0) % SUBLANES
        d_conv = q.shape[1]

        def history(j):
            return jnp.concatenate(
                [jnp.broadcast_to(hist_ref[s:s + 1, j * d_conv:(j + 1) * d_conv],
                                  (SUBLANES, d_conv)) for s in range(TILE_M // SUBLANES)], axis=0)

        e0 = history(0)
        e1 = history(1)
        q1 = jnp.where(t == 0, e1, pad_ref[CONV_PAD - 1:CONV_PAD - 1 + TILE_M, :])
        q2 = jnp.where(t == 0, e0,
                       jnp.where(t == 1, e1, pad_ref[CONV_PAD - 2:CONV_PAD - 2 + TILE_M, :]))
        finish(q1, q2, xs_ref, ys_ref)


def _tail_call(q, act, ya, xp, xs, hist, conv_w, w_out_c, w_o, ln_g, ln_b, seq_len, alpha):
    m, d_conv = q.shape
    m_p, d_model = xp.shape
    m_s = xs.shape[0]
    n_p = m_p // TILE_M
    kernel = functools.partial(_tail_kernel, n_prompt_tiles=n_p,
                               tiles_per_seq=seq_len // TILE_M, alpha=alpha)
    prompt_map = lambda i: (jnp.minimum(i, n_p - 1), 0)
    sample_map = lambda i: (jnp.maximum(i - n_p, 0), 0)
    row_map = lambda i: (i, 0)
    hbm = pl.BlockSpec(memory_space=pl.ANY)
    assert w_out_c.shape[1] == w_o.shape[1] == d_model
    return pl.pallas_call(
        kernel,
        grid=(m // TILE_M,),
        in_specs=[
            pl.BlockSpec((TILE_M, d_conv), row_map),
            pl.BlockSpec((TILE_M, d_conv), lambda i: (i, 2)),
            pl.BlockSpec((TILE_M, d_model), row_map),
            pl.BlockSpec((TILE_M, d_model), lambda i: (i, 2)),
            pl.BlockSpec((TILE_M, d_model), lambda i: (i, 3)),
            pl.BlockSpec((TILE_M, d_model), prompt_map),
            pl.BlockSpec((TILE_M, d_model), sample_map),
            pl.BlockSpec((TILE_M // SUBLANES, hist.shape[1]), sample_map),
            _const_spec(conv_w.shape),
            _const_spec(ln_g.shape),
            _const_spec(ln_b.shape),
            hbm, hbm,
        ],
        out_specs=[
            pl.BlockSpec((TILE_M, d_model), prompt_map),
            pl.BlockSpec((TILE_M, d_model), sample_map),
        ],
        out_shape=[
            jax.ShapeDtypeStruct((m_p, d_model), F32),
            jax.ShapeDtypeStruct((m_s, d_model), F32),
        ],
        scratch_shapes=[
            pltpu.VMEM((TILE_M + CONV_PAD, d_conv), F32),
            pltpu.VMEM((TILE_M, d_conv), F32),
            pltpu.VMEM((TILE_M, d_model), F32),
            pltpu.VMEM(w_out_c.shape, BF16),
            pltpu.VMEM(w_o.shape, BF16),
            pltpu.VMEM((2, STAGE_ROWS, d_model), F32),
            pltpu.SemaphoreType.DMA((2,)),
        ],
        compiler_params=pltpu.CompilerParams(
            dimension_semantics=("arbitrary",), vmem_limit_bytes=VMEM_LIMIT_BYTES),
        name="tail",
    )(q, act, ya, act, act, xp, xs, hist, conv_w, ln_g, ln_b, w_out_c, w_o)


def _ssm_params(a_re, a_im, log_dt, b_re, b_im, c_re, c_im, prompt_steps):
    g, p, gc = b_re.shape
    dt = jnp.exp(log_dt)[:, None]
    mag = jnp.exp(a_re * dt)
    ang = a_im * dt
    lam_re = mag * jnp.cos(ang)
    lam_im = mag * jnp.sin(ang)
    den = a_re * a_re + a_im * a_im
    q_re = ((lam_re - 1.0) * a_re + lam_im * a_im) / den
    q_im = (lam_im * a_re - (lam_re - 1.0) * a_im) / den
    bb_re = q_re[..., None] * b_re - q_im[..., None] * b_im
    bb_im = q_re[..., None] * b_im + q_im[..., None] * b_re

    def channel_rows(bb):
        rows = bb.transpose(0, 2, 1).reshape(g * gc, p)
        return jnp.tile(rows, (1, LANES // p))

    def state_rows(cc):
        rows = cc.transpose(0, 2, 1).reshape(g * p, gc)
        return jnp.tile(rows, (1, LANES // gc))

    b2 = (channel_rows(bb_re), channel_rows(bb_im))
    ct = (state_rows(c_re), -state_rows(c_im))

    lam8 = (jnp.broadcast_to(lam_re.reshape(1, -1), (SUBLANES, g * p)),
            jnp.broadcast_to(lam_im.reshape(1, -1), (SUBLANES, g * p)))

    row = jnp.arange(SUBLANES, dtype=F32)[None, :, None]
    shift = jnp.array([1.0, 2.0, 4.0], F32)[:, None, None]
    exponent = jnp.concatenate([jnp.broadcast_to(shift, (3, SUBLANES, 1)), row + 1.0]) * prompt_steps
    keep = jnp.concatenate([row >= shift, jnp.ones((1, SUBLANES, 1), bool)])
    mag_e = jnp.where(keep, jnp.exp(exponent * (a_re * dt).reshape(1, 1, -1)), 0.0)
    ang_e = exponent * ang.reshape(1, 1, -1)
    tabq = (mag_e * jnp.cos(ang_e), mag_e * jnp.sin(ang_e))
    return b2, ct, lam8, tabq


def kernel(x_prompt, x_sample, state_ssm_re, state_ssm_im, state_conv, w_in, ssm_a_re, ssm_a_im, ssm_log_dt, ssm_b_re, ssm_b_im, ssm_c_re, ssm_c_im, ssm_d, w_glu, w_out_a, conv_w, w_out_c, w_o, ln_g, ln_b):
    depth = w_in.shape[0]
    assert depth == 1, "single-layer trunk"
    batch, seq, d_model = x_prompt.shape
    dec_batch, dec_seq, _ = x_sample.shape
    assert dec_seq == SUBLANES and seq % TILE_M == 0 and (dec_batch * dec_seq) % TILE_M == 0
    g, p, gc = ssm_b_re.shape[1:]
    d_ssm = g * gc
    d_conv = conv_w.shape[2]
    n_state = g * p
    alpha = (2 * depth) ** 0.25

    xp = x_prompt.reshape(batch * seq, d_model)
    xs = x_sample.reshape(dec_batch * dec_seq, d_model)
    m_p = xp.shape[0]
    w = w_in[0]

    assert d_ssm == d_conv == PROJ_COLS and d_model == 2 * PROJ_COLS
    act, q = _proj_call(xp, xs, w)

    b2, ct, lam8, tabq = _ssm_params(
        ssm_a_re[0], ssm_a_im[0], ssm_log_dt[0], ssm_b_re[0], ssm_b_im[0], ssm_c_re[0],
        ssm_c_im[0], TILE_M // SUBLANES)
    h0re = state_ssm_re[0].reshape(dec_batch, n_state)
    h0im = state_ssm_im[0].reshape(dec_batch, n_state)
    ya, spre, spim, ssre, ssim = _ssm_call(
        act, h0re, h0im, lam8, tabq, ssm_d[0][None, :], b2, ct, w_glu[0], w_out_a[0],
        m_p, seq, dec_seq, gc, p)

    hist = state_conv[0].reshape(dec_batch, (state_conv.shape[2]) * d_conv)
    yp, ys = _tail_call(q, act, ya, xp, xs, hist, conv_w[0], w_out_c[0], w_o[0],
                        ln_g[0][None, :], ln_b[0][None, :], seq, alpha)

    q8 = q.reshape(-1, SUBLANES, d_conv)
    q_p = q8[seq // SUBLANES - 1:m_p // SUBLANES:seq // SUBLANES, SUBLANES - 2:, :]
    q_s = q8[m_p // SUBLANES:, dec_seq - 2:, :]
    return (yp.reshape(batch, seq, d_model),
            ys.reshape(dec_batch, dec_seq, d_model),
            spre.reshape(1, batch, g, p),
            spim.reshape(1, batch, g, p),
            q_p[None],
            ssre.reshape(1, dec_batch, g, p),
            ssim.reshape(1, dec_batch, g, p),
            q_s[None])
```

```python
import functools
import math

import jax
import jax.numpy as jnp
from jax import lax
from jax.experimental import pallas as pl
from jax.experimental.pallas import tpu as pltpu

F32 = jnp.float32
BF16 = jnp.bfloat16

SUBLANES = 8
LANES = 128
VMEM_LIMIT_BYTES = 56 * 1024 * 1024

TILE_M = 256
PROJ_TILE_M = 512
SSM_FOLD = 2
SSM_VMEM_LIMIT_BYTES = 60 * 1024 * 1024
SSM_BUNDLE_CH = 256
SCAN_CHUNK = 4 * LANES
CONV_PAD = SUBLANES
STAGE_ROWS = 512
NEXT_STAGE_ROWS = 128

LN_EPS = 1e-5
GELU_C = math.sqrt(2.0 / math.pi)


def _sigmoid(x):
    return 0.5 * jnp.tanh(0.5 * x) + 0.5


def _silu(x):
    return x * _sigmoid(x)


def _gelu_tanh(x):
    return 0.5 * x * (1.0 + jnp.tanh(GELU_C * (x + 0.044715 * (x * x * x))))


def _const_spec(shape):
    nd = len(shape)
    return pl.BlockSpec(shape, lambda i: (0,) * nd, pipeline_mode=pl.Buffered(1))


PROJ_GROUPS = ((0, 1), (2, 5), (3, 4), (6, 7), (8, 9))
PROJ_COLS = 1024
Q_GROUP = 2


def _epi_ssm_in(a, b, act_ref, q_ref):
    act_ref[:, :PROJ_COLS] = a.astype(BF16)
    act_ref[:, PROJ_COLS:] = _silu(b).astype(BF16)


def _epi_conv_gate(a, b, act_ref, q_ref):
    act_ref[:, :PROJ_COLS] = (a * _silu(b)).astype(BF16)


def _epi_conv_in(a, b, act_ref, q_ref):
    q_ref[...] = a * b


def _epi_merge_gate(a, b, act_ref, q_ref):
    act_ref[:, :PROJ_COLS] = _sigmoid(a).astype(BF16)
    act_ref[:, PROJ_COLS:] = _sigmoid(b).astype(BF16)


PROJ_EPILOGUES = (_epi_ssm_in, _epi_conv_gate, _epi_conv_in, _epi_merge_gate, _epi_merge_gate)


def _proj_kernel(xp_ref, xs_ref, w_hbm, act_ref, q_ref, wbf_ref, stage_ref, next_ref, sem_ref,
                 *, n_prompt_tiles):
    g = pl.program_id(0)
    i = pl.program_id(1)
    next_rows = next_ref.shape[1]
    n_next = xp_ref.shape[1] // next_rows

    @pl.when(jnp.logical_and(g == 0, i == 0))
    def _():
        for blk, col_block in enumerate(PROJ_GROUPS[0]):
            _stage_weight_bf16(w_hbm, col_block * PROJ_COLS, wbf_ref.at[0, blk], stage_ref, sem_ref)

    rows = pl.ds(pl.multiple_of(i * next_rows, next_rows), next_rows)

    def next_copy(gi, blk):
        col0 = PROJ_GROUPS[gi + 1][blk] * PROJ_COLS
        return pltpu.make_async_copy(w_hbm.at[rows, pl.ds(col0, PROJ_COLS)], next_ref.at[blk],
                                     sem_ref.at[blk])

    for gi, epilogue in enumerate(PROJ_EPILOGUES):
        @pl.when(g == gi)
        def _():
            has_next = gi + 1 < len(PROJ_GROUPS)
            if has_next:
                @pl.when(i < n_next)
                def _():
                    for blk in range(2):
                        next_copy(gi, blk).start()

            slot = gi % 2

            def compute(x_ref):
                if epilogue is _epi_conv_gate:
                    act_ref[:, PROJ_COLS:] = jnp.zeros((act_ref.shape[0], PROJ_COLS), BF16)
                xb = x_ref[...].astype(BF16)
                a = jnp.dot(xb, wbf_ref[slot, 0], preferred_element_type=F32)
                b = jnp.dot(xb, wbf_ref[slot, 1], preferred_element_type=F32)
                epilogue(a, b, act_ref, q_ref)

            @pl.when(i < n_prompt_tiles)
            def _():
                compute(xp_ref)

            @pl.when(i >= n_prompt_tiles)
            def _():
                compute(xs_ref)

            if has_next:
                @pl.when(i < n_next)
                def _():
                    for blk in range(2):
                        next_copy(gi, blk).wait()
                        wbf_ref[1 - slot, blk, rows, :] = next_ref[blk].astype(BF16)


def _proj_call(xp, xs, w_in):
    m_p, d = xp.shape
    m_s = xs.shape[0]
    tile = PROJ_TILE_M
    n_p, n_s = m_p // tile, m_s // tile
    n_tiles = n_p + n_s
    n_groups = len(PROJ_GROUPS)
    act_block = 2 * PROJ_COLS
    assert m_p % tile == 0 and m_s % tile == 0
    assert n_tiles * NEXT_STAGE_ROWS >= d and d % NEXT_STAGE_ROWS == 0
    kernel = functools.partial(_proj_kernel, n_prompt_tiles=n_p)

    def act_map(g, i):
        col = jnp.where(g < Q_GROUP, g, jnp.where(g == Q_GROUP, Q_GROUP - 1, g - 1))
        return jnp.where(g == Q_GROUP, n_tiles - 1, i), col

    def q_map(g, i):
        return jnp.where(g < Q_GROUP, 0, jnp.where(g == Q_GROUP, i, n_tiles - 1)), 0

    return pl.pallas_call(
        kernel,
        grid=(n_groups, n_tiles),
        in_specs=[
            pl.BlockSpec((tile, d), lambda g, i: (jnp.minimum(i, n_p - 1), 0)),
            pl.BlockSpec((tile, d), lambda g, i: (jnp.clip(i - n_p, 0, n_s - 1), 0)),
            pl.BlockSpec(memory_space=pl.ANY),
        ],
        out_specs=[pl.BlockSpec((tile, act_block), act_map),
                   pl.BlockSpec((tile, PROJ_COLS), q_map)],
        out_shape=[jax.ShapeDtypeStruct((m_p + m_s, (n_groups - 1) * act_block), BF16),
                   jax.ShapeDtypeStruct((m_p + m_s, PROJ_COLS), F32)],
        scratch_shapes=[
            pltpu.VMEM((2, 2, d, PROJ_COLS), BF16),
            pltpu.VMEM((2, STAGE_ROWS, PROJ_COLS), F32),
            pltpu.VMEM((2, NEXT_STAGE_ROWS, PROJ_COLS), F32),
            pltpu.SemaphoreType.DMA((2,)),
        ],
        compiler_params=pltpu.CompilerParams(
            dimension_semantics=("arbitrary", "arbitrary"), vmem_limit_bytes=VMEM_LIMIT_BYTES),
        name="proj",
    )(xp, xs, w_in)


def _pitch(steps):
    return steps if (steps // SUBLANES) % 2 == 1 else steps + SUBLANES


def _stepmajor_blocks(val, slab_ref, n_groups, steps):
    pitch = _pitch(steps)
    n_slabs = val.shape[1] // LANES
    n_sub = n_groups * SUBLANES
    for j in range(n_slabs):
        lanes = slice(j * LANES, (j + 1) * LANES)
        if pitch == steps:
            slab_ref[j, 0:n_sub * steps, :] = val[:, lanes]
        else:
            for s in range(n_sub):
                slab_ref[j, s * pitch:s * pitch + steps, :] = val[s * steps:(s + 1) * steps, lanes]
    blocks = []
    for g in range(n_groups):
        for k in range(steps):
            rows = pl.ds(g * SUBLANES * pitch + k, SUBLANES, stride=pitch)
            blocks.append(jnp.concatenate([slab_ref[j, rows, :] for j in range(n_slabs)], axis=1))
    return blocks


def _store_folded(val, slab_ref, slab0, steps):
    pitch = _pitch(steps)
    n = val.shape[1] // SSM_FOLD
    for i in range(val.shape[0] // SUBLANES):
        g, kf = divmod(i, steps // SSM_FOLD)
        for m in range(SSM_FOLD):
            rows = pl.ds(g * SUBLANES * pitch + SSM_FOLD * kf + m, SUBLANES, stride=pitch)
            for j in range(n // LANES):
                slab_ref[slab0 + j, rows, :] = val[i * SUBLANES:(i + 1) * SUBLANES,
                                                   m * n + j * LANES:m * n + (j + 1) * LANES]


def _load_natural(slab_ref, n_groups, steps):
    pitch = _pitch(steps)
    n_slabs = slab_ref.shape[0]
    n_sub = n_groups * SUBLANES
    if pitch == steps:
        return jnp.concatenate([slab_ref[j, 0:n_sub * steps, :] for j in range(n_slabs)], axis=1)
    return jnp.concatenate(
        [jnp.concatenate([slab_ref[j, s * pitch:s * pitch + steps, :] for j in range(n_slabs)],
                         axis=1) for s in range(n_sub)], axis=0)


def _cmuladd(ar, ai, br, bi, cr, ci):
    return ar * br - ai * bi + cr, ar * bi + ai * br + ci


def _scan_group(bu_ref, b, row0, steps, width, lane0, lam_ref, enter_fn, leave_fn):
    for c in range(width // SCAN_CHUNK):
        re = slice(c * SCAN_CHUNK, (c + 1) * SCAN_CHUNK)
        im = slice(width + c * SCAN_CHUNK, width + (c + 1) * SCAN_CHUNK)
        tl = slice(lane0 + c * SCAN_CHUNK, lane0 + (c + 1) * SCAN_CHUNK)
        lr, li = lam_ref[0][:, tl], lam_ref[1][:, tl]

        def x_block(k):
            rows = slice(row0 + k * SUBLANES, row0 + (k + 1) * SUBLANES)
            return rows, bu_ref[b, rows, re], bu_ref[b, rows, im]

        _, pr, pi = x_block(0)
        for k in range(1, steps):
            _, xr, xi = x_block(k)
            pr, pi = _cmuladd(lr, li, pr, pi, xr, xi)
        hr, hi = enter_fn(pr, pi, tl)
        for k in range(steps):
            rows, xr, xi = x_block(k)
            bu_ref[b, rows, re] = hr
            bu_ref[b, rows, im] = hi
            hr, hi = _cmuladd(lr, li, hr, hi, xr, xi)
        leave_fn(hr, hi, tl)


def _build_folded_weights(b_refs, c_refs, k_refs, bq_ref, wg_ref, wu_ref, group_ch, group_states):
    n_bundles = bq_ref.shape[0]
    n_ch = bq_ref.shape[1] // SSM_FOLD
    width = bq_ref.shape[2] // 2
    ch_shift = group_ch.bit_length() - 1
    st_shift = group_states.bit_length() - 1
    assert group_ch == 1 << ch_shift and group_states == 1 << st_shift

    def group_of(n_rows, axis, shift):
        return lax.shift_right_logical(lax.broadcasted_iota(jnp.int32, (n_rows, LANES), axis), shift)

    ch_rows = group_of(n_ch, 0, ch_shift)
    st_rows = group_of(width, 0, st_shift)
    st_lanes = group_of(n_ch, 1, st_shift)
    ch_lanes_for_st = group_of(width, 1, ch_shift)
    ch_lanes_for_ch = group_of(n_ch, 1, ch_shift)
    for b in range(n_bundles):
        ch = slice(b * n_ch, (b + 1) * n_ch)
        st = slice(b * width, (b + 1) * width)
        for pos, pair in enumerate(b_refs):
            for part in range(2):
                src = pair[part][ch, :]
                for j in range(width // LANES):
                    keep = ch_rows == st_lanes + j * (LANES // group_states)
                    lanes = slice(part * width + j * LANES, part * width + (j + 1) * LANES)
                    bq_ref[b, pos * n_ch:(pos + 1) * n_ch, lanes] = (
                        jnp.where(keep, src, 0.0).astype(BF16))
        for pos, pair in enumerate(c_refs):
            for part in range(2):
                src = pair[part][st, :]
                for j in range(n_ch // LANES):
                    keep = st_rows == ch_lanes_for_st + j * (LANES // group_ch)
                    lanes = slice(pos * n_ch + j * LANES, pos * n_ch + (j + 1) * LANES)
                    wg_ref[b, part * width:(part + 1) * width, lanes] = (
                        jnp.where(keep, src, 0.0).astype(BF16))
        for (pos_in, pos_out), k_ref in (((0, 0), k_refs[0]), ((0, 1), k_refs[1]),
                                         ((1, 0), None), ((1, 1), k_refs[0])):
            for j in range(n_ch // LANES):
                lanes = slice(pos_out * n_ch + j * LANES, pos_out * n_ch + (j + 1) * LANES)
                rows = slice(pos_in * n_ch, (pos_in + 1) * n_ch)
                if k_ref is None:
                    wu_ref[b, rows, lanes] = jnp.zeros((n_ch, LANES), BF16)
                else:
                    keep = ch_rows == ch_lanes_for_ch + j * (LANES // group_ch)
                    wu_ref[b, rows, lanes] = jnp.where(keep, k_ref[ch, :], 0.0).astype(BF16)


def _stage_weight_bf16(w_hbm, col0, dst_ref, stage_ref, sem_ref):
    k, n = dst_ref.shape
    rows = stage_ref.shape[1]
    n_chunks = k // rows

    def copy(c):
        return pltpu.make_async_copy(w_hbm.at[pl.ds(c * rows, rows), pl.ds(col0, n)],
                                     stage_ref.at[c % 2], sem_ref.at[c % 2])

    copy(0).start()
    for c in range(n_chunks):
        if c + 1 < n_chunks:
            copy(c + 1).start()
        copy(c).wait()
        dst_ref[c * rows:(c + 1) * rows, :] = stage_ref[c % 2].astype(BF16)


def _ssm_kernel(u_ref, sza_ref, h0re_ref, h0im_ref, lamre_ref, lamim_ref, tqre_ref, tqim_ref, d_ref,
                blre_ref, blim_ref, b2re_ref, b2im_ref, c1re_ref, c1im_ref, c2re_ref, c2im_ref,
                k0_ref, k1_ref, wglu_hbm, wouta_hbm,
                ya_ref, spre_ref, spim_ref, ssre_ref, ssim_ref,
                bu_ref, uslab_ref, yslab_ref, carry_ref, bq_ref, wg_ref, wu_ref,
                wglu_ref, wouta_ref, stage_narrow_ref, sem_ref,
                *, n_prompt_tiles, tiles_per_seq, sample_steps, group_ch, group_states):
    i = pl.program_id(0)
    n_bundles = bq_ref.shape[0]
    width = bq_ref.shape[2] // 2
    is_prompt = i < n_prompt_tiles
    t_in_seq = i % tiles_per_seq

    @pl.when(i == 0)
    def _():
        _stage_weight_bf16(wouta_hbm, 0, wouta_ref, bu_ref, sem_ref)
        _stage_weight_bf16(wglu_hbm, 0, wglu_ref, stage_narrow_ref, sem_ref)
        _build_folded_weights(((blre_ref, blim_ref), (b2re_ref, b2im_ref)),
                              ((c1re_ref, c1im_ref), (c2re_ref, c2im_ref)), (k0_ref, k1_ref),
                              bq_ref, wg_ref, wu_ref, group_ch, group_states)

    @pl.when(jnp.logical_and(is_prompt, t_in_seq == 0))
    def _():
        carry_ref[...] = jnp.zeros_like(carry_ref)

    def run(n_groups, steps, enter_factory, leave_factory):
        folded = steps // SSM_FOLD
        d_ssm = u_ref.shape[1]
        blocks = _stepmajor_blocks(u_ref[...].astype(F32), uslab_ref, n_groups, steps)
        u_f = jnp.concatenate([jnp.concatenate(blocks[k:k + SSM_FOLD], axis=1)
                               for k in range(0, len(blocks), SSM_FOLD)], axis=0).astype(BF16)

        def u_bundle(b):
            return jnp.concatenate(
                [u_f[:, m * d_ssm + b * SSM_BUNDLE_CH:m * d_ssm + (b + 1) * SSM_BUNDLE_CH]
                 for m in range(SSM_FOLD)], axis=1)

        def b_matmul(b):
            bu_ref[b] = jnp.dot(u_bundle(b), bq_ref[b], preferred_element_type=F32)

        b_matmul(0)
        for b in range(n_bundles):
            if b + 1 < n_bundles:
                b_matmul(b + 1)
            for g in range(n_groups):
                _scan_group(bu_ref, b, g * SUBLANES * folded, folded, width, b * width,
                            (lamre_ref, lamim_ref), enter_factory(g), leave_factory(g))
            y_b = (jnp.dot(bu_ref[b].astype(BF16), wg_ref[b], preferred_element_type=F32)
                   + jnp.dot(u_bundle(b), wu_ref[b], preferred_element_type=F32))
            _store_folded(y_b, yslab_ref, b * (SSM_BUNDLE_CH // LANES), steps)
        y = _load_natural(yslab_ref, n_groups, steps) + d_ref[...] * u_ref[...].astype(F32)
        g_act = _gelu_tanh(y)
        z = jnp.dot(g_act.astype(BF16), wglu_ref[...], preferred_element_type=F32)
        o = g_act * _sigmoid(z) * sza_ref[...].astype(F32)
        ya = jnp.dot(o.astype(BF16), wouta_ref[...], preferred_element_type=F32)
        ya_ref[...] = ya.astype(BF16)

    def prompt_enter(_g):
        def enter(er, ei, tl):
            xr, xi = er, ei
            for step, shift in enumerate((1, 2, 4)):
                xr, xi = _cmuladd(tqre_ref[step, :, tl], tqim_ref[step, :, tl],
                                  pltpu.roll(xr, shift, 0), pltpu.roll(xi, shift, 0), xr, xi)
            c0r = jnp.broadcast_to(carry_ref[0:1, tl], xr.shape)
            c0i = jnp.broadcast_to(carry_ref[1:2, tl], xi.shape)
            xr, xi = _cmuladd(tqre_ref[3, :, tl], tqim_ref[3, :, tl], c0r, c0i, xr, xi)
            carry_ref[0:1, tl] = xr[SUBLANES - 1:SUBLANES, :]
            carry_ref[1:2, tl] = xi[SUBLANES - 1:SUBLANES, :]
            first = lax.broadcasted_iota(jnp.int32, xr.shape, 0) == 0
            return (jnp.where(first, c0r, pltpu.roll(xr, 1, 0)),
                    jnp.where(first, c0i, pltpu.roll(xi, 1, 0)))
        return enter

    def prompt_leave(_g):
        return lambda hr, hi, tl: None

    def sample_enter(g):
        rows = slice(g * SUBLANES, (g + 1) * SUBLANES)
        return lambda er, ei, tl: (h0re_ref[rows, tl], h0im_ref[rows, tl])

    def sample_leave(g):
        rows = slice(g * SUBLANES, (g + 1) * SUBLANES)

        def leave(hr, hi, tl):
            ssre_ref[rows, tl] = hr
            ssim_ref[rows, tl] = hi
        return leave

    @pl.when(is_prompt)
    def _():
        run(1, TILE_M // SUBLANES, prompt_enter, prompt_leave)

    @pl.when(jnp.logical_not(is_prompt))
    def _():
        run(TILE_M // (SUBLANES * sample_steps), sample_steps, sample_enter, sample_leave)

    @pl.when(jnp.logical_and(is_prompt, t_in_seq == tiles_per_seq - 1))
    def _():
        row = pl.ds(i // tiles_per_seq, 1)
        spre_ref[row, :] = carry_ref[0:1, :]
        spim_ref[row, :] = carry_ref[1:2, :]


def _ssm_call(act, h0re, h0im, lam8, tabq, d, bl, b2, c1, c2, kk, w_glu, w_out_a, n_prompt_rows,
              seq_len, sample_steps, group_ch, group_states):
    m = act.shape[0]
    d_ssm, d_model = w_out_a.shape
    n_state = lam8[0].shape[1]
    n_tiles = m // TILE_M
    n_p = n_prompt_rows // TILE_M
    n_batch = n_prompt_rows // seq_len
    seqs_per_tile = TILE_M // sample_steps
    n_sample_seq = h0re.shape[0]
    slab_rows = SUBLANES * _pitch(TILE_M // SUBLANES)
    n_bundles = d_ssm // SSM_BUNDLE_CH
    width = n_state // n_bundles
    assert 2 * width == d_model and n_bundles >= 2 and d_ssm % TILE_M == 0
    assert sample_steps % SSM_FOLD == 0
    kernel = functools.partial(_ssm_kernel, n_prompt_tiles=n_p,
                               tiles_per_seq=seq_len // TILE_M, sample_steps=sample_steps,
                               group_ch=group_ch, group_states=group_states)
    tile_map = lambda i: (i, 0)
    sample_map = lambda i: (jnp.maximum(i - n_p, 0), 0)
    hbm = pl.BlockSpec(memory_space=pl.ANY)
    return pl.pallas_call(
        kernel,
        grid=(n_tiles,),
        in_specs=[
            pl.BlockSpec((TILE_M, d_ssm), tile_map),
            pl.BlockSpec((TILE_M, d_ssm), lambda i: (i, 1)),
            pl.BlockSpec((seqs_per_tile, n_state), sample_map),
            pl.BlockSpec((seqs_per_tile, n_state), sample_map),
            _const_spec(lam8[0].shape), _const_spec(lam8[1].shape),
            _const_spec(tabq[0].shape), _const_spec(tabq[1].shape),
            _const_spec(d.shape),
            _const_spec(bl[0].shape), _const_spec(bl[1].shape),
            _const_spec(b2[0].shape), _const_spec(b2[1].shape),
            _const_spec(c1[0].shape), _const_spec(c1[1].shape),
            _const_spec(c2[0].shape), _const_spec(c2[1].shape),
            _const_spec(kk[0].shape), _const_spec(kk[1].shape),
            hbm, hbm,
        ],
        out_specs=[
            pl.BlockSpec((TILE_M, d_model), tile_map),
            pl.BlockSpec((n_batch, n_state), lambda i: (0, 0)),
            pl.BlockSpec((n_batch, n_state), lambda i: (0, 0)),
            pl.BlockSpec((seqs_per_tile, n_state), sample_map),
            pl.BlockSpec((seqs_per_tile, n_state), sample_map),
        ],
        out_shape=[
            jax.ShapeDtypeStruct((m, d_model), BF16),
            jax.ShapeDtypeStruct((n_batch, n_state), F32),
            jax.ShapeDtypeStruct((n_batch, n_state), F32),
            jax.ShapeDtypeStruct((n_sample_seq, n_state), F32),
            jax.ShapeDtypeStruct((n_sample_seq, n_state), F32),
        ],
        scratch_shapes=[
            pltpu.VMEM((n_bundles, TILE_M // SSM_FOLD, 2 * width), F32),
            pltpu.VMEM((d_ssm // LANES, slab_rows, LANES), F32),
            pltpu.VMEM((d_ssm // LANES, slab_rows, LANES), F32),
            pltpu.VMEM((2, n_state), F32),
            pltpu.VMEM((n_bundles, SSM_FOLD * SSM_BUNDLE_CH, 2 * width), BF16),
            pltpu.VMEM((n_bundles, 2 * width, SSM_FOLD * SSM_BUNDLE_CH), BF16),
            pltpu.VMEM((n_bundles, SSM_FOLD * SSM_BUNDLE_CH, SSM_FOLD * SSM_BUNDLE_CH), BF16),
            pltpu.VMEM((d_ssm, d_ssm), BF16),
            pltpu.VMEM((d_ssm, d_model), BF16),
            pltpu.VMEM((2, STAGE_ROWS, d_ssm), F32),
            pltpu.SemaphoreType.DMA((2,)),
        ],
        compiler_params=pltpu.CompilerParams(
            dimension_semantics=("arbitrary",), vmem_limit_bytes=SSM_VMEM_LIMIT_BYTES),
        name="ssm_branch",
    )(act, act, h0re, h0im, *lam8, *tabq, d, *bl, *b2, *c1, *c2, *kk, w_glu, w_out_a)


def _tail_kernel(q_ref, bz_ref, ya_ref, sga_ref, sgc_ref, xp_ref, xs_ref, hist_ref,
                 cw_ref, lng_ref, lnb_ref, woutc_hbm, wo_hbm,
                 yp_ref, ys_ref,
                 pad_ref, conv_ref, yc_ref, woutc_ref, wo_ref, stage_ref, sem_ref,
                 *, n_prompt_tiles, tiles_per_seq, alpha):
    i = pl.program_id(0)
    is_prompt = i < n_prompt_tiles

    @pl.when(i == 0)
    def _():
        _stage_weight_bf16(woutc_hbm, 0, woutc_ref, stage_ref, sem_ref)
        _stage_weight_bf16(wo_hbm, 0, wo_ref, stage_ref, sem_ref)

    @pl.when(jnp.logical_and(is_prompt, i % tiles_per_seq == 0))
    def _():
        pad_ref[0:CONV_PAD, :] = jnp.zeros((CONV_PAD, pad_ref.shape[1]), F32)

    q = q_ref[...]
    pad_ref[CONV_PAD:CONV_PAD + TILE_M, :] = q
    w0 = cw_ref[0:1, :]
    w1 = cw_ref[1:2, :]
    w2 = cw_ref[2:3, :]

    def finish(q1, q2, x_ref, y_ref):
        conv_ref[...] = w0 * q2 + w1 * q1 + w2 * q
        yc_in = conv_ref[...].astype(BF16) * bz_ref[...]
        yc_ref[...] = jnp.dot(yc_in, woutc_ref[...], preferred_element_type=F32)
        merged = sga_ref[...] * ya_ref[...] + sgc_ref[...] * yc_ref[...].astype(BF16)
        out = jnp.dot(merged, wo_ref[...], preferred_element_type=F32)
        r = alpha * x_ref[...] + out
        mu = jnp.mean(r, axis=-1, keepdims=True)
        rc = r - mu
        var = jnp.mean(rc * rc, axis=-1, keepdims=True)
        y_ref[...] = rc * lax.rsqrt(var + LN_EPS) * lng_ref[...] + lnb_ref[...]

    @pl.when(is_prompt)
    def _():
        q1 = pad_ref[CONV_PAD - 1:CONV_PAD - 1 + TILE_M, :]
        q2 = pad_ref[CONV_PAD - 2:CONV_PAD - 2 + TILE_M, :]
        finish(q1, q2, xp_ref, yp_ref)
        pad_ref[0:CONV_PAD, :] = pad_ref[TILE_M:TILE_M + CONV_PAD, :]

    @pl.when(jnp.logical_not(is_prompt))
    def _():
        t = lax.broadcasted_iota(jnp.int32, q.shape, 0) % SUBLANES
        d_conv = q.shape[1]

        def history(j):
            return jnp.concatenate(
                [jnp.broadcast_to(hist_ref[s:s + 1, j * d_conv:(j + 1) * d_conv],
                                  (SUBLANES, d_conv)) for s in range(TILE_M // SUBLANES)], axis=0)

        e0 = history(0)
        e1 = history(1)
        q1 = jnp.where(t == 0, e1, pad_ref[CONV_PAD - 1:CONV_PAD - 1 + TILE_M, :])
        q2 = jnp.where(t == 0, e0,
                       jnp.where(t == 1, e1, pad_ref[CONV_PAD - 2:CONV_PAD - 2 + TILE_M, :]))
        finish(q1, q2, xs_ref, ys_ref)


def _tail_call(q, act, ya, xp, xs, hist, conv_w, w_out_c, w_o, ln_g, ln_b, seq_len, alpha):
    m, d_conv = q.shape
    m_p, d_model = xp.shape
    m_s = xs.shape[0]
    n_p = m_p // TILE_M
    kernel = functools.partial(_tail_kernel, n_prompt_tiles=n_p,
                               tiles_per_seq=seq_len // TILE_M, alpha=alpha)
    prompt_map = lambda i: (jnp.minimum(i, n_p - 1), 0)
    sample_map = lambda i: (jnp.maximum(i - n_p, 0), 0)
    row_map = lambda i: (i, 0)
    hbm = pl.BlockSpec(memory_space=pl.ANY)
    assert w_out_c.shape[1] == w_o.shape[1] == d_model
    return pl.pallas_call(
        kernel,
        grid=(m // TILE_M,),
        in_specs=[
            pl.BlockSpec((TILE_M, d_conv), row_map),
            pl.BlockSpec((TILE_M, d_conv), lambda i: (i, 2)),
            pl.BlockSpec((TILE_M, d_model), row_map),
            pl.BlockSpec((TILE_M, d_model), lambda i: (i, 2)),
            pl.BlockSpec((TILE_M, d_model), lambda i: (i, 3)),
            pl.BlockSpec((TILE_M, d_model), prompt_map),
            pl.BlockSpec((TILE_M, d_model), sample_map),
            pl.BlockSpec((TILE_M // SUBLANES, hist.shape[1]), sample_map),
            _const_spec(conv_w.shape),
            _const_spec(ln_g.shape),
            _const_spec(ln_b.shape),
            hbm, hbm,
        ],
        out_specs=[
            pl.BlockSpec((TILE_M, d_model), prompt_map),
            pl.BlockSpec((TILE_M, d_model), sample_map),
        ],
        out_shape=[
            jax.ShapeDtypeStruct((m_p, d_model), F32),
            jax.ShapeDtypeStruct((m_s, d_model), F32),
        ],
        scratch_shapes=[
            pltpu.VMEM((TILE_M + CONV_PAD, d_conv), F32),
            pltpu.VMEM((TILE_M, d_conv), F32),
            pltpu.VMEM((TILE_M, d_model), F32),
            pltpu.VMEM(w_out_c.shape, BF16),
            pltpu.VMEM(w_o.shape, BF16),
            pltpu.VMEM((2, STAGE_ROWS, d_model), F32),
            pltpu.SemaphoreType.DMA((2,)),
        ],
        compiler_params=pltpu.CompilerParams(
            dimension_semantics=("arbitrary",), vmem_limit_bytes=VMEM_LIMIT_BYTES),
        name="tail",
    )(q, act, ya, act, act, xp, xs, hist, conv_w, ln_g, ln_b, w_out_c, w_o)


def _ssm_params(a_re, a_im, log_dt, b_re, b_im, c_re, c_im, prompt_steps):
    g, p, gc = b_re.shape
    dt = jnp.exp(log_dt)[:, None]
    mag = jnp.exp(a_re * dt)
    ang = a_im * dt
    lam_re = mag * jnp.cos(ang)
    lam_im = mag * jnp.sin(ang)
    den = a_re * a_re + a_im * a_im
    q_re = ((lam_re - 1.0) * a_re + lam_im * a_im) / den
    q_im = (lam_im * a_re - (lam_re - 1.0) * a_im) / den
    bb_re = q_re[..., None] * b_re - q_im[..., None] * b_im
    bb_im = q_re[..., None] * b_im + q_im[..., None] * b_re

    def channel_rows(bb):
        rows = bb.transpose(0, 2, 1).reshape(g * gc, p)
        return jnp.tile(rows, (1, LANES // p))

    def state_rows(cc):
        rows = cc.transpose(0, 2, 1).reshape(g * p, gc)
        return jnp.tile(rows, (1, LANES // gc))

    def in_rows(kk):
        rows = kk.transpose(0, 2, 1).reshape(g * gc, gc)
        return jnp.tile(rows, (1, LANES // gc))

    def cmul(xr, xi, yr, yi):
        return xr * yr - xi * yi, xr * yi + xi * yr

    lam2_re, lam2_im = cmul(lam_re, lam_im, lam_re, lam_im)
    bl_re, bl_im = cmul(lam_re[..., None], lam_im[..., None], bb_re, bb_im)
    c1_re, c1_im = cmul(c_re, c_im, lam_re[:, None, :], lam_im[:, None, :])
    c2_re, c2_im = cmul(c_re, c_im, lam2_re[:, None, :], lam2_im[:, None, :])
    exact = functools.partial(jnp.einsum, precision=lax.Precision.HIGHEST)
    k0 = exact('gcp,gpd->gcd', c_re, bb_re) - exact('gcp,gpd->gcd', c_im, bb_im)
    k1 = exact('gcp,gpd->gcd', c1_re, bb_re) - exact('gcp,gpd->gcd', c1_im, bb_im)

    bl = (channel_rows(bl_re), channel_rows(bl_im))
    b2 = (channel_rows(bb_re), channel_rows(bb_im))
    c1 = (state_rows(c1_re), -state_rows(c1_im))
    c2 = (state_rows(c2_re), -state_rows(c2_im))
    kk = (in_rows(k0), in_rows(k1))

    lam8 = (jnp.broadcast_to(lam2_re.reshape(1, -1), (SUBLANES, g * p)),
            jnp.broadcast_to(lam2_im.reshape(1, -1), (SUBLANES, g * p)))

    row = jnp.arange(SUBLANES, dtype=F32)[None, :, None]
    shift = jnp.array([1.0, 2.0, 4.0], F32)[:, None, None]
    exponent = jnp.concatenate([jnp.broadcast_to(shift, (3, SUBLANES, 1)), row + 1.0]) * prompt_steps
    keep = jnp.concatenate([row >= shift, jnp.ones((1, SUBLANES, 1), bool)])
    mag_e = jnp.where(keep, jnp.exp(exponent * (a_re * dt).reshape(1, 1, -1)), 0.0)
    ang_e = exponent * ang.reshape(1, 1, -1)
    tabq = (mag_e * jnp.cos(ang_e), mag_e * jnp.sin(ang_e))
    return bl, b2, c1, c2, kk, lam8, tabq


def kernel(x_prompt, x_sample, state_ssm_re, state_ssm_im, state_conv, w_in, ssm_a_re, ssm_a_im, ssm_log_dt, ssm_b_re, ssm_b_im, ssm_c_re, ssm_c_im, ssm_d, w_glu, w_out_a, conv_w, w_out_c, w_o, ln_g, ln_b):
    depth = w_in.shape[0]
    assert depth == 1, "single-layer trunk"
    batch, seq, d_model = x_prompt.shape
    dec_batch, dec_seq, _ = x_sample.shape
    assert dec_seq == SUBLANES and seq % TILE_M == 0 and (dec_batch * dec_seq) % TILE_M == 0
    g, p, gc = ssm_b_re.shape[1:]
    d_ssm = g * gc
    d_conv = conv_w.shape[2]
    n_state = g * p
    alpha = (2 * depth) ** 0.25

    xp = x_prompt.reshape(batch * seq, d_model)
    xs = x_sample.reshape(dec_batch * dec_seq, d_model)
    m_p = xp.shape[0]
    w = w_in[0]

    assert d_ssm == d_conv == PROJ_COLS and d_model == 2 * PROJ_COLS
    act, q = _proj_call(xp, xs, w)

    bl, b2, c1, c2, kk, lam8, tabq = _ssm_params(
        ssm_a_re[0], ssm_a_im[0], ssm_log_dt[0], ssm_b_re[0], ssm_b_im[0], ssm_c_re[0],
        ssm_c_im[0], TILE_M // SUBLANES)
    h0re = state_ssm_re[0].reshape(dec_batch, n_state)
    h0im = state_ssm_im[0].reshape(dec_batch, n_state)
    ya, spre, spim, ssre, ssim = _ssm_call(
        act, h0re, h0im, lam8, tabq, ssm_d[0][None, :], bl, b2, c1, c2, kk, w_glu[0], w_out_a[0],
        m_p, seq, dec_seq, gc, p)

    hist = state_conv[0].reshape(dec_batch, (state_conv.shape[2]) * d_conv)
    yp, ys = _tail_call(q, act, ya, xp, xs, hist, conv_w[0], w_out_c[0], w_o[0],
                        ln_g[0][None, :], ln_b[0][None, :], seq, alpha)

    q8 = q.reshape(-1, SUBLANES, d_conv)
    q_p = q8[seq // SUBLANES - 1:m_p // SUBLANES:seq // SUBLANES, SUBLANES - 2:, :]
    q_s = q8[m_p // SUBLANES:, dec_seq - 2:, :]
    return (yp.reshape(batch, seq, d_model),
            ys.reshape(dec_batch, dec_seq, d_model),
            spre.reshape(1, batch, g, p),
            spim.reshape(1, batch, g, p),
            q_p[None],
            ssre.reshape(1, dec_batch, g, p),
            ssim.reshape(1, dec_batch, g, p),
            q_s[None])
```

```python
import functools
import math

import jax
import jax.numpy as jnp
from jax import lax
from jax.experimental import pallas as pl
from jax.experimental.pallas import tpu as pltpu

F32 = jnp.float32
BF16 = jnp.bfloat16

SUBLANES = 8
LANES = 128
VMEM_LIMIT_BYTES = 56 * 1024 * 1024

TILE_M = 256
PROJ_TILE_M = 512
SSM_FOLD = 2
SSM_BUNDLE_CH = 256
SCAN_CHUNK = 4 * LANES
CONV_PAD = SUBLANES
STAGE_ROWS = 512
NEXT_STAGE_ROWS = 128

LN_EPS = 1e-5
GELU_C = math.sqrt(2.0 / math.pi)


def _sigmoid(x):
    return 0.5 * jnp.tanh(0.5 * x) + 0.5


def _silu(x):
    return x * _sigmoid(x)


def _gelu_tanh(x):
    return 0.5 * x * (1.0 + jnp.tanh(GELU_C * (x + 0.044715 * (x * x * x))))


def _const_spec(shape):
    nd = len(shape)
    return pl.BlockSpec(shape, lambda i: (0,) * nd, pipeline_mode=pl.Buffered(1))


PROJ_GROUPS = ((0, 1), (2, 5), (3, 4), (6, 7), (8, 9))
PROJ_COLS = 1024
Q_GROUP = 2


def _epi_ssm_in(a, b, act_ref, q_ref):
    act_ref[:, :PROJ_COLS] = a.astype(BF16)
    act_ref[:, PROJ_COLS:] = _silu(b).astype(BF16)


def _epi_conv_gate(a, b, act_ref, q_ref):
    act_ref[:, :PROJ_COLS] = (a * _silu(b)).astype(BF16)


def _epi_conv_in(a, b, act_ref, q_ref):
    q_ref[...] = a * b


def _epi_merge_gate(a, b, act_ref, q_ref):
    act_ref[:, :PROJ_COLS] = _sigmoid(a).astype(BF16)
    act_ref[:, PROJ_COLS:] = _sigmoid(b).astype(BF16)


PROJ_EPILOGUES = (_epi_ssm_in, _epi_conv_gate, _epi_conv_in, _epi_merge_gate, _epi_merge_gate)


def _proj_kernel(xp_ref, xs_ref, w_hbm, act_ref, q_ref, wbf_ref, stage_ref, next_ref, sem_ref,
                 *, n_prompt_tiles):
    g = pl.program_id(0)
    i = pl.program_id(1)
    next_rows = next_ref.shape[1]
    n_next = xp_ref.shape[1] // next_rows

    @pl.when(jnp.logical_and(g == 0, i == 0))
    def _():
        for blk, col_block in enumerate(PROJ_GROUPS[0]):
            _stage_weight_bf16(w_hbm, col_block * PROJ_COLS, wbf_ref.at[0, blk], stage_ref, sem_ref)

    rows = pl.ds(pl.multiple_of(i * next_rows, next_rows), next_rows)

    def next_copy(gi, blk):
        col0 = PROJ_GROUPS[gi + 1][blk] * PROJ_COLS
        return pltpu.make_async_copy(w_hbm.at[rows, pl.ds(col0, PROJ_COLS)], next_ref.at[blk],
                                     sem_ref.at[blk])

    for gi, epilogue in enumerate(PROJ_EPILOGUES):
        @pl.when(g == gi)
        def _():
            has_next = gi + 1 < len(PROJ_GROUPS)
            if has_next:
                @pl.when(i < n_next)
                def _():
                    for blk in range(2):
                        next_copy(gi, blk).start()

            slot = gi % 2

            def compute(x_ref):
                if epilogue is _epi_conv_gate:
                    act_ref[:, PROJ_COLS:] = jnp.zeros((act_ref.shape[0], PROJ_COLS), BF16)
                xb = x_ref[...].astype(BF16)
                a = jnp.dot(xb, wbf_ref[slot, 0], preferred_element_type=F32)
                b = jnp.dot(xb, wbf_ref[slot, 1], preferred_element_type=F32)
                epilogue(a, b, act_ref, q_ref)

            @pl.when(i < n_prompt_tiles)
            def _():
                compute(xp_ref)

            @pl.when(i >= n_prompt_tiles)
            def _():
                compute(xs_ref)

            if has_next:
                @pl.when(i < n_next)
                def _():
                    for blk in range(2):
                        next_copy(gi, blk).wait()
                        wbf_ref[1 - slot, blk, rows, :] = next_ref[blk].astype(BF16)


def _proj_call(xp, xs, w_in):
    m_p, d = xp.shape
    m_s = xs.shape[0]
    tile = PROJ_TILE_M
    n_p, n_s = m_p // tile, m_s // tile
    n_tiles = n_p + n_s
    n_groups = len(PROJ_GROUPS)
    act_block = 2 * PROJ_COLS
    assert m_p % tile == 0 and m_s % tile == 0
    assert n_tiles * NEXT_STAGE_ROWS >= d and d % NEXT_STAGE_ROWS == 0
    kernel = functools.partial(_proj_kernel, n_prompt_tiles=n_p)

    def act_map(g, i):
        col = jnp.where(g < Q_GROUP, g, jnp.where(g == Q_GROUP, Q_GROUP - 1, g - 1))
        return jnp.where(g == Q_GROUP, n_tiles - 1, i), col

    def q_map(g, i):
        return jnp.where(g < Q_GROUP, 0, jnp.where(g == Q_GROUP, i, n_tiles - 1)), 0

    return pl.pallas_call(
        kernel,
        grid=(n_groups, n_tiles),
        in_specs=[
            pl.BlockSpec((tile, d), lambda g, i: (jnp.minimum(i, n_p - 1), 0)),
            pl.BlockSpec((tile, d), lambda g, i: (jnp.clip(i - n_p, 0, n_s - 1), 0)),
            pl.BlockSpec(memory_space=pl.ANY),
        ],
        out_specs=[pl.BlockSpec((tile, act_block), act_map),
                   pl.BlockSpec((tile, PROJ_COLS), q_map)],
        out_shape=[jax.ShapeDtypeStruct((m_p + m_s, (n_groups - 1) * act_block), BF16),
                   jax.ShapeDtypeStruct((m_p + m_s, PROJ_COLS), F32)],
        scratch_shapes=[
            pltpu.VMEM((2, 2, d, PROJ_COLS), BF16),
            pltpu.VMEM((2, STAGE_ROWS, PROJ_COLS), F32),
            pltpu.VMEM((2, NEXT_STAGE_ROWS, PROJ_COLS), F32),
            pltpu.SemaphoreType.DMA((2,)),
        ],
        compiler_params=pltpu.CompilerParams(
            dimension_semantics=("arbitrary", "arbitrary"), vmem_limit_bytes=VMEM_LIMIT_BYTES),
        name="proj",
    )(xp, xs, w_in)


def _pitch(steps):
    return steps if (steps // SUBLANES) % 2 == 1 else steps + SUBLANES


def _stepmajor_blocks(val, slab_ref, n_groups, steps):
    pitch = _pitch(steps)
    n_slabs = val.shape[1] // LANES
    n_sub = n_groups * SUBLANES
    for j in range(n_slabs):
        lanes = slice(j * LANES, (j + 1) * LANES)
        if pitch == steps:
            slab_ref[j, 0:n_sub * steps, :] = val[:, lanes]
        else:
            for s in range(n_sub):
                slab_ref[j, s * pitch:s * pitch + steps, :] = val[s * steps:(s + 1) * steps, lanes]
    blocks = []
    for g in range(n_groups):
        for k in range(steps):
            rows = pl.ds(g * SUBLANES * pitch + k, SUBLANES, stride=pitch)
            blocks.append(jnp.concatenate([slab_ref[j, rows, :] for j in range(n_slabs)], axis=1))
    return blocks


def _store_folded(val, slab_ref, slab0, steps):
    pitch = _pitch(steps)
    n = val.shape[1] // SSM_FOLD
    for i in range(val.shape[0] // SUBLANES):
        g, kf = divmod(i, steps // SSM_FOLD)
        for m in range(SSM_FOLD):
            rows = pl.ds(g * SUBLANES * pitch + SSM_FOLD * kf + m, SUBLANES, stride=pitch)
            for j in range(n // LANES):
                slab_ref[slab0 + j, rows, :] = val[i * SUBLANES:(i + 1) * SUBLANES,
                                                   m * n + j * LANES:m * n + (j + 1) * LANES]


def _load_natural(slab_ref, n_groups, steps):
    pitch = _pitch(steps)
    n_slabs = slab_ref.shape[0]
    n_sub = n_groups * SUBLANES
    if pitch == steps:
        return jnp.concatenate([slab_ref[j, 0:n_sub * steps, :] for j in range(n_slabs)], axis=1)
    return jnp.concatenate(
        [jnp.concatenate([slab_ref[j, s * pitch:s * pitch + steps, :] for j in range(n_slabs)],
                         axis=1) for s in range(n_sub)], axis=0)


def _cmuladd(ar, ai, br, bi, cr, ci):
    return ar * br - ai * bi + cr, ar * bi + ai * br + ci


def _scan_group(bu_ref, b, row0, steps, width, lane0, lam_ref, enter_fn, leave_fn):
    for c in range(width // SCAN_CHUNK):
        re = slice(c * SCAN_CHUNK, (c + 1) * SCAN_CHUNK)
        im = slice(width + c * SCAN_CHUNK, width + (c + 1) * SCAN_CHUNK)
        tl = slice(lane0 + c * SCAN_CHUNK, lane0 + (c + 1) * SCAN_CHUNK)
        lr, li = lam_ref[0][:, tl], lam_ref[1][:, tl]

        def x_block(k):
            rows = slice(row0 + k * SUBLANES, row0 + (k + 1) * SUBLANES)
            return rows, bu_ref[b, rows, re], bu_ref[b, rows, im]

        _, pr, pi = x_block(0)
        for k in range(1, steps):
            _, xr, xi = x_block(k)
            pr, pi = _cmuladd(lr, li, pr, pi, xr, xi)
        hr, hi = enter_fn(pr, pi, tl)
        for k in range(steps):
            rows, xr, xi = x_block(k)
            bu_ref[b, rows, re] = hr
            bu_ref[b, rows, im] = hi
            hr, hi = _cmuladd(lr, li, hr, hi, xr, xi)
        leave_fn(hr, hi, tl)


def _build_folded_weights(b_refs, c_refs, k_refs, bq_ref, wg_ref, wu_ref, group_ch, group_states):
    n_bundles = bq_ref.shape[0]
    n_ch = bq_ref.shape[1] // SSM_FOLD
    width = bq_ref.shape[2] // 2
    ch_shift = group_ch.bit_length() - 1
    st_shift = group_states.bit_length() - 1
    assert group_ch == 1 << ch_shift and group_states == 1 << st_shift

    def group_of(n_rows, axis, shift):
        return lax.shift_right_logical(lax.broadcasted_iota(jnp.int32, (n_rows, LANES), axis), shift)

    ch_rows = group_of(n_ch, 0, ch_shift)
    st_rows = group_of(width, 0, st_shift)
    st_lanes = group_of(n_ch, 1, st_shift)
    ch_lanes_for_st = group_of(width, 1, ch_shift)
    ch_lanes_for_ch = group_of(n_ch, 1, ch_shift)
    for b in range(n_bundles):
        ch = slice(b * n_ch, (b + 1) * n_ch)
        st = slice(b * width, (b + 1) * width)
        for pos, pair in enumerate(b_refs):
            for part in range(2):
                src = pair[part][ch, :]
                for j in range(width // LANES):
                    keep = ch_rows == st_lanes + j * (LANES // group_states)
                    lanes = slice(part * width + j * LANES, part * width + (j + 1) * LANES)
                    bq_ref[b, pos * n_ch:(pos + 1) * n_ch, lanes] = (
                        jnp.where(keep, src, 0.0).astype(BF16))
        for pos, pair in enumerate(c_refs):
            for part in range(2):
                src = pair[part][st, :]
                for j in range(n_ch // LANES):
                    keep = st_rows == ch_lanes_for_st + j * (LANES // group_ch)
                    lanes = slice(pos * n_ch + j * LANES, pos * n_ch + (j + 1) * LANES)
                    wg_ref[b, part * width:(part + 1) * width, lanes] = (
                        jnp.where(keep, src, 0.0).astype(BF16))
        for (pos_in, pos_out), k_ref in (((0, 0), k_refs[0]), ((0, 1), k_refs[1]),
                                         ((1, 0), None), ((1, 1), k_refs[0])):
            for j in range(n_ch // LANES):
                lanes = slice(pos_out * n_ch + j * LANES, pos_out * n_ch + (j + 1) * LANES)
                rows = slice(pos_in * n_ch, (pos_in + 1) * n_ch)
                if k_ref is None:
                    wu_ref[b, rows, lanes] = jnp.zeros((n_ch, LANES), BF16)
                else:
                    keep = ch_rows == ch_lanes_for_ch + j * (LANES // group_ch)
                    wu_ref[b, rows, lanes] = jnp.where(keep, k_ref[ch, :], 0.0).astype(BF16)


def _stage_weight_bf16(w_hbm, col0, dst_ref, stage_ref, sem_ref):
    k, n = dst_ref.shape
    rows = stage_ref.shape[1]
    n_chunks = k // rows

    def copy(c):
        return pltpu.make_async_copy(w_hbm.at[pl.ds(c * rows, rows), pl.ds(col0, n)],
                                     stage_ref.at[c % 2], sem_ref.at[c % 2])

    copy(0).start()
    for c in range(n_chunks):
        if c + 1 < n_chunks:
            copy(c + 1).start()
        copy(c).wait()
        dst_ref[c * rows:(c + 1) * rows, :] = stage_ref[c % 2].astype(BF16)


def _ssm_kernel(u_ref, sza_ref, h0re_ref, h0im_ref, lamre_ref, lamim_ref, tqre_ref, tqim_ref, d_ref,
                bst_ref, cst_ref, kst_ref, wglu_hbm, wouta_hbm,
                ya_ref, spre_ref, spim_ref, ssre_ref, ssim_ref,
                bu_ref, uslab_ref, yslab_ref, carry_ref, bq_ref, wg_ref, wu_ref,
                wglu_ref, wouta_ref, stage_narrow_ref, sem_ref,
                *, n_prompt_tiles, tiles_per_seq, sample_steps, group_ch, group_states):
    i = pl.program_id(0)
    n_bundles = bq_ref.shape[0]
    width = bq_ref.shape[2] // 2
    is_prompt = i < n_prompt_tiles
    t_in_seq = i % tiles_per_seq

    @pl.when(i == 0)
    def _():
        _stage_weight_bf16(wouta_hbm, 0, wouta_ref, bu_ref, sem_ref)
        _stage_weight_bf16(wglu_hbm, 0, wglu_ref, stage_narrow_ref, sem_ref)
        _build_folded_weights(((bst_ref.at[0], bst_ref.at[1]), (bst_ref.at[2], bst_ref.at[3])),
                              ((cst_ref.at[0], cst_ref.at[1]), (cst_ref.at[2], cst_ref.at[3])),
                              (kst_ref.at[0], kst_ref.at[1]),
                              bq_ref, wg_ref, wu_ref, group_ch, group_states)

    @pl.when(jnp.logical_and(is_prompt, t_in_seq == 0))
    def _():
        carry_ref[...] = jnp.zeros_like(carry_ref)

    def run(n_groups, steps, enter_factory, leave_factory):
        folded = steps // SSM_FOLD
        d_ssm = u_ref.shape[1]
        blocks = _stepmajor_blocks(u_ref[...].astype(F32), uslab_ref, n_groups, steps)
        u_f = jnp.concatenate([jnp.concatenate(blocks[k:k + SSM_FOLD], axis=1)
                               for k in range(0, len(blocks), SSM_FOLD)], axis=0).astype(BF16)

        def u_bundle(b):
            return jnp.concatenate(
                [u_f[:, m * d_ssm + b * SSM_BUNDLE_CH:m * d_ssm + (b + 1) * SSM_BUNDLE_CH]
                 for m in range(SSM_FOLD)], axis=1)

        def b_matmul(b):
            bu_ref[b] = jnp.dot(u_bundle(b), bq_ref[b], preferred_element_type=F32)

        b_matmul(0)
        for b in range(n_bundles):
            if b + 1 < n_bundles:
                b_matmul(b + 1)
            for g in range(n_groups):
                _scan_group(bu_ref, b, g * SUBLANES * folded, folded, width, b * width,
                            (lamre_ref, lamim_ref), enter_factory(g), leave_factory(g))
            y_b = (jnp.dot(bu_ref[b].astype(BF16), wg_ref[b], preferred_element_type=F32)
                   + jnp.dot(u_bundle(b), wu_ref[b], preferred_element_type=F32))
            _store_folded(y_b, yslab_ref, b * (SSM_BUNDLE_CH // LANES), steps)
        y = _load_natural(yslab_ref, n_groups, steps) + d_ref[...] * u_ref[...].astype(F32)
        g_act = _gelu_tanh(y)
        z = jnp.dot(g_act.astype(BF16), wglu_ref[...], preferred_element_type=F32)
        o = g_act * _sigmoid(z) * sza_ref[...].astype(F32)
        ya = jnp.dot(o.astype(BF16), wouta_ref[...], preferred_element_type=F32)
        ya_ref[...] = ya.astype(BF16)

    def prompt_enter(_g):
        def enter(er, ei, tl):
            xr, xi = er, ei
            for step, shift in enumerate((1, 2, 4)):
                xr, xi = _cmuladd(tqre_ref[step, :, tl], tqim_ref[step, :, tl],
                                  pltpu.roll(xr, shift, 0), pltpu.roll(xi, shift, 0), xr, xi)
            c0r = jnp.broadcast_to(carry_ref[0:1, tl], xr.shape)
            c0i = jnp.broadcast_to(carry_ref[1:2, tl], xi.shape)
            xr, xi = _cmuladd(tqre_ref[3, :, tl], tqim_ref[3, :, tl], c0r, c0i, xr, xi)
            carry_ref[0:1, tl] = xr[SUBLANES - 1:SUBLANES, :]
            carry_ref[1:2, tl] = xi[SUBLANES - 1:SUBLANES, :]
            first = lax.broadcasted_iota(jnp.int32, xr.shape, 0) == 0
            return (jnp.where(first, c0r, pltpu.roll(xr, 1, 0)),
                    jnp.where(first, c0i, pltpu.roll(xi, 1, 0)))
        return enter

    def prompt_leave(_g):
        return lambda hr, hi, tl: None

    def sample_enter(g):
        rows = slice(g * SUBLANES, (g + 1) * SUBLANES)
        return lambda er, ei, tl: (h0re_ref[rows, tl], h0im_ref[rows, tl])

    def sample_leave(g):
        rows = slice(g * SUBLANES, (g + 1) * SUBLANES)

        def leave(hr, hi, tl):
            ssre_ref[rows, tl] = hr
            ssim_ref[rows, tl] = hi
        return leave

    @pl.when(is_prompt)
    def _():
        run(1, TILE_M // SUBLANES, prompt_enter, prompt_leave)

    @pl.when(jnp.logical_not(is_prompt))
    def _():
        run(TILE_M // (SUBLANES * sample_steps), sample_steps, sample_enter, sample_leave)

    @pl.when(jnp.logical_and(is_prompt, t_in_seq == tiles_per_seq - 1))
    def _():
        row = pl.ds(i // tiles_per_seq, 1)
        spre_ref[row, :] = carry_ref[0:1, :]
        spim_ref[row, :] = carry_ref[1:2, :]


def _ssm_call(act, h0re, h0im, lam8, tabq, d, bst, cst, kst, w_glu, w_out_a, n_prompt_rows,
              seq_len, sample_steps, group_ch, group_states):
    m = act.shape[0]
    d_ssm, d_model = w_out_a.shape
    n_state = lam8[0].shape[1]
    n_tiles = m // TILE_M
    n_p = n_prompt_rows // TILE_M
    n_batch = n_prompt_rows // seq_len
    seqs_per_tile = TILE_M // sample_steps
    n_sample_seq = h0re.shape[0]
    slab_rows = SUBLANES * _pitch(TILE_M // SUBLANES)
    n_bundles = d_ssm // SSM_BUNDLE_CH
    width = n_state // n_bundles
    assert 2 * width == d_model and n_bundles >= 2 and d_ssm % TILE_M == 0
    assert sample_steps % SSM_FOLD == 0
    kernel = functools.partial(_ssm_kernel, n_prompt_tiles=n_p,
                               tiles_per_seq=seq_len // TILE_M, sample_steps=sample_steps,
                               group_ch=group_ch, group_states=group_states)
    tile_map = lambda i: (i, 0)
    sample_map = lambda i: (jnp.maximum(i - n_p, 0), 0)
    hbm = pl.BlockSpec(memory_space=pl.ANY)
    return pl.pallas_call(
        kernel,
        grid=(n_tiles,),
        in_specs=[
            pl.BlockSpec((TILE_M, d_ssm), tile_map),
            pl.BlockSpec((TILE_M, d_ssm), lambda i: (i, 1)),
            pl.BlockSpec((seqs_per_tile, n_state), sample_map),
            pl.BlockSpec((seqs_per_tile, n_state), sample_map),
            _const_spec(lam8[0].shape), _const_spec(lam8[1].shape),
            _const_spec(tabq[0].shape), _const_spec(tabq[1].shape),
            _const_spec(d.shape),
            _const_spec(bst.shape), _const_spec(cst.shape), _const_spec(kst.shape),
            hbm, hbm,
        ],
        out_specs=[
            pl.BlockSpec((TILE_M, d_model), tile_map),
            pl.BlockSpec((n_batch, n_state), lambda i: (0, 0)),
            pl.BlockSpec((n_batch, n_state), lambda i: (0, 0)),
            pl.BlockSpec((seqs_per_tile, n_state), sample_map),
            pl.BlockSpec((seqs_per_tile, n_state), sample_map),
        ],
        out_shape=[
            jax.ShapeDtypeStruct((m, d_model), BF16),
            jax.ShapeDtypeStruct((n_batch, n_state), F32),
            jax.ShapeDtypeStruct((n_batch, n_state), F32),
            jax.ShapeDtypeStruct((n_sample_seq, n_state), F32),
            jax.ShapeDtypeStruct((n_sample_seq, n_state), F32),
        ],
        scratch_shapes=[
            pltpu.VMEM((n_bundles, TILE_M // SSM_FOLD, 2 * width), F32),
            pltpu.VMEM((d_ssm // LANES, slab_rows, LANES), F32),
            pltpu.VMEM((d_ssm // LANES, slab_rows, LANES), F32),
            pltpu.VMEM((2, n_state), F32),
            pltpu.VMEM((n_bundles, SSM_FOLD * SSM_BUNDLE_CH, 2 * width), BF16),
            pltpu.VMEM((n_bundles, 2 * width, SSM_FOLD * SSM_BUNDLE_CH), BF16),
            pltpu.VMEM((n_bundles, SSM_FOLD * SSM_BUNDLE_CH, SSM_FOLD * SSM_BUNDLE_CH), BF16),
            pltpu.VMEM((d_ssm, d_ssm), BF16),
            pltpu.VMEM((d_ssm, d_model), BF16),
            pltpu.VMEM((2, STAGE_ROWS // 2, d_ssm), F32),
            pltpu.SemaphoreType.DMA((2,)),
        ],
        compiler_params=pltpu.CompilerParams(
            dimension_semantics=("arbitrary",), vmem_limit_bytes=VMEM_LIMIT_BYTES),
        name="ssm_branch",
    )(act, act, h0re, h0im, *lam8, *tabq, d, bst, cst, kst, w_glu, w_out_a)


def _tail_kernel(q_ref, bz_ref, ya_ref, sga_ref, sgc_ref, xp_ref, xs_ref, hist_ref,
                 cw_ref, lng_ref, lnb_ref, woutc_hbm, wo_hbm,
                 yp_ref, ys_ref,
                 pad_ref, conv_ref, yc_ref, woutc_ref, wo_ref, stage_ref, sem_ref,
                 *, n_prompt_tiles, tiles_per_seq, alpha):
    i = pl.program_id(0)
    is_prompt = i < n_prompt_tiles

    @pl.when(i == 0)
    def _():
        _stage_weight_bf16(woutc_hbm, 0, woutc_ref, stage_ref, sem_ref)
        _stage_weight_bf16(wo_hbm, 0, wo_ref, stage_ref, sem_ref)

    @pl.when(jnp.logical_and(is_prompt, i % tiles_per_seq == 0))
    def _():
        pad_ref[0:CONV_PAD, :] = jnp.zeros((CONV_PAD, pad_ref.shape[1]), F32)

    q = q_ref[...]
    pad_ref[CONV_PAD:CONV_PAD + TILE_M, :] = q
    w0 = cw_ref[0:1, :]
    w1 = cw_ref[1:2, :]
    w2 = cw_ref[2:3, :]

    def finish(q1, q2, x_ref, y_ref):
        conv_ref[...] = w0 * q2 + w1 * q1 + w2 * q
        yc_in = conv_ref[...].astype(BF16) * bz_ref[...]
        yc_ref[...] = jnp.dot(yc_in, woutc_ref[...], preferred_element_type=F32)
        merged = sga_ref[...] * ya_ref[...] + sgc_ref[...] * yc_ref[...].astype(BF16)
        out = jnp.dot(merged, wo_ref[...], preferred_element_type=F32)
        r = alpha * x_ref[...] + out
        mu = jnp.mean(r, axis=-1, keepdims=True)
        rc = r - mu
        var = jnp.mean(rc * rc, axis=-1, keepdims=True)
        y_ref[...] = rc * lax.rsqrt(var + LN_EPS) * lng_ref[...] + lnb_ref[...]

    @pl.when(is_prompt)
    def _():
        q1 = pad_ref[CONV_PAD - 1:CONV_PAD - 1 + TILE_M, :]
        q2 = pad_ref[CONV_PAD - 2:CONV_PAD - 2 + TILE_M, :]
        finish(q1, q2, xp_ref, yp_ref)
        pad_ref[0:CONV_PAD, :] = pad_ref[TILE_M:TILE_M + CONV_PAD, :]

    @pl.when(jnp.logical_not(is_prompt))
    def _():
        t = lax.broadcasted_iota(jnp.int32, q.shape, 0) % SUBLANES
        d_conv = q.shape[1]

        def history(j):
            return jnp.concatenate(
                [jnp.broadcast_to(hist_ref[s:s + 1, j * d_conv:(j + 1) * d_conv],
                                  (SUBLANES, d_conv)) for s in range(TILE_M // SUBLANES)], axis=0)

        e0 = history(0)
        e1 = history(1)
        q1 = jnp.where(t == 0, e1, pad_ref[CONV_PAD - 1:CONV_PAD - 1 + TILE_M, :])
        q2 = jnp.where(t == 0, e0,
                       jnp.where(t == 1, e1, pad_ref[CONV_PAD - 2:CONV_PAD - 2 + TILE_M, :]))
        finish(q1, q2, xs_ref, ys_ref)


def _tail_call(q, act, ya, xp, xs, hist, conv_w, w_out_c, w_o, ln_g, ln_b, seq_len, alpha):
    m, d_conv = q.shape
    m_p, d_model = xp.shape
    m_s = xs.shape[0]
    n_p = m_p // TILE_M
    kernel = functools.partial(_tail_kernel, n_prompt_tiles=n_p,
                               tiles_per_seq=seq_len // TILE_M, alpha=alpha)
    prompt_map = lambda i: (jnp.minimum(i, n_p - 1), 0)
    sample_map = lambda i: (jnp.maximum(i - n_p, 0), 0)
    row_map = lambda i: (i, 0)
    hbm = pl.BlockSpec(memory_space=pl.ANY)
    assert w_out_c.shape[1] == w_o.shape[1] == d_model
    return pl.pallas_call(
        kernel,
        grid=(m // TILE_M,),
        in_specs=[
            pl.BlockSpec((TILE_M, d_conv), row_map),
            pl.BlockSpec((TILE_M, d_conv), lambda i: (i, 2)),
            pl.BlockSpec((TILE_M, d_model), row_map),
            pl.BlockSpec((TILE_M, d_model), lambda i: (i, 2)),
            pl.BlockSpec((TILE_M, d_model), lambda i: (i, 3)),
            pl.BlockSpec((TILE_M, d_model), prompt_map),
            pl.BlockSpec((TILE_M, d_model), sample_map),
            pl.BlockSpec((TILE_M // SUBLANES, hist.shape[1]), sample_map),
            _const_spec(conv_w.shape),
            _const_spec(ln_g.shape),
            _const_spec(ln_b.shape),
            hbm, hbm,
        ],
        out_specs=[
            pl.BlockSpec((TILE_M, d_model), prompt_map),
            pl.BlockSpec((TILE_M, d_model), sample_map),
        ],
        out_shape=[
            jax.ShapeDtypeStruct((m_p, d_model), F32),
            jax.ShapeDtypeStruct((m_s, d_model), F32),
        ],
        scratch_shapes=[
            pltpu.VMEM((TILE_M + CONV_PAD, d_conv), F32),
            pltpu.VMEM((TILE_M, d_conv), F32),
            pltpu.VMEM((TILE_M, d_model), F32),
            pltpu.VMEM(w_out_c.shape, BF16),
            pltpu.VMEM(w_o.shape, BF16),
            pltpu.VMEM((2, STAGE_ROWS, d_model), F32),
            pltpu.SemaphoreType.DMA((2,)),
        ],
        compiler_params=pltpu.CompilerParams(
            dimension_semantics=("arbitrary",), vmem_limit_bytes=VMEM_LIMIT_BYTES),
        name="tail",
    )(q, act, ya, act, act, xp, xs, hist, conv_w, ln_g, ln_b, w_out_c, w_o)


def _ssm_params(a_re, a_im, log_dt, b_re, b_im, c_re, c_im, prompt_steps):
    g, p, gc = b_re.shape
    dt = jnp.exp(log_dt)[:, None]
    mag = jnp.exp(a_re * dt)
    ang = a_im * dt
    lam_re = mag * jnp.cos(ang)
    lam_im = mag * jnp.sin(ang)
    den = a_re * a_re + a_im * a_im
    q_re = ((lam_re - 1.0) * a_re + lam_im * a_im) / den
    q_im = (lam_im * a_re - (lam_re - 1.0) * a_im) / den
    bb_re = q_re[..., None] * b_re - q_im[..., None] * b_im
    bb_im = q_re[..., None] * b_im + q_im[..., None] * b_re

    def partner_rows(stack):
        n, _, a, b = stack.shape
        rows = stack.transpose(0, 1, 3, 2).reshape(n, g * b, a)
        return jnp.tile(rows, (1, 1, LANES // a))

    def cmul(xr, xi, yr, yi):
        return xr * yr - xi * yi, xr * yi + xi * yr

    lam2_re, lam2_im = cmul(lam_re, lam_im, lam_re, lam_im)
    bl_re, bl_im = cmul(lam_re[..., None], lam_im[..., None], bb_re, bb_im)
    c1_re, c1_im = cmul(c_re, c_im, lam_re[:, None, :], lam_im[:, None, :])
    c2_re, c2_im = cmul(c_re, c_im, lam2_re[:, None, :], lam2_im[:, None, :])
    exact = functools.partial(jnp.einsum, precision=lax.Precision.HIGHEST)
    k0 = exact('gcp,gpd->gcd', c_re, bb_re) - exact('gcp,gpd->gcd', c_im, bb_im)
    k1 = exact('gcp,gpd->gcd', c1_re, bb_re) - exact('gcp,gpd->gcd', c1_im, bb_im)

    bst = partner_rows(jnp.stack([bl_re, bl_im, bb_re, bb_im]))
    cst = partner_rows(jnp.stack([c1_re, -c1_im, c2_re, -c2_im]))
    kst = partner_rows(jnp.stack([k0, k1]))

    lam8 = (jnp.broadcast_to(lam2_re.reshape(1, -1), (SUBLANES, g * p)),
            jnp.broadcast_to(lam2_im.reshape(1, -1), (SUBLANES, g * p)))

    row = jnp.arange(SUBLANES, dtype=F32)[None, :, None]
    shift = jnp.array([1.0, 2.0, 4.0], F32)[:, None, None]
    exponent = jnp.concatenate([jnp.broadcast_to(shift, (3, SUBLANES, 1)), row + 1.0]) * prompt_steps
    keep = jnp.concatenate([row >= shift, jnp.ones((1, SUBLANES, 1), bool)])
    mag_e = jnp.where(keep, jnp.exp(exponent * (a_re * dt).reshape(1, 1, -1)), 0.0)
    ang_e = exponent * ang.reshape(1, 1, -1)
    tabq = (mag_e * jnp.cos(ang_e), mag_e * jnp.sin(ang_e))
    return bst, cst, kst, lam8, tabq


def kernel(x_prompt, x_sample, state_ssm_re, state_ssm_im, state_conv, w_in, ssm_a_re, ssm_a_im, ssm_log_dt, ssm_b_re, ssm_b_im, ssm_c_re, ssm_c_im, ssm_d, w_glu, w_out_a, conv_w, w_out_c, w_o, ln_g, ln_b):
    depth = w_in.shape[0]
    assert depth == 1, "single-layer trunk"
    batch, seq, d_model = x_prompt.shape
    dec_batch, dec_seq, _ = x_sample.shape
    assert dec_seq == SUBLANES and seq % TILE_M == 0 and (dec_batch * dec_seq) % TILE_M == 0
    g, p, gc = ssm_b_re.shape[1:]
    d_ssm = g * gc
    d_conv = conv_w.shape[2]
    n_state = g * p
    alpha = (2 * depth) ** 0.25

    xp = x_prompt.reshape(batch * seq, d_model)
    xs = x_sample.reshape(dec_batch * dec_seq, d_model)
    m_p = xp.shape[0]
    w = w_in[0]

    assert d_ssm == d_conv == PROJ_COLS and d_model == 2 * PROJ_COLS
    act, q = _proj_call(xp, xs, w)

    bst, cst, kst, lam8, tabq = _ssm_params(
        ssm_a_re[0], ssm_a_im[0], ssm_log_dt[0], ssm_b_re[0], ssm_b_im[0], ssm_c_re[0],
        ssm_c_im[0], TILE_M // SUBLANES)
    h0re = state_ssm_re[0].reshape(dec_batch, n_state)
    h0im = state_ssm_im[0].reshape(dec_batch, n_state)
    ya, spre, spim, ssre, ssim = _ssm_call(
        act, h0re, h0im, lam8, tabq, ssm_d[0][None, :], bst, cst, kst, w_glu[0], w_out_a[0],
        m_p, seq, dec_seq, gc, p)

    hist = state_conv[0].reshape(dec_batch, (state_conv.shape[2]) * d_conv)
    yp, ys = _tail_call(q, act, ya, xp, xs, hist, conv_w[0], w_out_c[0], w_o[0],
                        ln_g[0][None, :], ln_b[0][None, :], seq, alpha)

    q8 = q.reshape(-1, SUBLANES, d_conv)
    q_p = q8[seq // SUBLANES - 1:m_p // SUBLANES:seq // SUBLANES, SUBLANES - 2:, :]
    q_s = q8[m_p // SUBLANES:, dec_seq - 2:, :]
    return (yp.reshape(batch, seq, d_model),
            ys.reshape(dec_batch, dec_seq, d_model),
            spre.reshape(1, batch, g, p),
            spim.reshape(1, batch, g, p),
            q_p[None],
            ssre.reshape(1, dec_batch, g, p),
            ssim.reshape(1, dec_batch, g, p),
            q_s[None])
```

```python
import functools
import math

import jax
import jax.numpy as jnp
from jax import lax
from jax.experimental import pallas as pl
from jax.experimental.pallas import tpu as pltpu

F32 = jnp.float32
BF16 = jnp.bfloat16

SUBLANES = 8
LANES = 128
VMEM_LIMIT_BYTES = 56 * 1024 * 1024

TILE_M = 256
PROJ_TILE_M = 512
SSM_FOLD = 2
SSM_BUNDLE_CH = 256
SCAN_CHUNK = 4 * LANES
CONV_PAD = SUBLANES
STAGE_ROWS = 512
NEXT_STAGE_ROWS = 128

LN_EPS = 1e-5
GELU_C = math.sqrt(2.0 / math.pi)


def _sigmoid(x):
    return 0.5 * jnp.tanh(0.5 * x) + 0.5


def _silu(x):
    return x * _sigmoid(x)


def _gelu_tanh(x):
    return 0.5 * x * (1.0 + jnp.tanh(GELU_C * (x + 0.044715 * (x * x * x))))


def _const_spec(shape):
    nd = len(shape)
    return pl.BlockSpec(shape, lambda i: (0,) * nd, pipeline_mode=pl.Buffered(1))


PROJ_GROUPS = ((0, 1), (2, 5), (3, 4), (6, 7), (8, 9))
PROJ_COLS = 1024
Q_GROUP = 2


def _epi_ssm_in(a, b, act_ref, q_ref):
    act_ref[:, :PROJ_COLS] = a.astype(BF16)
    act_ref[:, PROJ_COLS:] = _silu(b).astype(BF16)


def _epi_conv_gate(a, b, act_ref, q_ref):
    act_ref[:, :PROJ_COLS] = (a * _silu(b)).astype(BF16)


def _epi_conv_in(a, b, act_ref, q_ref):
    q_ref[...] = a * b


def _epi_merge_gate(a, b, act_ref, q_ref):
    act_ref[:, :PROJ_COLS] = _sigmoid(a).astype(BF16)
    act_ref[:, PROJ_COLS:] = _sigmoid(b).astype(BF16)


PROJ_EPILOGUES = (_epi_ssm_in, _epi_conv_gate, _epi_conv_in, _epi_merge_gate, _epi_merge_gate)


def _proj_kernel(xp_ref, xs_ref, w_hbm, act_ref, q_ref, wbf_ref, stage_ref, next_ref, sem_ref,
                 *, n_prompt_tiles):
    g = pl.program_id(0)
    i = pl.program_id(1)
    next_rows = next_ref.shape[1]
    n_next = xp_ref.shape[1] // next_rows

    @pl.when(jnp.logical_and(g == 0, i == 0))
    def _():
        for blk, col_block in enumerate(PROJ_GROUPS[0]):
            _stage_weight_bf16(w_hbm, col_block * PROJ_COLS, wbf_ref.at[0, blk], stage_ref, sem_ref)

    rows = pl.ds(pl.multiple_of(i * next_rows, next_rows), next_rows)

    def next_copy(gi, blk):
        col0 = PROJ_GROUPS[gi + 1][blk] * PROJ_COLS
        return pltpu.make_async_copy(w_hbm.at[rows, pl.ds(col0, PROJ_COLS)], next_ref.at[blk],
                                     sem_ref.at[blk])

    for gi, epilogue in enumerate(PROJ_EPILOGUES):
        @pl.when(g == gi)
        def _():
            has_next = gi + 1 < len(PROJ_GROUPS)
            if has_next:
                @pl.when(i < n_next)
                def _():
                    for blk in range(2):
                        next_copy(gi, blk).start()

            slot = gi % 2

            def compute(x_ref):
                if epilogue is _epi_conv_gate:
                    act_ref[:, PROJ_COLS:] = jnp.zeros((act_ref.shape[0], PROJ_COLS), BF16)
                xb = x_ref[...].astype(BF16)
                a = jnp.dot(xb, wbf_ref[slot, 0], preferred_element_type=F32)
                b = jnp.dot(xb, wbf_ref[slot, 1], preferred_element_type=F32)
                epilogue(a, b, act_ref, q_ref)

            @pl.when(i < n_prompt_tiles)
            def _():
                compute(xp_ref)

            @pl.when(i >= n_prompt_tiles)
            def _():
                compute(xs_ref)

            if has_next:
                @pl.when(i < n_next)
                def _():
                    for blk in range(2):
                        next_copy(gi, blk).wait()
                        wbf_ref[1 - slot, blk, rows, :] = next_ref[blk].astype(BF16)


def _proj_call(xp, xs, w_in):
    m_p, d = xp.shape
    m_s = xs.shape[0]
    tile = PROJ_TILE_M
    n_p, n_s = m_p // tile, m_s // tile
    n_tiles = n_p + n_s
    n_groups = len(PROJ_GROUPS)
    act_block = 2 * PROJ_COLS
    assert m_p % tile == 0 and m_s % tile == 0
    assert n_tiles * NEXT_STAGE_ROWS >= d and d % NEXT_STAGE_ROWS == 0
    kernel = functools.partial(_proj_kernel, n_prompt_tiles=n_p)

    def act_map(g, i):
        col = jnp.where(g < Q_GROUP, g, jnp.where(g == Q_GROUP, Q_GROUP - 1, g - 1))
        return jnp.where(g == Q_GROUP, n_tiles - 1, i), col

    def q_map(g, i):
        return jnp.where(g < Q_GROUP, 0, jnp.where(g == Q_GROUP, i, n_tiles - 1)), 0

    return pl.pallas_call(
        kernel,
        grid=(n_groups, n_tiles),
        in_specs=[
            pl.BlockSpec((tile, d), lambda g, i: (jnp.minimum(i, n_p - 1), 0)),
            pl.BlockSpec((tile, d), lambda g, i: (jnp.clip(i - n_p, 0, n_s - 1), 0)),
            pl.BlockSpec(memory_space=pl.ANY),
        ],
        out_specs=[pl.BlockSpec((tile, act_block), act_map),
                   pl.BlockSpec((tile, PROJ_COLS), q_map)],
        out_shape=[jax.ShapeDtypeStruct((m_p + m_s, (n_groups - 1) * act_block), BF16),
                   jax.ShapeDtypeStruct((m_p + m_s, PROJ_COLS), F32)],
        scratch_shapes=[
            pltpu.VMEM((2, 2, d, PROJ_COLS), BF16),
            pltpu.VMEM((2, STAGE_ROWS, PROJ_COLS), F32),
            pltpu.VMEM((2, NEXT_STAGE_ROWS, PROJ_COLS), F32),
            pltpu.SemaphoreType.DMA((2,)),
        ],
        compiler_params=pltpu.CompilerParams(
            dimension_semantics=("arbitrary", "arbitrary"), vmem_limit_bytes=VMEM_LIMIT_BYTES),
        name="proj",
    )(xp, xs, w_in)


def _pitch(steps):
    return steps if (steps // SUBLANES) % 2 == 1 else steps + SUBLANES


def _stepmajor_blocks(val, slab_ref, n_groups, steps):
    pitch = _pitch(steps)
    n_slabs = val.shape[1] // LANES
    n_sub = n_groups * SUBLANES
    for j in range(n_slabs):
        lanes = slice(j * LANES, (j + 1) * LANES)
        if pitch == steps:
            slab_ref[j, 0:n_sub * steps, :] = val[:, lanes]
        else:
            for s in range(n_sub):
                slab_ref[j, s * pitch:s * pitch + steps, :] = val[s * steps:(s + 1) * steps, lanes]
    blocks = []
    for g in range(n_groups):
        for k in range(steps):
            rows = pl.ds(g * SUBLANES * pitch + k, SUBLANES, stride=pitch)
            blocks.append(jnp.concatenate([slab_ref[j, rows, :] for j in range(n_slabs)], axis=1))
    return blocks


def _store_folded(val, slab_ref, slab0, steps):
    pitch = _pitch(steps)
    n = val.shape[1] // SSM_FOLD
    for i in range(val.shape[0] // SUBLANES):
        g, kf = divmod(i, steps // SSM_FOLD)
        for m in range(SSM_FOLD):
            rows = pl.ds(g * SUBLANES * pitch + SSM_FOLD * kf + m, SUBLANES, stride=pitch)
            for j in range(n // LANES):
                slab_ref[slab0 + j, rows, :] = val[i * SUBLANES:(i + 1) * SUBLANES,
                                                   m * n + j * LANES:m * n + (j + 1) * LANES]


def _load_natural(slab_ref, n_groups, steps):
    pitch = _pitch(steps)
    n_slabs = slab_ref.shape[0]
    n_sub = n_groups * SUBLANES
    if pitch == steps:
        return jnp.concatenate([slab_ref[j, 0:n_sub * steps, :] for j in range(n_slabs)], axis=1)
    return jnp.concatenate(
        [jnp.concatenate([slab_ref[j, s * pitch:s * pitch + steps, :] for j in range(n_slabs)],
                         axis=1) for s in range(n_sub)], axis=0)


def _cmuladd(ar, ai, br, bi, cr, ci):
    return ar * br - ai * bi + cr, ar * bi + ai * br + ci


def _scan_group(bu_ref, b, row0, steps, width, lane0, lam_ref, enter_fn, leave_fn):
    for c in range(width // SCAN_CHUNK):
        re = slice(c * SCAN_CHUNK, (c + 1) * SCAN_CHUNK)
        im = slice(width + c * SCAN_CHUNK, width + (c + 1) * SCAN_CHUNK)
        tl = slice(lane0 + c * SCAN_CHUNK, lane0 + (c + 1) * SCAN_CHUNK)
        lr, li = lam_ref[0][:, tl], lam_ref[1][:, tl]

        def x_block(k):
            rows = slice(row0 + k * SUBLANES, row0 + (k + 1) * SUBLANES)
            return rows, bu_ref[b, rows, re], bu_ref[b, rows, im]

        _, pr, pi = x_block(0)
        for k in range(1, steps):
            _, xr, xi = x_block(k)
            pr, pi = _cmuladd(lr, li, pr, pi, xr, xi)
        hr, hi = enter_fn(pr, pi, tl)
        for k in range(steps):
            rows, xr, xi = x_block(k)
            bu_ref[b, rows, re] = hr
            bu_ref[b, rows, im] = hi
            hr, hi = _cmuladd(lr, li, hr, hi, xr, xi)
        leave_fn(hr, hi, tl)


def _build_folded_weights(b2_ref, ct_ref, lam_ref, bq_ref, wg_ref, wu_ref, group_ch, group_states):
    n_bundles = bq_ref.shape[0]
    n_ch = bq_ref.shape[1] // SSM_FOLD
    width = bq_ref.shape[2] // 2
    ch_shift = group_ch.bit_length() - 1
    st_shift = group_states.bit_length() - 1
    assert group_ch == 1 << ch_shift and group_states == 1 << st_shift
    groups = n_ch // group_ch

    def group_of(n_rows, axis, shift):
        return lax.shift_right_logical(lax.broadcasted_iota(jnp.int32, (n_rows, LANES), axis), shift)

    ch_rows = group_of(n_ch, 0, ch_shift)
    st_rows = group_of(width, 0, st_shift)
    st_lanes = group_of(n_ch, 1, st_shift)
    ch_lanes = group_of(width, 1, ch_shift)

    def fill_b(b, pos, src_pair):
        for part in range(2):
            for j in range(width // LANES):
                keep = ch_rows == st_lanes + j * (LANES // group_states)
                lanes = slice(part * width + j * LANES, part * width + (j + 1) * LANES)
                bq_ref[b, pos * n_ch:(pos + 1) * n_ch, lanes] = (
                    jnp.where(keep, src_pair[part], 0.0).astype(BF16))

    def fill_c(b, pos, src_pair):
        for part in range(2):
            for j in range(n_ch // LANES):
                keep = st_rows == ch_lanes + j * (LANES // group_ch)
                lanes = slice(pos * n_ch + j * LANES, pos * n_ch + (j + 1) * LANES)
                wg_ref[b, part * width:(part + 1) * width, lanes] = (
                    jnp.where(keep, src_pair[part], 0.0).astype(BF16))

    for b in range(n_bundles):
        ch = slice(b * n_ch, (b + 1) * n_ch)
        st = slice(b * width, (b + 1) * width)
        per_group = [(lam_ref[0][g:g + 1, :], lam_ref[1][g:g + 1, :])
                     for g in range(b * groups, (b + 1) * groups)]
        lam_ch = [jnp.concatenate([jnp.broadcast_to(l[part], (group_ch, LANES)) for l in per_group],
                                  axis=0) for part in range(2)]
        lam_st = [jnp.concatenate([jnp.broadcast_to(l[part], (LANES, LANES)).T[:group_states, :]
                                   for l in per_group], axis=0) for part in range(2)]
        b_re, b_im = b2_ref[0][ch, :], b2_ref[1][ch, :]
        fill_b(b, 0, (lam_ch[0] * b_re - lam_ch[1] * b_im, lam_ch[0] * b_im + lam_ch[1] * b_re))
        fill_b(b, 1, (b_re, b_im))
        c_re, c_nim = ct_ref[0][st, :], ct_ref[1][st, :]
        c1 = (c_re * lam_st[0] + c_nim * lam_st[1], c_nim * lam_st[0] - c_re * lam_st[1])
        c2 = (c1[0] * lam_st[0] + c1[1] * lam_st[1], c1[1] * lam_st[0] - c1[0] * lam_st[1])
        fill_c(b, 0, c1)
        fill_c(b, 1, (c_re, c_nim))
        b_big = bq_ref[b, n_ch:2 * n_ch, :]
        k1 = jnp.dot(b_big, wg_ref[b, :, 0:n_ch], preferred_element_type=F32).astype(BF16)
        k0 = jnp.dot(b_big, wg_ref[b, :, n_ch:2 * n_ch], preferred_element_type=F32).astype(BF16)
        wu_ref[b, 0:n_ch, 0:n_ch] = k0
        wu_ref[b, 0:n_ch, n_ch:2 * n_ch] = k1
        wu_ref[b, n_ch:2 * n_ch, 0:n_ch] = jnp.zeros((n_ch, n_ch), BF16)
        wu_ref[b, n_ch:2 * n_ch, n_ch:2 * n_ch] = k0
        fill_c(b, 1, c2)


def _stage_weight_bf16(w_hbm, col0, dst_ref, stage_ref, sem_ref):
    k, n = dst_ref.shape
    rows = stage_ref.shape[1]
    n_chunks = k // rows

    def copy(c):
        return pltpu.make_async_copy(w_hbm.at[pl.ds(c * rows, rows), pl.ds(col0, n)],
                                     stage_ref.at[c % 2], sem_ref.at[c % 2])

    copy(0).start()
    for c in range(n_chunks):
        if c + 1 < n_chunks:
            copy(c + 1).start()
        copy(c).wait()
        dst_ref[c * rows:(c + 1) * rows, :] = stage_ref[c % 2].astype(BF16)


def _ssm_kernel(u_ref, sza_ref, h0re_ref, h0im_ref, lamre_ref, lamim_ref, tqre_ref, tqim_ref, d_ref,
                b2re_ref, b2im_ref, ctre_ref, ctim_ref, lgre_ref, lgim_ref, wglu_hbm, wouta_hbm,
                ya_ref, spre_ref, spim_ref, ssre_ref, ssim_ref,
                bu_ref, uslab_ref, yslab_ref, carry_ref, bq_ref, wg_ref, wu_ref,
                wglu_ref, wouta_ref, stage_narrow_ref, sem_ref,
                *, n_prompt_tiles, tiles_per_seq, sample_steps, group_ch, group_states):
    i = pl.program_id(0)
    n_bundles = bq_ref.shape[0]
    width = bq_ref.shape[2] // 2
    is_prompt = i < n_prompt_tiles
    t_in_seq = i % tiles_per_seq

    @pl.when(i == 0)
    def _():
        _stage_weight_bf16(wouta_hbm, 0, wouta_ref, bu_ref, sem_ref)
        _stage_weight_bf16(wglu_hbm, 0, wglu_ref, stage_narrow_ref, sem_ref)
        _build_folded_weights((b2re_ref, b2im_ref), (ctre_ref, ctim_ref), (lgre_ref, lgim_ref),
                              bq_ref, wg_ref, wu_ref, group_ch, group_states)

    @pl.when(jnp.logical_and(is_prompt, t_in_seq == 0))
    def _():
        carry_ref[...] = jnp.zeros_like(carry_ref)

    def run(n_groups, steps, enter_factory, leave_factory):
        folded = steps // SSM_FOLD
        d_ssm = u_ref.shape[1]
        blocks = _stepmajor_blocks(u_ref[...].astype(F32), uslab_ref, n_groups, steps)
        u_f = jnp.concatenate([jnp.concatenate(blocks[k:k + SSM_FOLD], axis=1)
                               for k in range(0, len(blocks), SSM_FOLD)], axis=0).astype(BF16)

        def u_bundle(b):
            return jnp.concatenate(
                [u_f[:, m * d_ssm + b * SSM_BUNDLE_CH:m * d_ssm + (b + 1) * SSM_BUNDLE_CH]
                 for m in range(SSM_FOLD)], axis=1)

        def b_matmul(b):
            bu_ref[b] = jnp.dot(u_bundle(b), bq_ref[b], preferred_element_type=F32)

        b_matmul(0)
        for b in range(n_bundles):
            if b + 1 < n_bundles:
                b_matmul(b + 1)
            for g in range(n_groups):
                _scan_group(bu_ref, b, g * SUBLANES * folded, folded, width, b * width,
                            (lamre_ref, lamim_ref), enter_factory(g), leave_factory(g))
            y_b = (jnp.dot(bu_ref[b].astype(BF16), wg_ref[b], preferred_element_type=F32)
                   + jnp.dot(u_bundle(b), wu_ref[b], preferred_element_type=F32))
            _store_folded(y_b, yslab_ref, b * (SSM_BUNDLE_CH // LANES), steps)
        y = _load_natural(yslab_ref, n_groups, steps) + d_ref[...] * u_ref[...].astype(F32)
        g_act = _gelu_tanh(y)
        z = jnp.dot(g_act.astype(BF16), wglu_ref[...], preferred_element_type=F32)
        o = g_act * _sigmoid(z) * sza_ref[...].astype(F32)
        ya = jnp.dot(o.astype(BF16), wouta_ref[...], preferred_element_type=F32)
        ya_ref[...] = ya.astype(BF16)

    def prompt_enter(_g):
        def enter(er, ei, tl):
            xr, xi = er, ei
            for step, shift in enumerate((1, 2, 4)):
                xr, xi = _cmuladd(tqre_ref[step, :, tl], tqim_ref[step, :, tl],
                                  pltpu.roll(xr, shift, 0), pltpu.roll(xi, shift, 0), xr, xi)
            c0r = jnp.broadcast_to(carry_ref[0:1, tl], xr.shape)
            c0i = jnp.broadcast_to(carry_ref[1:2, tl], xi.shape)
            xr, xi = _cmuladd(tqre_ref[3, :, tl], tqim_ref[3, :, tl], c0r, c0i, xr, xi)
            carry_ref[0:1, tl] = xr[SUBLANES - 1:SUBLANES, :]
            carry_ref[1:2, tl] = xi[SUBLANES - 1:SUBLANES, :]
            first = lax.broadcasted_iota(jnp.int32, xr.shape, 0) == 0
            return (jnp.where(first, c0r, pltpu.roll(xr, 1, 0)),
                    jnp.where(first, c0i, pltpu.roll(xi, 1, 0)))
        return enter

    def prompt_leave(_g):
        return lambda hr, hi, tl: None

    def sample_enter(g):
        rows = slice(g * SUBLANES, (g + 1) * SUBLANES)
        return lambda er, ei, tl: (h0re_ref[rows, tl], h0im_ref[rows, tl])

    def sample_leave(g):
        rows = slice(g * SUBLANES, (g + 1) * SUBLANES)

        def leave(hr, hi, tl):
            ssre_ref[rows, tl] = hr
            ssim_ref[rows, tl] = hi
        return leave

    @pl.when(is_prompt)
    def _():
        run(1, TILE_M // SUBLANES, prompt_enter, prompt_leave)

    @pl.when(jnp.logical_not(is_prompt))
    def _():
        run(TILE_M // (SUBLANES * sample_steps), sample_steps, sample_enter, sample_leave)

    @pl.when(jnp.logical_and(is_prompt, t_in_seq == tiles_per_seq - 1))
    def _():
        row = pl.ds(i // tiles_per_seq, 1)
        spre_ref[row, :] = carry_ref[0:1, :]
        spim_ref[row, :] = carry_ref[1:2, :]


def _ssm_call(act, h0re, h0im, lam8, tabq, d, b2, ct, lamg, w_glu, w_out_a, n_prompt_rows,
              seq_len, sample_steps, group_ch, group_states):
    m = act.shape[0]
    d_ssm, d_model = w_out_a.shape
    n_state = lam8[0].shape[1]
    n_tiles = m // TILE_M
    n_p = n_prompt_rows // TILE_M
    n_batch = n_prompt_rows // seq_len
    seqs_per_tile = TILE_M // sample_steps
    n_sample_seq = h0re.shape[0]
    slab_rows = SUBLANES * _pitch(TILE_M // SUBLANES)
    n_bundles = d_ssm // SSM_BUNDLE_CH
    width = n_state // n_bundles
    assert 2 * width == d_model and n_bundles >= 2 and d_ssm % TILE_M == 0
    assert sample_steps % SSM_FOLD == 0
    kernel = functools.partial(_ssm_kernel, n_prompt_tiles=n_p,
                               tiles_per_seq=seq_len // TILE_M, sample_steps=sample_steps,
                               group_ch=group_ch, group_states=group_states)
    tile_map = lambda i: (i, 0)
    sample_map = lambda i: (jnp.maximum(i - n_p, 0), 0)
    hbm = pl.BlockSpec(memory_space=pl.ANY)
    return pl.pallas_call(
        kernel,
        grid=(n_tiles,),
        in_specs=[
            pl.BlockSpec((TILE_M, d_ssm), tile_map),
            pl.BlockSpec((TILE_M, d_ssm), lambda i: (i, 1)),
            pl.BlockSpec((seqs_per_tile, n_state), sample_map),
            pl.BlockSpec((seqs_per_tile, n_state), sample_map),
            _const_spec(lam8[0].shape), _const_spec(lam8[1].shape),
            _const_spec(tabq[0].shape), _const_spec(tabq[1].shape),
            _const_spec(d.shape),
            _const_spec(b2[0].shape), _const_spec(b2[1].shape),
            _const_spec(ct[0].shape), _const_spec(ct[1].shape),
            _const_spec(lamg[0].shape), _const_spec(lamg[1].shape),
            hbm, hbm,
        ],
        out_specs=[
            pl.BlockSpec((TILE_M, d_model), tile_map),
            pl.BlockSpec((n_batch, n_state), lambda i: (0, 0)),
            pl.BlockSpec((n_batch, n_state), lambda i: (0, 0)),
            pl.BlockSpec((seqs_per_tile, n_state), sample_map),
            pl.BlockSpec((seqs_per_tile, n_state), sample_map),
        ],
        out_shape=[
            jax.ShapeDtypeStruct((m, d_model), BF16),
            jax.ShapeDtypeStruct((n_batch, n_state), F32),
            jax.ShapeDtypeStruct((n_batch, n_state), F32),
            jax.ShapeDtypeStruct((n_sample_seq, n_state), F32),
            jax.ShapeDtypeStruct((n_sample_seq, n_state), F32),
        ],
        scratch_shapes=[
            pltpu.VMEM((n_bundles, TILE_M // SSM_FOLD, 2 * width), F32),
            pltpu.VMEM((d_ssm // LANES, slab_rows, LANES), F32),
            pltpu.VMEM((d_ssm // LANES, slab_rows, LANES), F32),
            pltpu.VMEM((2, n_state), F32),
            pltpu.VMEM((n_bundles, SSM_FOLD * SSM_BUNDLE_CH, 2 * width), BF16),
            pltpu.VMEM((n_bundles, 2 * width, SSM_FOLD * SSM_BUNDLE_CH), BF16),
            pltpu.VMEM((n_bundles, SSM_FOLD * SSM_BUNDLE_CH, SSM_FOLD * SSM_BUNDLE_CH), BF16),
            pltpu.VMEM((d_ssm, d_ssm), BF16),
            pltpu.VMEM((d_ssm, d_model), BF16),
            pltpu.VMEM((2, STAGE_ROWS // 2, d_ssm), F32),
            pltpu.SemaphoreType.DMA((2,)),
        ],
        compiler_params=pltpu.CompilerParams(
            dimension_semantics=("arbitrary",), vmem_limit_bytes=VMEM_LIMIT_BYTES),
        name="ssm_branch",
    )(act, act, h0re, h0im, *lam8, *tabq, d, *b2, *ct, *lamg, w_glu, w_out_a)


def _tail_kernel(q_ref, bz_ref, ya_ref, sga_ref, sgc_ref, xp_ref, xs_ref, hist_ref,
                 cw_ref, lng_ref, lnb_ref, woutc_hbm, wo_hbm,
                 yp_ref, ys_ref,
                 pad_ref, conv_ref, yc_ref, woutc_ref, wo_ref, stage_ref, sem_ref,
                 *, n_prompt_tiles, tiles_per_seq, alpha):
    i = pl.program_id(0)
    is_prompt = i < n_prompt_tiles

    @pl.when(i == 0)
    def _():
        _stage_weight_bf16(woutc_hbm, 0, woutc_ref, stage_ref, sem_ref)
        _stage_weight_bf16(wo_hbm, 0, wo_ref, stage_ref, sem_ref)

    @pl.when(jnp.logical_and(is_prompt, i % tiles_per_seq == 0))
    def _():
        pad_ref[0:CONV_PAD, :] = jnp.zeros((CONV_PAD, pad_ref.shape[1]), F32)

    q = q_ref[...]
    pad_ref[CONV_PAD:CONV_PAD + TILE_M, :] = q
    w0 = cw_ref[0:1, :]
    w1 = cw_ref[1:2, :]
    w2 = cw_ref[2:3, :]

    def finish(q1, q2, x_ref, y_ref):
        conv_ref[...] = w0 * q2 + w1 * q1 + w2 * q
        yc_in = conv_ref[...].astype(BF16) * bz_ref[...]
        yc_ref[...] = jnp.dot(yc_in, woutc_ref[...], preferred_element_type=F32)
        merged = sga_ref[...] * ya_ref[...] + sgc_ref[...] * yc_ref[...].astype(BF16)
        out = jnp.dot(merged, wo_ref[...], preferred_element_type=F32)
        r = alpha * x_ref[...] + out
        mu = jnp.mean(r, axis=-1, keepdims=True)
        rc = r - mu
        var = jnp.mean(rc * rc, axis=-1, keepdims=True)
        y_ref[...] = rc * lax.rsqrt(var + LN_EPS) * lng_ref[...] + lnb_ref[...]

    @pl.when(is_prompt)
    def _():
        q1 = pad_ref[CONV_PAD - 1:CONV_PAD - 1 + TILE_M, :]
        q2 = pad_ref[CONV_PAD - 2:CONV_PAD - 2 + TILE_M, :]
        finish(q1, q2, xp_ref, yp_ref)
        pad_ref[0:CONV_PAD, :] = pad_ref[TILE_M:TILE_M + CONV_PAD, :]

    @pl.when(jnp.logical_not(is_prompt))
    def _():
        t = lax.broadcasted_iota(jnp.int32, q.shape, 0) % SUBLANES
        d_conv = q.shape[1]

        def history(j):
            return jnp.concatenate(
                [jnp.broadcast_to(hist_ref[s:s + 1, j * d_conv:(j + 1) * d_conv],
                                  (SUBLANES, d_conv)) for s in range(TILE_M // SUBLANES)], axis=0)

        e0 = history(0)
        e1 = history(1)
        q1 = jnp.where(t == 0, e1, pad_ref[CONV_PAD - 1:CONV_PAD - 1 + TILE_M, :])
        q2 = jnp.where(t == 0, e0,
                       jnp.where(t == 1, e1, pad_ref[CONV_PAD - 2:CONV_PAD - 2 + TILE_M, :]))
        finish(q1, q2, xs_ref, ys_ref)


def _tail_call(q, act, ya, xp, xs, hist, conv_w, w_out_c, w_o, ln_g, ln_b, seq_len, alpha):
    m, d_conv = q.shape
    m_p, d_model = xp.shape
    m_s = xs.shape[0]
    n_p = m_p // TILE_M
    kernel = functools.partial(_tail_kernel, n_prompt_tiles=n_p,
                               tiles_per_seq=seq_len // TILE_M, alpha=alpha)
    prompt_map = lambda i: (jnp.minimum(i, n_p - 1), 0)
    sample_map = lambda i: (jnp.maximum(i - n_p, 0), 0)
    row_map = lambda i: (i, 0)
    hbm = pl.BlockSpec(memory_space=pl.ANY)
    assert w_out_c.shape[1] == w_o.shape[1] == d_model
    return pl.pallas_call(
        kernel,
        grid=(m // TILE_M,),
        in_specs=[
            pl.BlockSpec((TILE_M, d_conv), row_map),
            pl.BlockSpec((TILE_M, d_conv), lambda i: (i, 2)),
            pl.BlockSpec((TILE_M, d_model), row_map),
            pl.BlockSpec((TILE_M, d_model), lambda i: (i, 2)),
            pl.BlockSpec((TILE_M, d_model), lambda i: (i, 3)),
            pl.BlockSpec((TILE_M, d_model), prompt_map),
            pl.BlockSpec((TILE_M, d_model), sample_map),
            pl.BlockSpec((TILE_M // SUBLANES, hist.shape[1]), sample_map),
            _const_spec(conv_w.shape),
            _const_spec(ln_g.shape),
            _const_spec(ln_b.shape),
            hbm, hbm,
        ],
        out_specs=[
            pl.BlockSpec((TILE_M, d_model), prompt_map),
            pl.BlockSpec((TILE_M, d_model), sample_map),
        ],
        out_shape=[
            jax.ShapeDtypeStruct((m_p, d_model), F32),
            jax.ShapeDtypeStruct((m_s, d_model), F32),
        ],
        scratch_shapes=[
            pltpu.VMEM((TILE_M + CONV_PAD, d_conv), F32),
            pltpu.VMEM((TILE_M, d_conv), F32),
            pltpu.VMEM((TILE_M, d_model), F32),
            pltpu.VMEM(w_out_c.shape, BF16),
            pltpu.VMEM(w_o.shape, BF16),
            pltpu.VMEM((2, STAGE_ROWS, d_model), F32),
            pltpu.SemaphoreType.DMA((2,)),
        ],
        compiler_params=pltpu.CompilerParams(
            dimension_semantics=("arbitrary",), vmem_limit_bytes=VMEM_LIMIT_BYTES),
        name="tail",
    )(q, act, ya, act, act, xp, xs, hist, conv_w, ln_g, ln_b, w_out_c, w_o)


def _ssm_params(a_re, a_im, log_dt, b_re, b_im, c_re, c_im, prompt_steps):
    g, p, gc = b_re.shape
    dt = jnp.exp(log_dt)[:, None]
    mag = jnp.exp(a_re * dt)
    ang = a_im * dt
    lam_re = mag * jnp.cos(ang)
    lam_im = mag * jnp.sin(ang)
    den = a_re * a_re + a_im * a_im
    q_re = ((lam_re - 1.0) * a_re + lam_im * a_im) / den
    q_im = (lam_im * a_re - (lam_re - 1.0) * a_im) / den
    bb_re = q_re[..., None] * b_re - q_im[..., None] * b_im
    bb_im = q_re[..., None] * b_im + q_im[..., None] * b_re

    def channel_rows(bb):
        rows = bb.transpose(0, 2, 1).reshape(g * gc, p)
        return jnp.tile(rows, (1, LANES // p))

    def state_rows(cc):
        rows = cc.transpose(0, 2, 1).reshape(g * p, gc)
        return jnp.tile(rows, (1, LANES // gc))

    b2 = (channel_rows(bb_re), channel_rows(bb_im))
    ct = (state_rows(c_re), -state_rows(c_im))
    lamg = (jnp.tile(lam_re, (1, LANES // p)), jnp.tile(lam_im, (1, LANES // p)))

    lam2_re = lam_re * lam_re - lam_im * lam_im
    lam2_im = 2.0 * lam_re * lam_im
    lam8 = (jnp.broadcast_to(lam2_re.reshape(1, -1), (SUBLANES, g * p)),
            jnp.broadcast_to(lam2_im.reshape(1, -1), (SUBLANES, g * p)))

    row = jnp.arange(SUBLANES, dtype=F32)[None, :, None]
    shift = jnp.array([1.0, 2.0, 4.0], F32)[:, None, None]
    exponent = jnp.concatenate([jnp.broadcast_to(shift, (3, SUBLANES, 1)), row + 1.0]) * prompt_steps
    keep = jnp.concatenate([row >= shift, jnp.ones((1, SUBLANES, 1), bool)])
    mag_e = jnp.where(keep, jnp.exp(exponent * (a_re * dt).reshape(1, 1, -1)), 0.0)
    ang_e = exponent * ang.reshape(1, 1, -1)
    tabq = (mag_e * jnp.cos(ang_e), mag_e * jnp.sin(ang_e))
    return b2, ct, lamg, lam8, tabq


def kernel(x_prompt, x_sample, state_ssm_re, state_ssm_im, state_conv, w_in, ssm_a_re, ssm_a_im, ssm_log_dt, ssm_b_re, ssm_b_im, ssm_c_re, ssm_c_im, ssm_d, w_glu, w_out_a, conv_w, w_out_c, w_o, ln_g, ln_b):
    depth = w_in.shape[0]
    assert depth == 1, "single-layer trunk"
    batch, seq, d_model = x_prompt.shape
    dec_batch, dec_seq, _ = x_sample.shape
    assert dec_seq == SUBLANES and seq % TILE_M == 0 and (dec_batch * dec_seq) % TILE_M == 0
    g, p, gc = ssm_b_re.shape[1:]
    d_ssm = g * gc
    d_conv = conv_w.shape[2]
    n_state = g * p
    alpha = (2 * depth) ** 0.25

    xp = x_prompt.reshape(batch * seq, d_model)
    xs = x_sample.reshape(dec_batch * dec_seq, d_model)
    m_p = xp.shape[0]
    w = w_in[0]

    assert d_ssm == d_conv == PROJ_COLS and d_model == 2 * PROJ_COLS
    act, q = _proj_call(xp, xs, w)

    b2, ct, lamg, lam8, tabq = _ssm_params(
        ssm_a_re[0], ssm_a_im[0], ssm_log_dt[0], ssm_b_re[0], ssm_b_im[0], ssm_c_re[0],
        ssm_c_im[0], TILE_M // SUBLANES)
    h0re = state_ssm_re[0].reshape(dec_batch, n_state)
    h0im = state_ssm_im[0].reshape(dec_batch, n_state)
    ya, spre, spim, ssre, ssim = _ssm_call(
        act, h0re, h0im, lam8, tabq, ssm_d[0][None, :], b2, ct, lamg, w_glu[0], w_out_a[0],
        m_p, seq, dec_seq, gc, p)

    hist = state_conv[0].reshape(dec_batch, (state_conv.shape[2]) * d_conv)
    yp, ys = _tail_call(q, act, ya, xp, xs, hist, conv_w[0], w_out_c[0], w_o[0],
                        ln_g[0][None, :], ln_b[0][None, :], seq, alpha)

    q8 = q.reshape(-1, SUBLANES, d_conv)
    q_p = q8[seq // SUBLANES - 1:m_p // SUBLANES:seq // SUBLANES, SUBLANES - 2:, :]
    q_s = q8[m_p // SUBLANES:, dec_seq - 2:, :]
    return (yp.reshape(batch, seq, d_model),
            ys.reshape(dec_batch, dec_seq, d_model),
            spre.reshape(1, batch, g, p),
            spim.reshape(1, batch, g, p),
            q_p[None],
            ssre.reshape(1, dec_batch, g, p),
            ssim.reshape(1, dec_batch, g, p),
            q_s[None])
```

```python
import functools
import math

import jax
import jax.numpy as jnp
from jax import lax
from jax.experimental import pallas as pl
from jax.experimental.pallas import tpu as pltpu

F32 = jnp.float32
BF16 = jnp.bfloat16

SUBLANES = 8
LANES = 128
VMEM_LIMIT_BYTES = 56 * 1024 * 1024

TILE_M = 256
PROJ_TILE_M = 512
SSM_FOLD = 2
SSM_BUNDLE_CH = 256
SCAN_CHUNK = 4 * LANES
CONV_PAD = SUBLANES
STAGE_ROWS = 512
NEXT_STAGE_ROWS = 128

LN_EPS = 1e-5
GELU_C = math.sqrt(2.0 / math.pi)


def _sigmoid(x):
    return 0.5 * jnp.tanh(0.5 * x) + 0.5


def _silu(x):
    return x * _sigmoid(x)


def _gelu_tanh(x):
    return 0.5 * x * (1.0 + jnp.tanh(GELU_C * (x + 0.044715 * (x * x * x))))


def _const_spec(shape):
    nd = len(shape)
    return pl.BlockSpec(shape, lambda i: (0,) * nd, pipeline_mode=pl.Buffered(1))


PROJ_GROUPS = ((0, 1), (2, 5), (3, 4), (6, 7), (8, 9))
PROJ_COLS = 1024
Q_GROUP = 2


def _epi_ssm_in(a, b, act_ref, q_ref):
    act_ref[:, :PROJ_COLS] = a.astype(BF16)
    act_ref[:, PROJ_COLS:] = _silu(b).astype(BF16)


def _epi_conv_gate(a, b, act_ref, q_ref):
    act_ref[:, :PROJ_COLS] = (a * _silu(b)).astype(BF16)


def _epi_conv_in(a, b, act_ref, q_ref):
    q_ref[...] = a * b


def _epi_merge_gate(a, b, act_ref, q_ref):
    act_ref[:, :PROJ_COLS] = _sigmoid(a).astype(BF16)
    act_ref[:, PROJ_COLS:] = _sigmoid(b).astype(BF16)


PROJ_EPILOGUES = (_epi_ssm_in, _epi_conv_gate, _epi_conv_in, _epi_merge_gate, _epi_merge_gate)


def _proj_kernel(xp_ref, xs_ref, w_hbm, act_ref, q_ref, wbf_ref, stage_ref, next_ref, sem_ref,
                 *, n_prompt_tiles):
    g = pl.program_id(0)
    i = pl.program_id(1)
    next_rows = next_ref.shape[1]
    n_next = xp_ref.shape[1] // next_rows

    @pl.when(jnp.logical_and(g == 0, i == 0))
    def _():
        for blk, col_block in enumerate(PROJ_GROUPS[0]):
            _stage_weight_bf16(w_hbm, col_block * PROJ_COLS, wbf_ref.at[0, blk], stage_ref, sem_ref)

    rows = pl.ds(pl.multiple_of(i * next_rows, next_rows), next_rows)

    def next_copy(gi, blk):
        col0 = PROJ_GROUPS[gi + 1][blk] * PROJ_COLS
        return pltpu.make_async_copy(w_hbm.at[rows, pl.ds(col0, PROJ_COLS)], next_ref.at[blk],
                                     sem_ref.at[blk])

    for gi, epilogue in enumerate(PROJ_EPILOGUES):
        @pl.when(g == gi)
        def _():
            has_next = gi + 1 < len(PROJ_GROUPS)
            if has_next:
                @pl.when(i < n_next)
                def _():
                    for blk in range(2):
                        next_copy(gi, blk).start()

            slot = gi % 2

            def compute(x_ref):
                if epilogue is _epi_conv_gate:
                    act_ref[:, PROJ_COLS:] = jnp.zeros((act_ref.shape[0], PROJ_COLS), BF16)
                xb = x_ref[...].astype(BF16)
                a = jnp.dot(xb, wbf_ref[slot, 0], preferred_element_type=F32)
                b = jnp.dot(xb, wbf_ref[slot, 1], preferred_element_type=F32)
                epilogue(a, b, act_ref, q_ref)

            @pl.when(i < n_prompt_tiles)
            def _():
                compute(xp_ref)

            @pl.when(i >= n_prompt_tiles)
            def _():
                compute(xs_ref)

            if has_next:
                @pl.when(i < n_next)
                def _():
                    for blk in range(2):
                        next_copy(gi, blk).wait()
                        wbf_ref[1 - slot, blk, rows, :] = next_ref[blk].astype(BF16)


def _proj_call(xp, xs, w_in):
    m_p, d = xp.shape
    m_s = xs.shape[0]
    tile = PROJ_TILE_M
    n_p, n_s = m_p // tile, m_s // tile
    n_tiles = n_p + n_s
    n_groups = len(PROJ_GROUPS)
    act_block = 2 * PROJ_COLS
    assert m_p % tile == 0 and m_s % tile == 0
    assert n_tiles * NEXT_STAGE_ROWS >= d and d % NEXT_STAGE_ROWS == 0
    kernel = functools.partial(_proj_kernel, n_prompt_tiles=n_p)

    def act_map(g, i):
        col = jnp.where(g < Q_GROUP, g, jnp.where(g == Q_GROUP, Q_GROUP - 1, g - 1))
        return jnp.where(g == Q_GROUP, n_tiles - 1, i), col

    def q_map(g, i):
        return jnp.where(g < Q_GROUP, 0, jnp.where(g == Q_GROUP, i, n_tiles - 1)), 0

    return pl.pallas_call(
        kernel,
        grid=(n_groups, n_tiles),
        in_specs=[
            pl.BlockSpec((tile, d), lambda g, i: (jnp.minimum(i, n_p - 1), 0)),
            pl.BlockSpec((tile, d), lambda g, i: (jnp.clip(i - n_p, 0, n_s - 1), 0)),
            pl.BlockSpec(memory_space=pl.ANY),
        ],
        out_specs=[pl.BlockSpec((tile, act_block), act_map),
                   pl.BlockSpec((tile, PROJ_COLS), q_map)],
        out_shape=[jax.ShapeDtypeStruct((m_p + m_s, (n_groups - 1) * act_block), BF16),
                   jax.ShapeDtypeStruct((m_p + m_s, PROJ_COLS), F32)],
        scratch_shapes=[
            pltpu.VMEM((2, 2, d, PROJ_COLS), BF16),
            pltpu.VMEM((2, STAGE_ROWS, PROJ_COLS), F32),
            pltpu.VMEM((2, NEXT_STAGE_ROWS, PROJ_COLS), F32),
            pltpu.SemaphoreType.DMA((2,)),
        ],
        compiler_params=pltpu.CompilerParams(
            dimension_semantics=("arbitrary", "arbitrary"), vmem_limit_bytes=VMEM_LIMIT_BYTES),
        name="proj",
    )(xp, xs, w_in)


def _pitch(steps):
    return steps if (steps // SUBLANES) % 2 == 1 else steps + SUBLANES


def _stepmajor_blocks(val, slab_ref, n_groups, steps):
    pitch = _pitch(steps)
    n_slabs = val.shape[1] // LANES
    n_sub = n_groups * SUBLANES
    for j in range(n_slabs):
        lanes = slice(j * LANES, (j + 1) * LANES)
        if pitch == steps:
            slab_ref[j, 0:n_sub * steps, :] = val[:, lanes]
        else:
            for s in range(n_sub):
                slab_ref[j, s * pitch:s * pitch + steps, :] = val[s * steps:(s + 1) * steps, lanes]
    blocks = []
    for g in range(n_groups):
        for k in range(steps):
            rows = pl.ds(g * SUBLANES * pitch + k, SUBLANES, stride=pitch)
            blocks.append(jnp.concatenate([slab_ref[j, rows, :] for j in range(n_slabs)], axis=1))
    return blocks


def _store_folded(val, slab_ref, slab0, steps):
    pitch = _pitch(steps)
    n = val.shape[1] // SSM_FOLD
    for i in range(val.shape[0] // SUBLANES):
        g, kf = divmod(i, steps // SSM_FOLD)
        for m in range(SSM_FOLD):
            rows = pl.ds(g * SUBLANES * pitch + SSM_FOLD * kf + m, SUBLANES, stride=pitch)
            for j in range(n // LANES):
                slab_ref[slab0 + j, rows, :] = val[i * SUBLANES:(i + 1) * SUBLANES,
                                                   m * n + j * LANES:m * n + (j + 1) * LANES]


def _load_natural(slab_ref, n_groups, steps):
    pitch = _pitch(steps)
    n_slabs = slab_ref.shape[0]
    n_sub = n_groups * SUBLANES
    if pitch == steps:
        return jnp.concatenate([slab_ref[j, 0:n_sub * steps, :] for j in range(n_slabs)], axis=1)
    return jnp.concatenate(
        [jnp.concatenate([slab_ref[j, s * pitch:s * pitch + steps, :] for j in range(n_slabs)],
                         axis=1) for s in range(n_sub)], axis=0)


def _cmuladd(ar, ai, br, bi, cr, ci):
    return ar * br - ai * bi + cr, ar * bi + ai * br + ci


def _scan_group(bu_ref, b, row0, steps, width, lane0, lam_ref, enter_fn, leave_fn):
    for c in range(width // SCAN_CHUNK):
        re = slice(c * SCAN_CHUNK, (c + 1) * SCAN_CHUNK)
        im = slice(width + c * SCAN_CHUNK, width + (c + 1) * SCAN_CHUNK)
        tl = slice(lane0 + c * SCAN_CHUNK, lane0 + (c + 1) * SCAN_CHUNK)
        lr, li = lam_ref[0][:, tl], lam_ref[1][:, tl]

        def x_block(k):
            rows = slice(row0 + k * SUBLANES, row0 + (k + 1) * SUBLANES)
            return rows, bu_ref[b, rows, re], bu_ref[b, rows, im]

        _, pr, pi = x_block(0)
        for k in range(1, steps):
            _, xr, xi = x_block(k)
            pr, pi = _cmuladd(lr, li, pr, pi, xr, xi)
        hr, hi = enter_fn(pr, pi, tl)
        for k in range(steps):
            rows, xr, xi = x_block(k)
            bu_ref[b, rows, re] = hr
            bu_ref[b, rows, im] = hi
            hr, hi = _cmuladd(lr, li, hr, hi, xr, xi)
        leave_fn(hr, hi, tl)


def _build_folded_weights(b2_ref, ct_ref, lam_ref, bq_ref, wg_ref, wu_ref, group_ch, group_states):
    n_bundles = bq_ref.shape[0]
    n_ch = bq_ref.shape[1] // SSM_FOLD
    width = bq_ref.shape[2] // 2
    ch_shift = group_ch.bit_length() - 1
    st_shift = group_states.bit_length() - 1
    assert group_ch == 1 << ch_shift and group_states == 1 << st_shift
    groups = n_ch // group_ch

    def group_of(n_rows, axis, shift):
        return lax.shift_right_logical(lax.broadcasted_iota(jnp.int32, (n_rows, LANES), axis), shift)

    ch_rows = group_of(n_ch, 0, ch_shift)
    st_rows = group_of(width, 0, st_shift)
    st_lanes = group_of(n_ch, 1, st_shift)
    ch_lanes = group_of(width, 1, ch_shift)

    def fill_b(b, pos, src_pair):
        for part in range(2):
            for j in range(width // LANES):
                keep = ch_rows == st_lanes + j * (LANES // group_states)
                lanes = slice(part * width + j * LANES, part * width + (j + 1) * LANES)
                bq_ref[b, pos * n_ch:(pos + 1) * n_ch, lanes] = (
                    jnp.where(keep, src_pair[part], 0.0).astype(BF16))

    def fill_c(b, pos, src_pair):
        for part in range(2):
            for j in range(n_ch // LANES):
                keep = st_rows == ch_lanes + j * (LANES // group_ch)
                lanes = slice(pos * n_ch + j * LANES, pos * n_ch + (j + 1) * LANES)
                wg_ref[b, part * width:(part + 1) * width, lanes] = (
                    jnp.where(keep, src_pair[part], 0.0).astype(BF16))

    for b in range(n_bundles):
        ch = slice(b * n_ch, (b + 1) * n_ch)
        st = slice(b * width, (b + 1) * width)
        per_group = [(lam_ref[0][g:g + 1, :], lam_ref[1][g:g + 1, :])
                     for g in range(b * groups, (b + 1) * groups)]
        lam_ch = [jnp.concatenate([jnp.broadcast_to(l[part], (group_ch, LANES)) for l in per_group],
                                  axis=0) for part in range(2)]
        lam_st = [jnp.concatenate([jnp.broadcast_to(l[part], (LANES, LANES)).T[:group_states, :]
                                   for l in per_group], axis=0) for part in range(2)]
        b_re, b_im = b2_ref[0][ch, :], b2_ref[1][ch, :]
        fill_b(b, 0, (lam_ch[0] * b_re - lam_ch[1] * b_im, lam_ch[0] * b_im + lam_ch[1] * b_re))
        fill_b(b, 1, (b_re, b_im))
        c_re, c_nim = ct_ref[0][st, :], ct_ref[1][st, :]
        c1 = (c_re * lam_st[0] + c_nim * lam_st[1], c_nim * lam_st[0] - c_re * lam_st[1])
        c2 = (c1[0] * lam_st[0] + c1[1] * lam_st[1], c1[1] * lam_st[0] - c1[0] * lam_st[1])
        fill_c(b, 0, c1)
        fill_c(b, 1, (c_re, c_nim))
        b_big = bq_ref[b, n_ch:2 * n_ch, :]
        k1 = jnp.dot(b_big, wg_ref[b, :, 0:n_ch], preferred_element_type=F32).astype(BF16)
        k0 = jnp.dot(b_big, wg_ref[b, :, n_ch:2 * n_ch], preferred_element_type=F32).astype(BF16)
        wu_ref[b, 0:n_ch, 0:n_ch] = k0
        wu_ref[b, 0:n_ch, n_ch:2 * n_ch] = k1
        wu_ref[b, n_ch:2 * n_ch, 0:n_ch] = jnp.zeros((n_ch, n_ch), BF16)
        wu_ref[b, n_ch:2 * n_ch, n_ch:2 * n_ch] = k0
        fill_c(b, 1, c2)


def _stage_weight_bf16(w_hbm, col0, dst_ref, stage_ref, sem_ref):
    k, n = dst_ref.shape
    rows = stage_ref.shape[1]
    n_chunks = k // rows

    def copy(c):
        return pltpu.make_async_copy(w_hbm.at[pl.ds(c * rows, rows), pl.ds(col0, n)],
                                     stage_ref.at[c % 2], sem_ref.at[c % 2])

    copy(0).start()
    for c in range(n_chunks):
        if c + 1 < n_chunks:
            copy(c + 1).start()
        copy(c).wait()
        dst_ref[c * rows:(c + 1) * rows, :] = stage_ref[c % 2].astype(BF16)


def _ssm_kernel(u_ref, sza_ref, h0re_ref, h0im_ref, lamre_ref, lamim_ref, tqre_ref, tqim_ref, d_ref,
                b2re_ref, b2im_ref, ctre_ref, ctim_ref, lgre_ref, lgim_ref, wglu_hbm, wouta_hbm,
                ya_ref, spre_ref, spim_ref, ssre_ref, ssim_ref,
                bu_ref, uslab_ref, yslab_ref, carry_ref, bq_ref, wg_ref, wu_ref,
                wglu_ref, wouta_ref, stage_narrow_ref, sem_ref,
                *, n_prompt_tiles, tiles_per_seq, sample_steps, group_ch, group_states):
    i = pl.program_id(0)
    n_bundles = bq_ref.shape[0]
    width = bq_ref.shape[2] // 2
    is_prompt = i < n_prompt_tiles
    t_in_seq = i % tiles_per_seq

    @pl.when(i == 0)
    def _():
        _stage_weight_bf16(wouta_hbm, 0, wouta_ref, bu_ref, sem_ref)
        _stage_weight_bf16(wglu_hbm, 0, wglu_ref, stage_narrow_ref, sem_ref)
        _build_folded_weights((b2re_ref, b2im_ref), (ctre_ref, ctim_ref), (lgre_ref, lgim_ref),
                              bq_ref, wg_ref, wu_ref, group_ch, group_states)

    @pl.when(jnp.logical_and(is_prompt, t_in_seq == 0))
    def _():
        carry_ref[...] = jnp.zeros_like(carry_ref)

    def run(n_groups, steps, enter_factory, leave_factory):
        folded = steps // SSM_FOLD
        d_ssm = u_ref.shape[1]
        blocks = _stepmajor_blocks(u_ref[...].astype(F32), uslab_ref, n_groups, steps)
        u_f = jnp.concatenate([jnp.concatenate(blocks[k:k + SSM_FOLD], axis=1)
                               for k in range(0, len(blocks), SSM_FOLD)], axis=0).astype(BF16)

        def u_bundle(b):
            return jnp.concatenate(
                [u_f[:, m * d_ssm + b * SSM_BUNDLE_CH:m * d_ssm + (b + 1) * SSM_BUNDLE_CH]
                 for m in range(SSM_FOLD)], axis=1)

        def b_matmul(b):
            bu_ref[b] = jnp.dot(u_bundle(b), bq_ref[b], preferred_element_type=F32)

        b_matmul(0)
        for b in range(n_bundles):
            if b + 1 < n_bundles:
                b_matmul(b + 1)
            for g in range(n_groups):
                _scan_group(bu_ref, b, g * SUBLANES * folded, folded, width, b * width,
                            (lamre_ref, lamim_ref), enter_factory(g), leave_factory(g))
            y_b = (jnp.dot(bu_ref[b].astype(BF16), wg_ref[b], preferred_element_type=F32)
                   + jnp.dot(u_bundle(b), wu_ref[b], preferred_element_type=F32))
            _store_folded(y_b, yslab_ref, b * (SSM_BUNDLE_CH // LANES), steps)
        y = _load_natural(yslab_ref, n_groups, steps) + d_ref[...] * u_ref[...].astype(F32)
        g_act = _gelu_tanh(y)
        z = jnp.dot(g_act.astype(BF16), wglu_ref[...], preferred_element_type=F32)
        o = g_act * _sigmoid(z) * sza_ref[...].astype(F32)
        ya = jnp.dot(o.astype(BF16), wouta_ref[...], preferred_element_type=F32)
        ya_ref[...] = ya.astype(BF16)

    def prompt_enter(_g):
        def enter(er, ei, tl):
            xr, xi = er, ei
            for step, shift in enumerate((1, 2, 4)):
                xr, xi = _cmuladd(tqre_ref[step, :, tl], tqim_ref[step, :, tl],
                                  pltpu.roll(xr, shift, 0), pltpu.roll(xi, shift, 0), xr, xi)
            c0r = jnp.broadcast_to(carry_ref[0:1, tl], xr.shape)
            c0i = jnp.broadcast_to(carry_ref[1:2, tl], xi.shape)
            xr, xi = _cmuladd(tqre_ref[3, :, tl], tqim_ref[3, :, tl], c0r, c0i, xr, xi)
            carry_ref[0:1, tl] = xr[SUBLANES - 1:SUBLANES, :]
            carry_ref[1:2, tl] = xi[SUBLANES - 1:SUBLANES, :]
            first = lax.broadcasted_iota(jnp.int32, xr.shape, 0) == 0
            return (jnp.where(first, c0r, pltpu.roll(xr, 1, 0)),
                    jnp.where(first, c0i, pltpu.roll(xi, 1, 0)))
        return enter

    def prompt_leave(_g):
        return lambda hr, hi, tl: None

    def sample_enter(g):
        rows = slice(g * SUBLANES, (g + 1) * SUBLANES)
        return lambda er, ei, tl: (h0re_ref[rows, tl], h0im_ref[rows, tl])

    def sample_leave(g):
        rows = slice(g * SUBLANES, (g + 1) * SUBLANES)

        def leave(hr, hi, tl):
            ssre_ref[rows, tl] = hr
            ssim_ref[rows, tl] = hi
        return leave

    @pl.when(is_prompt)
    def _():
        run(1, TILE_M // SUBLANES, prompt_enter, prompt_leave)

    @pl.when(jnp.logical_not(is_prompt))
    def _():
        run(TILE_M // (SUBLANES * sample_steps), sample_steps, sample_enter, sample_leave)

    @pl.when(jnp.logical_and(is_prompt, t_in_seq == tiles_per_seq - 1))
    def _():
        row = pl.ds(i // tiles_per_seq, 1)
        spre_ref[row, :] = carry_ref[0:1, :]
        spim_ref[row, :] = carry_ref[1:2, :]


def _ssm_call(act, h0re, h0im, lam8, tabq, d, b2, ct, lamg, w_glu, w_out_a, n_prompt_rows,
              seq_len, sample_steps, group_ch, group_states):
    m = act.shape[0]
    d_ssm, d_model = w_out_a.shape
    n_state = lam8[0].shape[1]
    n_tiles = m // TILE_M
    n_p = n_prompt_rows // TILE_M
    n_batch = n_prompt_rows // seq_len
    seqs_per_tile = TILE_M // sample_steps
    n_sample_seq = h0re.shape[0]
    slab_rows = SUBLANES * _pitch(TILE_M // SUBLANES)
    n_bundles = d_ssm // SSM_BUNDLE_CH
    width = n_state // n_bundles
    assert 2 * width == d_model and n_bundles >= 2 and d_ssm % TILE_M == 0
    assert sample_steps % SSM_FOLD == 0
    kernel = functools.partial(_ssm_kernel, n_prompt_tiles=n_p,
                               tiles_per_seq=seq_len // TILE_M, sample_steps=sample_steps,
                               group_ch=group_ch, group_states=group_states)
    tile_map = lambda i: (i, 0)
    sample_map = lambda i: (jnp.maximum(i - n_p, 0), 0)
    hbm = pl.BlockSpec(memory_space=pl.ANY)
    return pl.pallas_call(
        kernel,
        grid=(n_tiles,),
        in_specs=[
            pl.BlockSpec((TILE_M, d_ssm), tile_map),
            pl.BlockSpec((TILE_M, d_ssm), lambda i: (i, 1)),
            pl.BlockSpec((seqs_per_tile, n_state), sample_map),
            pl.BlockSpec((seqs_per_tile, n_state), sample_map),
            _const_spec(lam8[0].shape), _const_spec(lam8[1].shape),
            _const_spec(tabq[0].shape), _const_spec(tabq[1].shape),
            _const_spec(d.shape),
            _const_spec(b2[0].shape), _const_spec(b2[1].shape),
            _const_spec(ct[0].shape), _const_spec(ct[1].shape),
            _const_spec(lamg[0].shape), _const_spec(lamg[1].shape),
            hbm, hbm,
        ],
        out_specs=[
            pl.BlockSpec((TILE_M, d_model), tile_map),
            pl.BlockSpec((n_batch, n_state), lambda i: (0, 0)),
            pl.BlockSpec((n_batch, n_state), lambda i: (0, 0)),
            pl.BlockSpec((seqs_per_tile, n_state), sample_map),
            pl.BlockSpec((seqs_per_tile, n_state), sample_map),
        ],
        out_shape=[
            jax.ShapeDtypeStruct((m, d_model), BF16),
            jax.ShapeDtypeStruct((n_batch, n_state), F32),
            jax.ShapeDtypeStruct((n_batch, n_state), F32),
            jax.ShapeDtypeStruct((n_sample_seq, n_state), F32),
            jax.ShapeDtypeStruct((n_sample_seq, n_state), F32),
        ],
        scratch_shapes=[
            pltpu.VMEM((n_bundles, TILE_M // SSM_FOLD, 2 * width), F32),
            pltpu.VMEM((d_ssm // LANES, slab_rows, LANES), F32),
            pltpu.VMEM((d_ssm // LANES, slab_rows, LANES), F32),
            pltpu.VMEM((2, n_state), F32),
            pltpu.VMEM((n_bundles, SSM_FOLD * SSM_BUNDLE_CH, 2 * width), BF16),
            pltpu.VMEM((n_bundles, 2 * width, SSM_FOLD * SSM_BUNDLE_CH), BF16),
            pltpu.VMEM((n_bundles, SSM_FOLD * SSM_BUNDLE_CH, SSM_FOLD * SSM_BUNDLE_CH), BF16),
            pltpu.VMEM((d_ssm, d_ssm), BF16),
            pltpu.VMEM((d_ssm, d_model), BF16),
            pltpu.VMEM((2, STAGE_ROWS // 2, d_ssm), F32),
            pltpu.SemaphoreType.DMA((2,)),
        ],
        compiler_params=pltpu.CompilerParams(
            dimension_semantics=("arbitrary",), vmem_limit_bytes=VMEM_LIMIT_BYTES),
        name="ssm_branch",
    )(act, act, h0re, h0im, *lam8, *tabq, d, *b2, *ct, *lamg, w_glu, w_out_a)


def _tail_kernel(q_ref, bz_ref, ya_ref, sga_ref, sgc_ref, xp_ref, xs_ref, hist_ref,
                 cw_ref, lng_ref, lnb_ref, woutc_hbm, wo_hbm,
                 yp_ref, ys_ref,
                 pad_ref, conv_ref, yc_ref, woutc_ref, wo_ref, stage_ref, sem_ref,
                 *, n_prompt_tiles, tiles_per_seq, alpha):
    i = pl.program_id(0)
    is_prompt = i < n_prompt_tiles

    @pl.when(i == 0)
    def _():
        _stage_weight_bf16(woutc_hbm, 0, woutc_ref, stage_ref, sem_ref)
        _stage_weight_bf16(wo_hbm, 0, wo_ref, stage_ref, sem_ref)

    @pl.when(jnp.logical_and(is_prompt, i % tiles_per_seq == 0))
    def _():
        pad_ref[0:CONV_PAD, :] = jnp.zeros((CONV_PAD, pad_ref.shape[1]), F32)

    q = q_ref[...]
    pad_ref[CONV_PAD:CONV_PAD + TILE_M, :] = q
    w0 = cw_ref[0:1, :]
    w1 = cw_ref[1:2, :]
    w2 = cw_ref[2:3, :]

    def finish(q1, q2, x_ref, y_ref):
        conv_ref[...] = w0 * q2 + w1 * q1 + w2 * q
        yc_in = conv_ref[...].astype(BF16) * bz_ref[...]
        yc_ref[...] = jnp.dot(yc_in, woutc_ref[...], preferred_element_type=F32)
        merged = sga_ref[...] * ya_ref[...] + sgc_ref[...] * yc_ref[...].astype(BF16)
        out = jnp.dot(merged, wo_ref[...], preferred_element_type=F32)
        r = alpha * x_ref[...] + out
        mu = jnp.mean(r, axis=-1, keepdims=True)
        rc = r - mu
        var = jnp.mean(rc * rc, axis=-1, keepdims=True)
        y_ref[...] = rc * lax.rsqrt(var + LN_EPS) * lng_ref[...] + lnb_ref[...]

    @pl.when(is_prompt)
    def _():
        q1 = pad_ref[CONV_PAD - 1:CONV_PAD - 1 + TILE_M, :]
        q2 = pad_ref[CONV_PAD - 2:CONV_PAD - 2 + TILE_M, :]
        finish(q1, q2, xp_ref, yp_ref)
        pad_ref[0:CONV_PAD, :] = pad_ref[TILE_M:TILE_M + CONV_PAD, :]

    @pl.when(jnp.logical_not(is_prompt))
    def _():
        t = lax.broadcasted_iota(jnp.int32, q.shape, 0) % SUBLANES
        d_conv = q.shape[1]

        def history(j):
            return jnp.concatenate(
                [jnp.broadcast_to(hist_ref[s:s + 1, j * d_conv:(j + 1) * d_conv],
                                  (SUBLANES, d_conv)) for s in range(TILE_M // SUBLANES)], axis=0)

        e0 = history(0)
        e1 = history(1)
        q1 = jnp.where(t == 0, e1, pad_ref[CONV_PAD - 1:CONV_PAD - 1 + TILE_M, :])
        q2 = jnp.where(t == 0, e0,
                       jnp.where(t == 1, e1, pad_ref[CONV_PAD - 2:CONV_PAD - 2 + TILE_M, :]))
        finish(q1, q2, xs_ref, ys_ref)


def _tail_call(q, act, ya, xp, xs, hist, conv_w, w_out_c, w_o, ln_g, ln_b, seq_len, alpha):
    m, d_conv = q.shape
    m_p, d_model = xp.shape
    m_s = xs.shape[0]
    n_p = m_p // TILE_M
    kernel = functools.partial(_tail_kernel, n_prompt_tiles=n_p,
                               tiles_per_seq=seq_len // TILE_M, alpha=alpha)
    prompt_map = lambda i: (jnp.minimum(i, n_p - 1), 0)
    sample_map = lambda i: (jnp.maximum(i - n_p, 0), 0)
    row_map = lambda i: (i, 0)
    hbm = pl.BlockSpec(memory_space=pl.ANY)
    assert w_out_c.shape[1] == w_o.shape[1] == d_model
    return pl.pallas_call(
        kernel,
        grid=(m // TILE_M,),
        in_specs=[
            pl.BlockSpec((TILE_M, d_conv), row_map),
            pl.BlockSpec((TILE_M, d_conv), lambda i: (i, 2)),
            pl.BlockSpec((TILE_M, d_model), row_map),
            pl.BlockSpec((TILE_M, d_model), lambda i: (i, 2)),
            pl.BlockSpec((TILE_M, d_model), lambda i: (i, 3)),
            pl.BlockSpec((TILE_M, d_model), prompt_map),
            pl.BlockSpec((TILE_M, d_model), sample_map),
            pl.BlockSpec((TILE_M // SUBLANES, hist.shape[1]), sample_map),
            _const_spec(conv_w.shape),
            _const_spec(ln_g.shape),
            _const_spec(ln_b.shape),
            hbm, hbm,
        ],
        out_specs=[
            pl.BlockSpec((TILE_M, d_model), prompt_map),
            pl.BlockSpec((TILE_M, d_model), sample_map),
        ],
        out_shape=[
            jax.ShapeDtypeStruct((m_p, d_model), F32),
            jax.ShapeDtypeStruct((m_s, d_model), F32),
        ],
        scratch_shapes=[
            pltpu.VMEM((TILE_M + CONV_PAD, d_conv), F32),
            pltpu.VMEM((TILE_M, d_conv), F32),
            pltpu.VMEM((TILE_M, d_model), F32),
            pltpu.VMEM(w_out_c.shape, BF16),
            pltpu.VMEM(w_o.shape, BF16),
            pltpu.VMEM((2, STAGE_ROWS, d_model), F32),
            pltpu.SemaphoreType.DMA((2,)),
        ],
        compiler_params=pltpu.CompilerParams(
            dimension_semantics=("arbitrary",), vmem_limit_bytes=VMEM_LIMIT_BYTES),
        name="tail",
    )(q, act, ya, act, act, xp, xs, hist, conv_w, ln_g, ln_b, w_out_c, w_o)


def _ssm_params(a_re, a_im, log_dt, b_re, b_im, c_re, c_im, prompt_steps):
    g, p, gc = b_re.shape
    dt = jnp.exp(log_dt)[:, None]
    mag = jnp.exp(a_re * dt)
    ang = a_im * dt
    lam_re = mag * jnp.cos(ang)
    lam_im = mag * jnp.sin(ang)
    den = a_re * a_re + a_im * a_im
    q_re = ((lam_re - 1.0) * a_re + lam_im * a_im) / den
    q_im = (lam_im * a_re - (lam_re - 1.0) * a_im) / den
    bb_re = q_re[..., None] * b_re - q_im[..., None] * b_im
    bb_im = q_re[..., None] * b_im + q_im[..., None] * b_re

    def channel_rows(bb):
        rows = bb.transpose(0, 2, 1).reshape(g * gc, p)
        return jnp.tile(rows, (1, LANES // p))

    def state_rows(cc):
        rows = cc.transpose(0, 2, 1).reshape(g * p, gc)
        return jnp.tile(rows, (1, LANES // gc))

    b2 = (channel_rows(bb_re), channel_rows(bb_im))
    ct = (state_rows(c_re), -state_rows(c_im))
    lamg = (jnp.tile(lam_re, (1, LANES // p)), jnp.tile(lam_im, (1, LANES // p)))

    lam2_re = lam_re * lam_re - lam_im * lam_im
    lam2_im = 2.0 * lam_re * lam_im
    lam8 = (jnp.broadcast_to(lam2_re.reshape(1, -1), (SUBLANES, g * p)),
            jnp.broadcast_to(lam2_im.reshape(1, -1), (SUBLANES, g * p)))

    row = jnp.arange(SUBLANES, dtype=F32)[None, :, None]
    shift = jnp.array([1.0, 2.0, 4.0], F32)[:, None, None]
    exponent = jnp.concatenate([jnp.broadcast_to(shift, (3, SUBLANES, 1)), row + 1.0]) * prompt_steps
    keep = jnp.concatenate([row >= shift, jnp.ones((1, SUBLANES, 1), bool)])
    mag_e = jnp.where(keep, jnp.exp(exponent * (a_re * dt).reshape(1, 1, -1)), 0.0)
    ang_e = exponent * ang.reshape(1, 1, -1)
    tabq = (mag_e * jnp.cos(ang_e), mag_e * jnp.sin(ang_e))
    return b2, ct, lamg, lam8, tabq


def kernel(x_prompt, x_sample, state_ssm_re, state_ssm_im, state_conv, w_in, ssm_a_re, ssm_a_im, ssm_log_dt, ssm_b_re, ssm_b_im, ssm_c_re, ssm_c_im, ssm_d, w_glu, w_out_a, conv_w, w_out_c, w_o, ln_g, ln_b):
    depth = w_in.shape[0]
    assert depth == 1, "single-layer trunk"
    batch, seq, d_model = x_prompt.shape
    dec_batch, dec_seq, _ = x_sample.shape
    assert dec_seq == SUBLANES and seq % TILE_M == 0 and (dec_batch * dec_seq) % TILE_M == 0
    g, p, gc = ssm_b_re.shape[1:]
    d_ssm = g * gc
    d_conv = conv_w.shape[2]
    n_state = g * p
    alpha = (2 * depth) ** 0.25

    xp = x_prompt.reshape(batch * seq, d_model)
    xs = x_sample.reshape(dec_batch * dec_seq, d_model)
    m_p = xp.shape[0]
    w = w_in[0]

    assert d_ssm == d_conv == PROJ_COLS and d_model == 2 * PROJ_COLS
    act, q = _proj_call(xp, xs, w)

    ssm_raw = (ssm_a_re[0], ssm_a_im[0], ssm_log_dt[0], ssm_b_re[0], ssm_b_im[0], ssm_c_re[0],
               ssm_c_im[0])
    act, q, ssm_raw, state_ssm_re, state_ssm_im, state_conv = lax.optimization_barrier(
        (act, q, ssm_raw, state_ssm_re, state_ssm_im, state_conv))

    b2, ct, lamg, lam8, tabq = _ssm_params(*ssm_raw, TILE_M // SUBLANES)
    h0re = state_ssm_re[0].reshape(dec_batch, n_state)
    h0im = state_ssm_im[0].reshape(dec_batch, n_state)
    ya, spre, spim, ssre, ssim = _ssm_call(
        act, h0re, h0im, lam8, tabq, ssm_d[0][None, :], b2, ct, lamg, w_glu[0], w_out_a[0],
        m_p, seq, dec_seq, gc, p)

    hist = state_conv[0].reshape(dec_batch, (state_conv.shape[2]) * d_conv)
    yp, ys = _tail_call(q, act, ya, xp, xs, hist, conv_w[0], w_out_c[0], w_o[0],
                        ln_g[0][None, :], ln_b[0][None, :], seq, alpha)

    q8 = q.reshape(-1, SUBLANES, d_conv)
    q_p = q8[seq // SUBLANES - 1:m_p // SUBLANES:seq // SUBLANES, SUBLANES - 2:, :]
    q_s = q8[m_p // SUBLANES:, dec_seq - 2:, :]
    return (yp.reshape(batch, seq, d_model),
            ys.reshape(dec_batch, dec_seq, d_model),
            spre.reshape(1, batch, g, p),
            spim.reshape(1, batch, g, p),
            q_p[None],
            ssre.reshape(1, dec_batch, g, p),
            ssim.reshape(1, dec_batch, g, p),
            q_s[None])
```

```python
import functools
import math

import jax
import jax.numpy as jnp
from jax import lax
from jax.experimental import pallas as pl
from jax.experimental.pallas import tpu as pltpu

F32 = jnp.float32
BF16 = jnp.bfloat16

SUBLANES = 8
LANES = 128
VMEM_LIMIT_BYTES = 56 * 1024 * 1024

TILE_M = 256
PROJ_TILE_M = 512
SSM_FOLD = 2
SSM_BUNDLE_CH = 256
SCAN_CHUNK = 4 * LANES
CONV_PAD = SUBLANES
STAGE_ROWS = 512
NEXT_STAGE_ROWS = 128

LN_EPS = 1e-5
GELU_C = math.sqrt(2.0 / math.pi)


def _sigmoid(x):
    return 0.5 * jnp.tanh(0.5 * x) + 0.5


def _silu(x):
    return x * _sigmoid(x)


def _gelu_tanh(x):
    return 0.5 * x * (1.0 + jnp.tanh(GELU_C * (x + 0.044715 * (x * x * x))))


def _const_spec(shape):
    nd = len(shape)
    return pl.BlockSpec(shape, lambda i: (0,) * nd, pipeline_mode=pl.Buffered(1))


PROJ_GROUPS = ((0, 1), (2, 5), (3, 4), (6, 7), (8, 9))
PROJ_COLS = 1024
Q_GROUP = 2


def _epi_ssm_in(a, b, act_ref, q_ref):
    act_ref[:, :PROJ_COLS] = a.astype(BF16)
    act_ref[:, PROJ_COLS:] = _silu(b).astype(BF16)


def _epi_conv_gate(a, b, act_ref, q_ref):
    act_ref[:, :PROJ_COLS] = (a * _silu(b)).astype(BF16)


def _epi_conv_in(a, b, act_ref, q_ref):
    q_ref[...] = a * b


def _epi_merge_gate(a, b, act_ref, q_ref):
    act_ref[:, :PROJ_COLS] = _sigmoid(a).astype(BF16)
    act_ref[:, PROJ_COLS:] = _sigmoid(b).astype(BF16)


PROJ_EPILOGUES = (_epi_ssm_in, _epi_conv_gate, _epi_conv_in, _epi_merge_gate, _epi_merge_gate)


def _proj_kernel(xp_ref, xs_ref, w_hbm, act_ref, q_ref, wbf_ref, stage_ref, next_ref, sem_ref,
                 *, n_prompt_tiles):
    g = pl.program_id(0)
    i = pl.program_id(1)
    next_rows = next_ref.shape[1]
    n_next = xp_ref.shape[1] // next_rows

    @pl.when(jnp.logical_and(g == 0, i == 0))
    def _():
        for blk, col_block in enumerate(PROJ_GROUPS[0]):
            _stage_weight_bf16(w_hbm, col_block * PROJ_COLS, wbf_ref.at[0, blk], stage_ref, sem_ref)

    rows = pl.ds(pl.multiple_of(i * next_rows, next_rows), next_rows)

    def next_copy(gi, blk):
        col0 = PROJ_GROUPS[gi + 1][blk] * PROJ_COLS
        return pltpu.make_async_copy(w_hbm.at[rows, pl.ds(col0, PROJ_COLS)], next_ref.at[blk],
                                     sem_ref.at[blk])

    for gi, epilogue in enumerate(PROJ_EPILOGUES):
        @pl.when(g == gi)
        def _():
            has_next = gi + 1 < len(PROJ_GROUPS)
            if has_next:
                @pl.when(i < n_next)
                def _():
                    for blk in range(2):
                        next_copy(gi, blk).start()

            slot = gi % 2

            def compute(x_ref):
                if epilogue is _epi_conv_gate:
                    act_ref[:, PROJ_COLS:] = jnp.zeros((act_ref.shape[0], PROJ_COLS), BF16)
                xb = x_ref[...].astype(BF16)
                a = jnp.dot(xb, wbf_ref[slot, 0], preferred_element_type=F32)
                b = jnp.dot(xb, wbf_ref[slot, 1], preferred_element_type=F32)
                epilogue(a, b, act_ref, q_ref)

            @pl.when(i < n_prompt_tiles)
            def _():
                compute(xp_ref)

            @pl.when(i >= n_prompt_tiles)
            def _():
                compute(xs_ref)

            if has_next:
                @pl.when(i < n_next)
                def _():
                    for blk in range(2):
                        next_copy(gi, blk).wait()
                        wbf_ref[1 - slot, blk, rows, :] = next_ref[blk].astype(BF16)


def _proj_call(xp, xs, w_in):
    m_p, d = xp.shape
    m_s = xs.shape[0]
    tile = PROJ_TILE_M
    n_p, n_s = m_p // tile, m_s // tile
    n_tiles = n_p + n_s
    n_groups = len(PROJ_GROUPS)
    act_block = 2 * PROJ_COLS
    assert m_p % tile == 0 and m_s % tile == 0
    assert n_tiles * NEXT_STAGE_ROWS >= d and d % NEXT_STAGE_ROWS == 0
    kernel = functools.partial(_proj_kernel, n_prompt_tiles=n_p)

    def act_map(g, i):
        col = jnp.where(g < Q_GROUP, g, jnp.where(g == Q_GROUP, Q_GROUP - 1, g - 1))
        return jnp.where(g == Q_GROUP, n_tiles - 1, i), col

    def q_map(g, i):
        return jnp.where(g < Q_GROUP, 0, jnp.where(g == Q_GROUP, i, n_tiles - 1)), 0

    return pl.pallas_call(
        kernel,
        grid=(n_groups, n_tiles),
        in_specs=[
            pl.BlockSpec((tile, d), lambda g, i: (jnp.minimum(i, n_p - 1), 0)),
            pl.BlockSpec((tile, d), lambda g, i: (jnp.clip(i - n_p, 0, n_s - 1), 0)),
            pl.BlockSpec(memory_space=pl.ANY),
        ],
        out_specs=[pl.BlockSpec((tile, act_block), act_map),
                   pl.BlockSpec((tile, PROJ_COLS), q_map)],
        out_shape=[jax.ShapeDtypeStruct((m_p + m_s, (n_groups - 1) * act_block), BF16),
                   jax.ShapeDtypeStruct((m_p + m_s, PROJ_COLS), F32)],
        scratch_shapes=[
            pltpu.VMEM((2, 2, d, PROJ_COLS), BF16),
            pltpu.VMEM((2, STAGE_ROWS, PROJ_COLS), F32),
            pltpu.VMEM((2, NEXT_STAGE_ROWS, PROJ_COLS), F32),
            pltpu.SemaphoreType.DMA((2,)),
        ],
        compiler_params=pltpu.CompilerParams(
            dimension_semantics=("arbitrary", "arbitrary"), vmem_limit_bytes=VMEM_LIMIT_BYTES),
        name="proj",
    )(xp, xs, w_in)


def _pitch(steps):
    return steps if (steps // SUBLANES) % 2 == 1 else steps + SUBLANES


def _stepmajor_blocks(val, slab_ref, n_groups, steps):
    pitch = _pitch(steps)
    n_slabs = val.shape[1] // LANES
    n_sub = n_groups * SUBLANES
    for j in range(n_slabs):
        lanes = slice(j * LANES, (j + 1) * LANES)
        if pitch == steps:
            slab_ref[j, 0:n_sub * steps, :] = val[:, lanes]
        else:
            for s in range(n_sub):
                slab_ref[j, s * pitch:s * pitch + steps, :] = val[s * steps:(s + 1) * steps, lanes]
    blocks = []
    for g in range(n_groups):
        for k in range(steps):
            rows = pl.ds(g * SUBLANES * pitch + k, SUBLANES, stride=pitch)
            blocks.append(jnp.concatenate([slab_ref[j, rows, :] for j in range(n_slabs)], axis=1))
    return blocks


def _store_folded(val, slab_ref, slab0, steps):
    pitch = _pitch(steps)
    n = val.shape[1] // SSM_FOLD
    for i in range(val.shape[0] // SUBLANES):
        g, kf = divmod(i, steps // SSM_FOLD)
        for m in range(SSM_FOLD):
            rows = pl.ds(g * SUBLANES * pitch + SSM_FOLD * kf + m, SUBLANES, stride=pitch)
            for j in range(n // LANES):
                slab_ref[slab0 + j, rows, :] = val[i * SUBLANES:(i + 1) * SUBLANES,
                                                   m * n + j * LANES:m * n + (j + 1) * LANES]


def _load_natural(slab_ref, n_groups, steps):
    pitch = _pitch(steps)
    n_slabs = slab_ref.shape[0]
    n_sub = n_groups * SUBLANES
    if pitch == steps:
        return jnp.concatenate([slab_ref[j, 0:n_sub * steps, :] for j in range(n_slabs)], axis=1)
    return jnp.concatenate(
        [jnp.concatenate([slab_ref[j, s * pitch:s * pitch + steps, :] for j in range(n_slabs)],
                         axis=1) for s in range(n_sub)], axis=0)


def _cmuladd(ar, ai, br, bi, cr, ci):
    return ar * br - ai * bi + cr, ar * bi + ai * br + ci


def _scan_group(bu_ref, b, row0, steps, width, lane0, lam_ref, enter_fn, leave_fn):
    for c in range(width // SCAN_CHUNK):
        re = slice(c * SCAN_CHUNK, (c + 1) * SCAN_CHUNK)
        im = slice(width + c * SCAN_CHUNK, width + (c + 1) * SCAN_CHUNK)
        tl = slice(lane0 + c * SCAN_CHUNK, lane0 + (c + 1) * SCAN_CHUNK)
        lr, li = lam_ref[0][:, tl], lam_ref[1][:, tl]

        def x_block(k):
            rows = slice(row0 + k * SUBLANES, row0 + (k + 1) * SUBLANES)
            return rows, bu_ref[b, rows, re], bu_ref[b, rows, im]

        _, pr, pi = x_block(0)
        for k in range(1, steps):
            _, xr, xi = x_block(k)
            pr, pi = _cmuladd(lr, li, pr, pi, xr, xi)
        hr, hi = enter_fn(pr, pi, tl)
        for k in range(steps):
            rows, xr, xi = x_block(k)
            bu_ref[b, rows, re] = hr
            bu_ref[b, rows, im] = hi
            hr, hi = _cmuladd(lr, li, hr, hi, xr, xi)
        leave_fn(hr, hi, tl)


def _build_folded_weights(b2_ref, ct_ref, lam_ref, d_ref, bq_ref, wg_ref, wu_ref, group_ch,
                          group_states):
    n_bundles = bq_ref.shape[0]
    n_ch = bq_ref.shape[1] // SSM_FOLD
    width = bq_ref.shape[2] // 2
    ch_shift = group_ch.bit_length() - 1
    st_shift = group_states.bit_length() - 1
    assert group_ch == 1 << ch_shift and group_states == 1 << st_shift
    groups = n_ch // group_ch

    def group_of(n_rows, axis, shift):
        return lax.shift_right_logical(lax.broadcasted_iota(jnp.int32, (n_rows, LANES), axis), shift)

    ch_rows = group_of(n_ch, 0, ch_shift)
    st_rows = group_of(width, 0, st_shift)
    st_lanes = group_of(n_ch, 1, st_shift)
    ch_lanes = group_of(width, 1, ch_shift)

    def fill_b(b, pos, src_pair):
        for part in range(2):
            for j in range(width // LANES):
                keep = ch_rows == st_lanes + j * (LANES // group_states)
                lanes = slice(part * width + j * LANES, part * width + (j + 1) * LANES)
                bq_ref[b, pos * n_ch:(pos + 1) * n_ch, lanes] = (
                    jnp.where(keep, src_pair[part], 0.0).astype(BF16))

    def fill_c(b, pos, src_pair):
        for part in range(2):
            for j in range(n_ch // LANES):
                keep = st_rows == ch_lanes + j * (LANES // group_ch)
                lanes = slice(pos * n_ch + j * LANES, pos * n_ch + (j + 1) * LANES)
                wg_ref[b, part * width:(part + 1) * width, lanes] = (
                    jnp.where(keep, src_pair[part], 0.0).astype(BF16))

    for b in range(n_bundles):
        ch = slice(b * n_ch, (b + 1) * n_ch)
        st = slice(b * width, (b + 1) * width)
        per_group = [(lam_ref[0][g:g + 1, :], lam_ref[1][g:g + 1, :])
                     for g in range(b * groups, (b + 1) * groups)]
        lam_ch = [jnp.concatenate([jnp.broadcast_to(l[part], (group_ch, LANES)) for l in per_group],
                                  axis=0) for part in range(2)]
        lam_st = [jnp.concatenate([jnp.broadcast_to(l[part], (LANES, LANES)).T[:group_states, :]
                                   for l in per_group], axis=0) for part in range(2)]
        b_re, b_im = b2_ref[0][ch, :], b2_ref[1][ch, :]
        fill_b(b, 0, (lam_ch[0] * b_re - lam_ch[1] * b_im, lam_ch[0] * b_im + lam_ch[1] * b_re))
        fill_b(b, 1, (b_re, b_im))
        c_re, c_nim = ct_ref[0][st, :], ct_ref[1][st, :]
        c1 = (c_re * lam_st[0] + c_nim * lam_st[1], c_nim * lam_st[0] - c_re * lam_st[1])
        c2 = (c1[0] * lam_st[0] + c1[1] * lam_st[1], c1[1] * lam_st[0] - c1[0] * lam_st[1])
        fill_c(b, 0, c1)
        fill_c(b, 1, (c_re, c_nim))
        b_big = bq_ref[b, n_ch:2 * n_ch, :]
        k1 = jnp.dot(b_big, wg_ref[b, :, 0:n_ch], preferred_element_type=F32).astype(BF16)
        k0 = jnp.dot(b_big, wg_ref[b, :, n_ch:2 * n_ch], preferred_element_type=F32)
        diagonal = (lax.broadcasted_iota(jnp.int32, (n_ch, n_ch), 0)
                    == lax.broadcasted_iota(jnp.int32, (n_ch, n_ch), 1))
        k0 = (k0 + jnp.where(diagonal, d_ref[:, ch], 0.0)).astype(BF16)
        wu_ref[b, 0:n_ch, 0:n_ch] = k0
        wu_ref[b, 0:n_ch, n_ch:2 * n_ch] = k1
        wu_ref[b, n_ch:2 * n_ch, 0:n_ch] = jnp.zeros((n_ch, n_ch), BF16)
        wu_ref[b, n_ch:2 * n_ch, n_ch:2 * n_ch] = k0
        fill_c(b, 1, c2)


def _stage_weight_bf16(w_hbm, col0, dst_ref, stage_ref, sem_ref):
    k, n = dst_ref.shape
    rows = stage_ref.shape[1]
    n_chunks = k // rows

    def copy(c):
        return pltpu.make_async_copy(w_hbm.at[pl.ds(c * rows, rows), pl.ds(col0, n)],
                                     stage_ref.at[c % 2], sem_ref.at[c % 2])

    copy(0).start()
    for c in range(n_chunks):
        if c + 1 < n_chunks:
            copy(c + 1).start()
        copy(c).wait()
        dst_ref[c * rows:(c + 1) * rows, :] = stage_ref[c % 2].astype(BF16)


def _ssm_kernel(u_ref, sza_ref, h0re_ref, h0im_ref, lamre_ref, lamim_ref, tqre_ref, tqim_ref, d_ref,
                b2re_ref, b2im_ref, ctre_ref, ctim_ref, lgre_ref, lgim_ref, wglu_hbm, wouta_hbm,
                ya_ref, spre_ref, spim_ref, ssre_ref, ssim_ref,
                bu_ref, uslab_ref, yslab_ref, carry_ref, bq_ref, wg_ref, wu_ref,
                wglu_ref, wouta_ref, stage_narrow_ref, sem_ref,
                *, n_prompt_tiles, tiles_per_seq, sample_steps, group_ch, group_states):
    i = pl.program_id(0)
    n_bundles = bq_ref.shape[0]
    width = bq_ref.shape[2] // 2
    is_prompt = i < n_prompt_tiles
    t_in_seq = i % tiles_per_seq

    @pl.when(i == 0)
    def _():
        _stage_weight_bf16(wouta_hbm, 0, wouta_ref, bu_ref, sem_ref)
        _stage_weight_bf16(wglu_hbm, 0, wglu_ref, stage_narrow_ref, sem_ref)
        _build_folded_weights((b2re_ref, b2im_ref), (ctre_ref, ctim_ref), (lgre_ref, lgim_ref),
                              d_ref, bq_ref, wg_ref, wu_ref, group_ch, group_states)

    @pl.when(jnp.logical_and(is_prompt, t_in_seq == 0))
    def _():
        carry_ref[...] = jnp.zeros_like(carry_ref)

    def run(n_groups, steps, enter_factory, leave_factory):
        folded = steps // SSM_FOLD
        d_ssm = u_ref.shape[1]
        blocks = _stepmajor_blocks(u_ref[...].astype(F32), uslab_ref, n_groups, steps)
        u_f = jnp.concatenate([jnp.concatenate(blocks[k:k + SSM_FOLD], axis=1)
                               for k in range(0, len(blocks), SSM_FOLD)], axis=0).astype(BF16)

        def u_bundle(b):
            return jnp.concatenate(
                [u_f[:, m * d_ssm + b * SSM_BUNDLE_CH:m * d_ssm + (b + 1) * SSM_BUNDLE_CH]
                 for m in range(SSM_FOLD)], axis=1)

        def b_matmul(b):
            bu_ref[b] = jnp.dot(u_bundle(b), bq_ref[b], preferred_element_type=F32)

        b_matmul(0)
        for b in range(n_bundles):
            if b + 1 < n_bundles:
                b_matmul(b + 1)
            for g in range(n_groups):
                _scan_group(bu_ref, b, g * SUBLANES * folded, folded, width, b * width,
                            (lamre_ref, lamim_ref), enter_factory(g), leave_factory(g))
            y_b = (jnp.dot(bu_ref[b].astype(BF16), wg_ref[b], preferred_element_type=F32)
                   + jnp.dot(u_bundle(b), wu_ref[b], preferred_element_type=F32))
            _store_folded(y_b, yslab_ref, b * (SSM_BUNDLE_CH // LANES), steps)
        y = _load_natural(yslab_ref, n_groups, steps)
        g_act = _gelu_tanh(y)
        z = jnp.dot(g_act.astype(BF16), wglu_ref[...], preferred_element_type=F32)
        o = g_act * _sigmoid(z) * sza_ref[...].astype(F32)
        ya = jnp.dot(o.astype(BF16), wouta_ref[...], preferred_element_type=F32)
        ya_ref[...] = ya.astype(BF16)

    def prompt_enter(_g):
        def enter(er, ei, tl):
            xr, xi = er, ei
            for step, shift in enumerate((1, 2, 4)):
                xr, xi = _cmuladd(tqre_ref[step, :, tl], tqim_ref[step, :, tl],
                                  pltpu.roll(xr, shift, 0), pltpu.roll(xi, shift, 0), xr, xi)
            c0r = jnp.broadcast_to(carry_ref[0:1, tl], xr.shape)
            c0i = jnp.broadcast_to(carry_ref[1:2, tl], xi.shape)
            xr, xi = _cmuladd(tqre_ref[3, :, tl], tqim_ref[3, :, tl], c0r, c0i, xr, xi)
            carry_ref[0:1, tl] = xr[SUBLANES - 1:SUBLANES, :]
            carry_ref[1:2, tl] = xi[SUBLANES - 1:SUBLANES, :]
            first = lax.broadcasted_iota(jnp.int32, xr.shape, 0) == 0
            return (jnp.where(first, c0r, pltpu.roll(xr, 1, 0)),
                    jnp.where(first, c0i, pltpu.roll(xi, 1, 0)))
        return enter

    def prompt_leave(_g):
        return lambda hr, hi, tl: None

    def sample_enter(g):
        rows = slice(g * SUBLANES, (g + 1) * SUBLANES)
        return lambda er, ei, tl: (h0re_ref[rows, tl], h0im_ref[rows, tl])

    def sample_leave(g):
        rows = slice(g * SUBLANES, (g + 1) * SUBLANES)

        def leave(hr, hi, tl):
            ssre_ref[rows, tl] = hr
            ssim_ref[rows, tl] = hi
        return leave

    @pl.when(is_prompt)
    def _():
        run(1, TILE_M // SUBLANES, prompt_enter, prompt_leave)

    @pl.when(jnp.logical_not(is_prompt))
    def _():
        run(TILE_M // (SUBLANES * sample_steps), sample_steps, sample_enter, sample_leave)

    @pl.when(jnp.logical_and(is_prompt, t_in_seq == tiles_per_seq - 1))
    def _():
        row = pl.ds(i // tiles_per_seq, 1)
        spre_ref[row, :] = carry_ref[0:1, :]
        spim_ref[row, :] = carry_ref[1:2, :]


def _ssm_call(act, h0re, h0im, lam8, tabq, d, b2, ct, lamg, w_glu, w_out_a, n_prompt_rows,
              seq_len, sample_steps, group_ch, group_states):
    m = act.shape[0]
    d_ssm, d_model = w_out_a.shape
    n_state = lam8[0].shape[1]
    n_tiles = m // TILE_M
    n_p = n_prompt_rows // TILE_M
    n_batch = n_prompt_rows // seq_len
    seqs_per_tile = TILE_M // sample_steps
    n_sample_seq = h0re.shape[0]
    slab_rows = SUBLANES * _pitch(TILE_M // SUBLANES)
    n_bundles = d_ssm // SSM_BUNDLE_CH
    width = n_state // n_bundles
    assert 2 * width == d_model and n_bundles >= 2 and d_ssm % TILE_M == 0
    assert sample_steps % SSM_FOLD == 0
    kernel = functools.partial(_ssm_kernel, n_prompt_tiles=n_p,
                               tiles_per_seq=seq_len // TILE_M, sample_steps=sample_steps,
                               group_ch=group_ch, group_states=group_states)
    tile_map = lambda i: (i, 0)
    sample_map = lambda i: (jnp.maximum(i - n_p, 0), 0)
    hbm = pl.BlockSpec(memory_space=pl.ANY)
    return pl.pallas_call(
        kernel,
        grid=(n_tiles,),
        in_specs=[
            pl.BlockSpec((TILE_M, d_ssm), tile_map),
            pl.BlockSpec((TILE_M, d_ssm), lambda i: (i, 1)),
            pl.BlockSpec((seqs_per_tile, n_state), sample_map),
            pl.BlockSpec((seqs_per_tile, n_state), sample_map),
            _const_spec(lam8[0].shape), _const_spec(lam8[1].shape),
            _const_spec(tabq[0].shape), _const_spec(tabq[1].shape),
            _const_spec(d.shape),
            _const_spec(b2[0].shape), _const_spec(b2[1].shape),
            _const_spec(ct[0].shape), _const_spec(ct[1].shape),
            _const_spec(lamg[0].shape), _const_spec(lamg[1].shape),
            hbm, hbm,
        ],
        out_specs=[
            pl.BlockSpec((TILE_M, d_model), tile_map),
            pl.BlockSpec((n_batch, n_state), lambda i: (0, 0)),
            pl.BlockSpec((n_batch, n_state), lambda i: (0, 0)),
            pl.BlockSpec((seqs_per_tile, n_state), sample_map),
            pl.BlockSpec((seqs_per_tile, n_state), sample_map),
        ],
        out_shape=[
            jax.ShapeDtypeStruct((m, d_model), BF16),
            jax.ShapeDtypeStruct((n_batch, n_state), F32),
            jax.ShapeDtypeStruct((n_batch, n_state), F32),
            jax.ShapeDtypeStruct((n_sample_seq, n_state), F32),
            jax.ShapeDtypeStruct((n_sample_seq, n_state), F32),
        ],
        scratch_shapes=[
            pltpu.VMEM((n_bundles, TILE_M // SSM_FOLD, 2 * width), F32),
            pltpu.VMEM((d_ssm // LANES, slab_rows, LANES), F32),
            pltpu.VMEM((d_ssm // LANES, slab_rows, LANES), F32),
            pltpu.VMEM((2, n_state), F32),
            pltpu.VMEM((n_bundles, SSM_FOLD * SSM_BUNDLE_CH, 2 * width), BF16),
            pltpu.VMEM((n_bundles, 2 * width, SSM_FOLD * SSM_BUNDLE_CH), BF16),
            pltpu.VMEM((n_bundles, SSM_FOLD * SSM_BUNDLE_CH, SSM_FOLD * SSM_BUNDLE_CH), BF16),
            pltpu.VMEM((d_ssm, d_ssm), BF16),
            pltpu.VMEM((d_ssm, d_model), BF16),
            pltpu.VMEM((2, STAGE_ROWS // 2, d_ssm), F32),
            pltpu.SemaphoreType.DMA((2,)),
        ],
        compiler_params=pltpu.CompilerParams(
            dimension_semantics=("arbitrary",), vmem_limit_bytes=VMEM_LIMIT_BYTES),
        name="ssm_branch",
    )(act, act, h0re, h0im, *lam8, *tabq, d, *b2, *ct, *lamg, w_glu, w_out_a)


def _tail_kernel(q_ref, bz_ref, ya_ref, sga_ref, sgc_ref, xp_ref, xs_ref, hist_ref,
                 cw_ref, lng_ref, lnb_ref, woutc_hbm, wo_hbm,
                 yp_ref, ys_ref,
                 pad_ref, conv_ref, yc_ref, woutc_ref, wo_ref, stage_ref, sem_ref,
                 *, n_prompt_tiles, tiles_per_seq, alpha):
    i = pl.program_id(0)
    is_prompt = i < n_prompt_tiles

    @pl.when(i == 0)
    def _():
        _stage_weight_bf16(woutc_hbm, 0, woutc_ref, stage_ref, sem_ref)
        _stage_weight_bf16(wo_hbm, 0, wo_ref, stage_ref, sem_ref)

    @pl.when(jnp.logical_and(is_prompt, i % tiles_per_seq == 0))
    def _():
        pad_ref[0:CONV_PAD, :] = jnp.zeros((CONV_PAD, pad_ref.shape[1]), F32)

    q = q_ref[...]
    pad_ref[CONV_PAD:CONV_PAD + TILE_M, :] = q
    w0 = cw_ref[0:1, :]
    w1 = cw_ref[1:2, :]
    w2 = cw_ref[2:3, :]

    def finish(q1, q2, x_ref, y_ref):
        conv_ref[...] = w0 * q2 + w1 * q1 + w2 * q
        yc_in = conv_ref[...].astype(BF16) * bz_ref[...]
        yc_ref[...] = jnp.dot(yc_in, woutc_ref[...], preferred_element_type=F32)
        merged = sga_ref[...] * ya_ref[...] + sgc_ref[...] * yc_ref[...].astype(BF16)
        out = jnp.dot(merged, wo_ref[...], preferred_element_type=F32)
        r = alpha * x_ref[...] + out
        mu = jnp.mean(r, axis=-1, keepdims=True)
        rc = r - mu
        var = jnp.mean(rc * rc, axis=-1, keepdims=True)
        y_ref[...] = rc * lax.rsqrt(var + LN_EPS) * lng_ref[...] + lnb_ref[...]

    @pl.when(is_prompt)
    def _():
        q1 = pad_ref[CONV_PAD - 1:CONV_PAD - 1 + TILE_M, :]
        q2 = pad_ref[CONV_PAD - 2:CONV_PAD - 2 + TILE_M, :]
        finish(q1, q2, xp_ref, yp_ref)
        pad_ref[0:CONV_PAD, :] = pad_ref[TILE_M:TILE_M + CONV_PAD, :]

    @pl.when(jnp.logical_not(is_prompt))
    def _():
        t = lax.broadcasted_iota(jnp.int32, q.shape, 0) % SUBLANES
        d_conv = q.shape[1]

        def history(j):
            return jnp.concatenate(
                [jnp.broadcast_to(hist_ref[s:s + 1, j * d_conv:(j + 1) * d_conv],
                                  (SUBLANES, d_conv)) for s in range(TILE_M // SUBLANES)], axis=0)

        e0 = history(0)
        e1 = history(1)
        q1 = jnp.where(t == 0, e1, pad_ref[CONV_PAD - 1:CONV_PAD - 1 + TILE_M, :])
        q2 = jnp.where(t == 0, e0,
                       jnp.where(t == 1, e1, pad_ref[CONV_PAD - 2:CONV_PAD - 2 + TILE_M, :]))
        finish(q1, q2, xs_ref, ys_ref)


def _tail_call(q, act, ya, xp, xs, hist, conv_w, w_out_c, w_o, ln_g, ln_b, seq_len, alpha):
    m, d_conv = q.shape
    m_p, d_model = xp.shape
    m_s = xs.shape[0]
    n_p = m_p // TILE_M
    kernel = functools.partial(_tail_kernel, n_prompt_tiles=n_p,
                               tiles_per_seq=seq_len // TILE_M, alpha=alpha)
    prompt_map = lambda i: (jnp.minimum(i, n_p - 1), 0)
    sample_map = lambda i: (jnp.maximum(i - n_p, 0), 0)
    row_map = lambda i: (i, 0)
    hbm = pl.BlockSpec(memory_space=pl.ANY)
    assert w_out_c.shape[1] == w_o.shape[1] == d_model
    return pl.pallas_call(
        kernel,
        grid=(m // TILE_M,),
        in_specs=[
            pl.BlockSpec((TILE_M, d_conv), row_map),
            pl.BlockSpec((TILE_M, d_conv), lambda i: (i, 2)),
            pl.BlockSpec((TILE_M, d_model), row_map),
            pl.BlockSpec((TILE_M, d_model), lambda i: (i, 2)),
            pl.BlockSpec((TILE_M, d_model), lambda i: (i, 3)),
            pl.BlockSpec((TILE_M, d_model), prompt_map),
            pl.BlockSpec((TILE_M, d_model), sample_map),
            pl.BlockSpec((TILE_M // SUBLANES, hist.shape[1]), sample_map),
            _const_spec(conv_w.shape),
            _const_spec(ln_g.shape),
            _const_spec(ln_b.shape),
            hbm, hbm,
        ],
        out_specs=[
            pl.BlockSpec((TILE_M, d_model), prompt_map),
            pl.BlockSpec((TILE_M, d_model), sample_map),
        ],
        out_shape=[
            jax.ShapeDtypeStruct((m_p, d_model), F32),
            jax.ShapeDtypeStruct((m_s, d_model), F32),
        ],
        scratch_shapes=[
            pltpu.VMEM((TILE_M + CONV_PAD, d_conv), F32),
            pltpu.VMEM((TILE_M, d_conv), F32),
            pltpu.VMEM((TILE_M, d_model), F32),
            pltpu.VMEM(w_out_c.shape, BF16),
            pltpu.VMEM(w_o.shape, BF16),
            pltpu.VMEM((2, STAGE_ROWS, d_model), F32),
            pltpu.SemaphoreType.DMA((2,)),
        ],
        compiler_params=pltpu.CompilerParams(
            dimension_semantics=("arbitrary",), vmem_limit_bytes=VMEM_LIMIT_BYTES),
        name="tail",
    )(q, act, ya, act, act, xp, xs, hist, conv_w, ln_g, ln_b, w_out_c, w_o)


def _ssm_params(a_re, a_im, log_dt, b_re, b_im, c_re, c_im, prompt_steps):
    g, p, gc = b_re.shape
    dt = jnp.exp(log_dt)[:, None]
    mag = jnp.exp(a_re * dt)
    ang = a_im * dt
    lam_re = mag * jnp.cos(ang)
    lam_im = mag * jnp.sin(ang)
    den = a_re * a_re + a_im * a_im
    q_re = ((lam_re - 1.0) * a_re + lam_im * a_im) / den
    q_im = (lam_im * a_re - (lam_re - 1.0) * a_im) / den
    bb_re = q_re[..., None] * b_re - q_im[..., None] * b_im
    bb_im = q_re[..., None] * b_im + q_im[..., None] * b_re

    def channel_rows(bb):
        rows = bb.transpose(0, 2, 1).reshape(g * gc, p)
        return jnp.tile(rows, (1, LANES // p))

    def state_rows(cc):
        rows = cc.transpose(0, 2, 1).reshape(g * p, gc)
        return jnp.tile(rows, (1, LANES // gc))

    b2 = (channel_rows(bb_re), channel_rows(bb_im))
    ct = (state_rows(c_re), -state_rows(c_im))
    lamg = (jnp.tile(lam_re, (1, LANES // p)), jnp.tile(lam_im, (1, LANES // p)))

    lam2_re = lam_re * lam_re - lam_im * lam_im
    lam2_im = 2.0 * lam_re * lam_im
    lam8 = (jnp.broadcast_to(lam2_re.reshape(1, -1), (SUBLANES, g * p)),
            jnp.broadcast_to(lam2_im.reshape(1, -1), (SUBLANES, g * p)))

    row = jnp.arange(SUBLANES, dtype=F32)[None, :, None]
    shift = jnp.array([1.0, 2.0, 4.0], F32)[:, None, None]
    exponent = jnp.concatenate([jnp.broadcast_to(shift, (3, SUBLANES, 1)), row + 1.0]) * prompt_steps
    keep = jnp.concatenate([row >= shift, jnp.ones((1, SUBLANES, 1), bool)])
    mag_e = jnp.where(keep, jnp.exp(exponent * (a_re * dt).reshape(1, 1, -1)), 0.0)
    ang_e = exponent * ang.reshape(1, 1, -1)
    tabq = (mag_e * jnp.cos(ang_e), mag_e * jnp.sin(ang_e))
    return b2, ct, lamg, lam8, tabq


def kernel(x_prompt, x_sample, state_ssm_re, state_ssm_im, state_conv, w_in, ssm_a_re, ssm_a_im, ssm_log_dt, ssm_b_re, ssm_b_im, ssm_c_re, ssm_c_im, ssm_d, w_glu, w_out_a, conv_w, w_out_c, w_o, ln_g, ln_b):
    depth = w_in.shape[0]
    assert depth == 1, "single-layer trunk"
    batch, seq, d_model = x_prompt.shape
    dec_batch, dec_seq, _ = x_sample.shape
    assert dec_seq == SUBLANES and seq % TILE_M == 0 and (dec_batch * dec_seq) % TILE_M == 0
    g, p, gc = ssm_b_re.shape[1:]
    d_ssm = g * gc
    d_conv = conv_w.shape[2]
    n_state = g * p
    alpha = (2 * depth) ** 0.25

    xp = x_prompt.reshape(batch * seq, d_model)
    xs = x_sample.reshape(dec_batch * dec_seq, d_model)
    m_p = xp.shape[0]
    w = w_in[0]

    assert d_ssm == d_conv == PROJ_COLS and d_model == 2 * PROJ_COLS
    act, q = _proj_call(xp, xs, w)

    ssm_raw = (ssm_a_re[0], ssm_a_im[0], ssm_log_dt[0], ssm_b_re[0], ssm_b_im[0], ssm_c_re[0],
               ssm_c_im[0])
    act, q, ssm_raw, state_ssm_re, state_ssm_im, state_conv = lax.optimization_barrier(
        (act, q, ssm_raw, state_ssm_re, state_ssm_im, state_conv))

    b2, ct, lamg, lam8, tabq = _ssm_params(*ssm_raw, TILE_M // SUBLANES)
    h0re = state_ssm_re[0].reshape(dec_batch, n_state)
    h0im = state_ssm_im[0].reshape(dec_batch, n_state)
    ya, spre, spim, ssre, ssim = _ssm_call(
        act, h0re, h0im, lam8, tabq, ssm_d[0][None, :], b2, ct, lamg, w_glu[0], w_out_a[0],
        m_p, seq, dec_seq, gc, p)

    hist = state_conv[0].reshape(dec_batch, (state_conv.shape[2]) * d_conv)
    yp, ys = _tail_call(q, act, ya, xp, xs, hist, conv_w[0], w_out_c[0], w_o[0],
                        ln_g[0][None, :], ln_b[0][None, :], seq, alpha)

    q8 = q.reshape(-1, SUBLANES, d_conv)
    q_p = q8[seq // SUBLANES - 1:m_p // SUBLANES:seq // SUBLANES, SUBLANES - 2:, :]
    q_s = q8[m_p // SUBLANES:, dec_seq - 2:, :]
    return (yp.reshape(batch, seq, d_model),
            ys.reshape(dec_batch, dec_seq, d_model),
            spre.reshape(1, batch, g, p),
            spim.reshape(1, batch, g, p),
            q_p[None],
            ssre.reshape(1, dec_batch, g, p),
            ssim.reshape(1, dec_batch, g, p),
            q_s[None])
```

```python
import functools
import math

import jax
import jax.numpy as jnp
from jax import lax
from jax.experimental import pallas as pl
from jax.experimental.pallas import tpu as pltpu

F32 = jnp.float32
BF16 = jnp.bfloat16

SUBLANES = 8
LANES = 128
VMEM_LIMIT_BYTES = 56 * 1024 * 1024

TILE_M = 256
PROJ_TILE_M = 512
SSM_FOLD = 2
SSM_BUNDLE_CH = 256
SCAN_CHUNK = 4 * LANES
CONV_PAD = SUBLANES
STAGE_ROWS = 512
NEXT_STAGE_ROWS = 128

LN_EPS = 1e-5
GELU_C = math.sqrt(2.0 / math.pi)


def _sigmoid(x):
    return 0.5 * jnp.tanh(0.5 * x) + 0.5


def _silu(x):
    return x * _sigmoid(x)


def _gelu_tanh(x):
    return 0.5 * x * (1.0 + jnp.tanh(GELU_C * (x + 0.044715 * (x * x * x))))


def _const_spec(shape):
    nd = len(shape)
    return pl.BlockSpec(shape, lambda i: (0,) * nd, pipeline_mode=pl.Buffered(1))


PROJ_GROUPS = ((0, 1), (2, 5), (3, 4), (6, 7), (8, 9))
PROJ_COLS = 1024
Q_GROUP = 2


def _epi_ssm_in(a, b, act_ref, q_ref):
    act_ref[:, :PROJ_COLS] = a.astype(BF16)
    act_ref[:, PROJ_COLS:] = _silu(b).astype(BF16)


def _epi_conv_gate(a, b, act_ref, q_ref):
    act_ref[:, :PROJ_COLS] = (a * _silu(b)).astype(BF16)


def _epi_conv_in(a, b, act_ref, q_ref):
    q_ref[...] = a * b


def _epi_merge_gate(a, b, act_ref, q_ref):
    act_ref[:, :PROJ_COLS] = _sigmoid(a).astype(BF16)
    act_ref[:, PROJ_COLS:] = _sigmoid(b).astype(BF16)


PROJ_EPILOGUES = (_epi_ssm_in, _epi_conv_gate, _epi_conv_in, _epi_merge_gate, _epi_merge_gate)


def _proj_kernel(xp_ref, xs_ref, w_hbm, act_ref, q_ref, wbf_ref, stage_ref, next_ref, sem_ref,
                 *, n_prompt_tiles):
    g = pl.program_id(0)
    i = pl.program_id(1)
    next_rows = next_ref.shape[1]
    n_next = xp_ref.shape[1] // next_rows

    @pl.when(jnp.logical_and(g == 0, i == 0))
    def _():
        for blk, col_block in enumerate(PROJ_GROUPS[0]):
            _stage_weight_bf16(w_hbm, col_block * PROJ_COLS, wbf_ref.at[0, blk], stage_ref, sem_ref)

    rows = pl.ds(pl.multiple_of(i * next_rows, next_rows), next_rows)

    def next_copy(gi, blk):
        col0 = PROJ_GROUPS[gi + 1][blk] * PROJ_COLS
        return pltpu.make_async_copy(w_hbm.at[rows, pl.ds(col0, PROJ_COLS)], next_ref.at[blk],
                                     sem_ref.at[blk])

    for gi, epilogue in enumerate(PROJ_EPILOGUES):
        @pl.when(g == gi)
        def _():
            has_next = gi + 1 < len(PROJ_GROUPS)
            if has_next:
                @pl.when(i < n_next)
                def _():
                    for blk in range(2):
                        next_copy(gi, blk).start()

            slot = gi % 2

            def compute(x_ref):
                if epilogue is _epi_conv_gate:
                    act_ref[:, PROJ_COLS:] = jnp.zeros((act_ref.shape[0], PROJ_COLS), BF16)
                xb = x_ref[...].astype(BF16)
                a = jnp.dot(xb, wbf_ref[slot, 0], preferred_element_type=F32)
                b = jnp.dot(xb, wbf_ref[slot, 1], preferred_element_type=F32)
                epilogue(a, b, act_ref, q_ref)

            @pl.when(i < n_prompt_tiles)
            def _():
                compute(xp_ref)

            @pl.when(i >= n_prompt_tiles)
            def _():
                compute(xs_ref)

            if has_next:
                @pl.when(i < n_next)
                def _():
                    for blk in range(2):
                        next_copy(gi, blk).wait()
                        wbf_ref[1 - slot, blk, rows, :] = next_ref[blk].astype(BF16)


def _proj_call(xp, xs, w_in):
    m_p, d = xp.shape
    m_s = xs.shape[0]
    tile = PROJ_TILE_M
    n_p, n_s = m_p // tile, m_s // tile
    n_tiles = n_p + n_s
    n_groups = len(PROJ_GROUPS)
    act_block = 2 * PROJ_COLS
    assert m_p % tile == 0 and m_s % tile == 0
    assert n_tiles * NEXT_STAGE_ROWS >= d and d % NEXT_STAGE_ROWS == 0
    kernel = functools.partial(_proj_kernel, n_prompt_tiles=n_p)

    def act_map(g, i):
        col = jnp.where(g < Q_GROUP, g, jnp.where(g == Q_GROUP, Q_GROUP - 1, g - 1))
        return jnp.where(g == Q_GROUP, n_tiles - 1, i), col

    def q_map(g, i):
        return jnp.where(g < Q_GROUP, 0, jnp.where(g == Q_GROUP, i, n_tiles - 1)), 0

    return pl.pallas_call(
        kernel,
        grid=(n_groups, n_tiles),
        in_specs=[
            pl.BlockSpec((tile, d), lambda g, i: (jnp.minimum(i, n_p - 1), 0)),
            pl.BlockSpec((tile, d), lambda g, i: (jnp.clip(i - n_p, 0, n_s - 1), 0)),
            pl.BlockSpec(memory_space=pl.ANY),
        ],
        out_specs=[pl.BlockSpec((tile, act_block), act_map),
                   pl.BlockSpec((tile, PROJ_COLS), q_map)],
        out_shape=[jax.ShapeDtypeStruct((m_p + m_s, (n_groups - 1) * act_block), BF16),
                   jax.ShapeDtypeStruct((m_p + m_s, PROJ_COLS), F32)],
        scratch_shapes=[
            pltpu.VMEM((2, 2, d, PROJ_COLS), BF16),
            pltpu.VMEM((2, STAGE_ROWS, PROJ_COLS), F32),
            pltpu.VMEM((2, NEXT_STAGE_ROWS, PROJ_COLS), F32),
            pltpu.SemaphoreType.DMA((2,)),
        ],
        compiler_params=pltpu.CompilerParams(
            dimension_semantics=("arbitrary", "arbitrary"), vmem_limit_bytes=VMEM_LIMIT_BYTES),
        name="proj",
    )(xp, xs, w_in)


def _pitch(steps):
    return steps if (steps // SUBLANES) % 2 == 1 else steps + SUBLANES


def _stepmajor_blocks(val, slab_ref, n_groups, steps):
    pitch = _pitch(steps)
    n_slabs = val.shape[1] // LANES
    n_sub = n_groups * SUBLANES
    for j in range(n_slabs):
        lanes = slice(j * LANES, (j + 1) * LANES)
        if pitch == steps:
            slab_ref[j, 0:n_sub * steps, :] = val[:, lanes]
        else:
            for s in range(n_sub):
                slab_ref[j, s * pitch:s * pitch + steps, :] = val[s * steps:(s + 1) * steps, lanes]
    blocks = []
    for g in range(n_groups):
        for k in range(steps):
            rows = pl.ds(g * SUBLANES * pitch + k, SUBLANES, stride=pitch)
            blocks.append(jnp.concatenate([slab_ref[j, rows, :] for j in range(n_slabs)], axis=1))
    return blocks


def _store_folded(val, slab_ref, slab0, steps):
    pitch = _pitch(steps)
    n = val.shape[1] // SSM_FOLD
    for i in range(val.shape[0] // SUBLANES):
        g, kf = divmod(i, steps // SSM_FOLD)
        for m in range(SSM_FOLD):
            rows = pl.ds(g * SUBLANES * pitch + SSM_FOLD * kf + m, SUBLANES, stride=pitch)
            for j in range(n // LANES):
                slab_ref[slab0 + j, rows, :] = val[i * SUBLANES:(i + 1) * SUBLANES,
                                                   m * n + j * LANES:m * n + (j + 1) * LANES]


def _load_natural(slab_ref, n_groups, steps):
    pitch = _pitch(steps)
    n_slabs = slab_ref.shape[0]
    n_sub = n_groups * SUBLANES
    if pitch == steps:
        return jnp.concatenate([slab_ref[j, 0:n_sub * steps, :] for j in range(n_slabs)], axis=1)
    return jnp.concatenate(
        [jnp.concatenate([slab_ref[j, s * pitch:s * pitch + steps, :] for j in range(n_slabs)],
                         axis=1) for s in range(n_sub)], axis=0)


def _cmuladd(ar, ai, br, bi, cr, ci):
    return ar * br - ai * bi + cr, ar * bi + ai * br + ci


def _scan_group(bu_ref, b, row0, steps, width, lane0, lam_ref, enter_fn, leave_fn):
    for c in range(width // SCAN_CHUNK):
        re = slice(c * SCAN_CHUNK, (c + 1) * SCAN_CHUNK)
        im = slice(width + c * SCAN_CHUNK, width + (c + 1) * SCAN_CHUNK)
        tl = slice(lane0 + c * SCAN_CHUNK, lane0 + (c + 1) * SCAN_CHUNK)
        lr, li = lam_ref[0][:, tl], lam_ref[1][:, tl]

        def x_block(k):
            rows = slice(row0 + k * SUBLANES, row0 + (k + 1) * SUBLANES)
            return rows, bu_ref[b, rows, re], bu_ref[b, rows, im]

        _, pr, pi = x_block(0)
        for k in range(1, steps):
            _, xr, xi = x_block(k)
            pr, pi = _cmuladd(lr, li, pr, pi, xr, xi)
        hr, hi = enter_fn(pr, pi, tl)
        for k in range(steps):
            rows, xr, xi = x_block(k)
            bu_ref[b, rows, re] = hr
            bu_ref[b, rows, im] = hi
            hr, hi = _cmuladd(lr, li, hr, hi, xr, xi)
        leave_fn(hr, hi, tl)


def _build_folded_weights(b2_ref, ct_ref, lam_ref, bq_ref, wg_ref, group_ch, group_states):
    n_bundles = bq_ref.shape[0]
    n_ch = bq_ref.shape[1] // SSM_FOLD
    width = bq_ref.shape[2] // 2
    ch_shift = group_ch.bit_length() - 1
    st_shift = group_states.bit_length() - 1
    assert group_ch == 1 << ch_shift and group_states == 1 << st_shift
    groups = n_ch // group_ch

    def group_of(n_rows, axis, shift):
        return lax.shift_right_logical(lax.broadcasted_iota(jnp.int32, (n_rows, LANES), axis), shift)

    ch_rows = group_of(n_ch, 0, ch_shift)
    st_rows = group_of(width, 0, st_shift)
    st_lanes = group_of(n_ch, 1, st_shift)
    ch_lanes = group_of(width, 1, ch_shift)

    def fill_b(b, pos, src_pair):
        for part in range(2):
            for j in range(width // LANES):
                keep = ch_rows == st_lanes + j * (LANES // group_states)
                lanes = slice(part * width + j * LANES, part * width + (j + 1) * LANES)
                bq_ref[b, pos * n_ch:(pos + 1) * n_ch, lanes] = (
                    jnp.where(keep, src_pair[part], 0.0).astype(BF16))

    def fill_c(b, pos, src_pair):
        for part in range(2):
            for j in range(n_ch // LANES):
                keep = st_rows == ch_lanes + j * (LANES // group_ch)
                lanes = slice(pos * n_ch + j * LANES, pos * n_ch + (j + 1) * LANES)
                wg_ref[b, part * width:(part + 1) * width, lanes] = (
                    jnp.where(keep, src_pair[part], 0.0).astype(BF16))

    for b in range(n_bundles):
        ch = slice(b * n_ch, (b + 1) * n_ch)
        st = slice(b * width, (b + 1) * width)
        per_group = [(lam_ref[0][g:g + 1, :], lam_ref[1][g:g + 1, :])
                     for g in range(b * groups, (b + 1) * groups)]
        lam_ch = [jnp.concatenate([jnp.broadcast_to(l[part], (group_ch, LANES)) for l in per_group],
                                  axis=0) for part in range(2)]
        lam_st = [jnp.concatenate([jnp.broadcast_to(l[part], (LANES, LANES)).T[:group_states, :]
                                   for l in per_group], axis=0) for part in range(2)]
        b_re, b_im = b2_ref[0][ch, :], b2_ref[1][ch, :]
        fill_b(b, 0, (lam_ch[0] * b_re - lam_ch[1] * b_im, lam_ch[0] * b_im + lam_ch[1] * b_re))
        fill_b(b, 1, (b_re, b_im))
        c_re, c_nim = ct_ref[0][st, :], ct_ref[1][st, :]
        c1 = (c_re * lam_st[0] + c_nim * lam_st[1], c_nim * lam_st[0] - c_re * lam_st[1])
        c2 = (c1[0] * lam_st[0] + c1[1] * lam_st[1], c1[1] * lam_st[0] - c1[0] * lam_st[1])
        fill_c(b, 0, c1)
        fill_c(b, 1, (c_re, c_nim))
        b_big = bq_ref[b, n_ch:2 * n_ch, :]
        states = slice(0, 2 * width)
        k1 = jnp.dot(b_big, wg_ref[b, states, 0:n_ch], preferred_element_type=F32).astype(BF16)
        k0 = jnp.dot(b_big, wg_ref[b, states, n_ch:2 * n_ch],
                     preferred_element_type=F32).astype(BF16)
        u0 = slice(2 * width, 2 * width + n_ch)
        u1 = slice(2 * width + n_ch, 2 * width + 2 * n_ch)
        wg_ref[b, u0, 0:n_ch] = k0
        wg_ref[b, u0, n_ch:2 * n_ch] = k1
        wg_ref[b, u1, 0:n_ch] = jnp.zeros((n_ch, n_ch), BF16)
        wg_ref[b, u1, n_ch:2 * n_ch] = k0
        fill_c(b, 1, c2)


def _stage_weight_bf16(w_hbm, col0, dst_ref, stage_ref, sem_ref):
    k, n = dst_ref.shape
    rows = stage_ref.shape[1]
    n_chunks = k // rows

    def copy(c):
        return pltpu.make_async_copy(w_hbm.at[pl.ds(c * rows, rows), pl.ds(col0, n)],
                                     stage_ref.at[c % 2], sem_ref.at[c % 2])

    copy(0).start()
    for c in range(n_chunks):
        if c + 1 < n_chunks:
            copy(c + 1).start()
        copy(c).wait()
        dst_ref[c * rows:(c + 1) * rows, :] = stage_ref[c % 2].astype(BF16)


def _ssm_kernel(u_ref, sza_ref, h0re_ref, h0im_ref, lamre_ref, lamim_ref, tqre_ref, tqim_ref, d_ref,
                b2re_ref, b2im_ref, ctre_ref, ctim_ref, lgre_ref, lgim_ref, wglu_hbm, wouta_hbm,
                ya_ref, spre_ref, spim_ref, ssre_ref, ssim_ref,
                bu_ref, uslab_ref, yslab_ref, carry_ref, bq_ref, wg_ref,
                wglu_ref, wouta_ref, stage_narrow_ref, sem_ref,
                *, n_prompt_tiles, tiles_per_seq, sample_steps, group_ch, group_states):
    i = pl.program_id(0)
    n_bundles = bq_ref.shape[0]
    width = bq_ref.shape[2] // 2
    is_prompt = i < n_prompt_tiles
    t_in_seq = i % tiles_per_seq

    @pl.when(i == 0)
    def _():
        _stage_weight_bf16(wouta_hbm, 0, wouta_ref, bu_ref, sem_ref)
        _stage_weight_bf16(wglu_hbm, 0, wglu_ref, stage_narrow_ref, sem_ref)
        _build_folded_weights((b2re_ref, b2im_ref), (ctre_ref, ctim_ref), (lgre_ref, lgim_ref),
                              bq_ref, wg_ref, group_ch, group_states)

    @pl.when(jnp.logical_and(is_prompt, t_in_seq == 0))
    def _():
        carry_ref[...] = jnp.zeros_like(carry_ref)

    def run(n_groups, steps, enter_factory, leave_factory):
        folded = steps // SSM_FOLD
        d_ssm = u_ref.shape[1]
        blocks = _stepmajor_blocks(u_ref[...].astype(F32), uslab_ref, n_groups, steps)
        u_f = jnp.concatenate([jnp.concatenate(blocks[k:k + SSM_FOLD], axis=1)
                               for k in range(0, len(blocks), SSM_FOLD)], axis=0).astype(BF16)

        def u_bundle(b):
            return jnp.concatenate(
                [u_f[:, m * d_ssm + b * SSM_BUNDLE_CH:m * d_ssm + (b + 1) * SSM_BUNDLE_CH]
                 for m in range(SSM_FOLD)], axis=1)

        def b_matmul(b):
            bu_ref[b] = jnp.dot(u_bundle(b), bq_ref[b], preferred_element_type=F32)

        b_matmul(0)
        for b in range(n_bundles):
            if b + 1 < n_bundles:
                b_matmul(b + 1)
            for g in range(n_groups):
                _scan_group(bu_ref, b, g * SUBLANES * folded, folded, width, b * width,
                            (lamre_ref, lamim_ref), enter_factory(g), leave_factory(g))
            y_b = jnp.dot(jnp.concatenate([bu_ref[b].astype(BF16), u_bundle(b)], axis=1), wg_ref[b],
                          preferred_element_type=F32)
            _store_folded(y_b, yslab_ref, b * (SSM_BUNDLE_CH // LANES), steps)
        y = _load_natural(yslab_ref, n_groups, steps) + d_ref[...] * u_ref[...].astype(F32)
        g_act = _gelu_tanh(y)
        z = jnp.dot(g_act.astype(BF16), wglu_ref[...], preferred_element_type=F32)
        o = g_act * _sigmoid(z) * sza_ref[...].astype(F32)
        ya = jnp.dot(o.astype(BF16), wouta_ref[...], preferred_element_type=F32)
        ya_ref[...] = ya.astype(BF16)

    def prompt_enter(_g):
        def enter(er, ei, tl):
            xr, xi = er, ei
            for step, shift in enumerate((1, 2, 4)):
                xr, xi = _cmuladd(tqre_ref[step, :, tl], tqim_ref[step, :, tl],
                                  pltpu.roll(xr, shift, 0), pltpu.roll(xi, shift, 0), xr, xi)
            c0r = jnp.broadcast_to(carry_ref[0:1, tl], xr.shape)
            c0i = jnp.broadcast_to(carry_ref[1:2, tl], xi.shape)
            xr, xi = _cmuladd(tqre_ref[3, :, tl], tqim_ref[3, :, tl], c0r, c0i, xr, xi)
            carry_ref[0:1, tl] = xr[SUBLANES - 1:SUBLANES, :]
            carry_ref[1:2, tl] = xi[SUBLANES - 1:SUBLANES, :]
            first = lax.broadcasted_iota(jnp.int32, xr.shape, 0) == 0
            return (jnp.where(first, c0r, pltpu.roll(xr, 1, 0)),
                    jnp.where(first, c0i, pltpu.roll(xi, 1, 0)))
        return enter

    def prompt_leave(_g):
        return lambda hr, hi, tl: None

    def sample_enter(g):
        rows = slice(g * SUBLANES, (g + 1) * SUBLANES)
        return lambda er, ei, tl: (h0re_ref[rows, tl], h0im_ref[rows, tl])

    def sample_leave(g):
        rows = slice(g * SUBLANES, (g + 1) * SUBLANES)

        def leave(hr, hi, tl):
            ssre_ref[rows, tl] = hr
            ssim_ref[rows, tl] = hi
        return leave

    @pl.when(is_prompt)
    def _():
        run(1, TILE_M // SUBLANES, prompt_enter, prompt_leave)

    @pl.when(jnp.logical_not(is_prompt))
    def _():
        run(TILE_M // (SUBLANES * sample_steps), sample_steps, sample_enter, sample_leave)

    @pl.when(jnp.logical_and(is_prompt, t_in_seq == tiles_per_seq - 1))
    def _():
        row = pl.ds(i // tiles_per_seq, 1)
        spre_ref[row, :] = carry_ref[0:1, :]
        spim_ref[row, :] = carry_ref[1:2, :]


def _ssm_call(act, h0re, h0im, lam8, tabq, d, b2, ct, lamg, w_glu, w_out_a, n_prompt_rows,
              seq_len, sample_steps, group_ch, group_states):
    m = act.shape[0]
    d_ssm, d_model = w_out_a.shape
    n_state = lam8[0].shape[1]
    n_tiles = m // TILE_M
    n_p = n_prompt_rows // TILE_M
    n_batch = n_prompt_rows // seq_len
    seqs_per_tile = TILE_M // sample_steps
    n_sample_seq = h0re.shape[0]
    slab_rows = SUBLANES * _pitch(TILE_M // SUBLANES)
    n_bundles = d_ssm // SSM_BUNDLE_CH
    width = n_state // n_bundles
    assert 2 * width == d_model and n_bundles >= 2 and d_ssm % TILE_M == 0
    assert sample_steps % SSM_FOLD == 0
    kernel = functools.partial(_ssm_kernel, n_prompt_tiles=n_p,
                               tiles_per_seq=seq_len // TILE_M, sample_steps=sample_steps,
                               group_ch=group_ch, group_states=group_states)
    tile_map = lambda i: (i, 0)
    sample_map = lambda i: (jnp.maximum(i - n_p, 0), 0)
    hbm = pl.BlockSpec(memory_space=pl.ANY)
    return pl.pallas_call(
        kernel,
        grid=(n_tiles,),
        in_specs=[
            pl.BlockSpec((TILE_M, d_ssm), tile_map),
            pl.BlockSpec((TILE_M, d_ssm), lambda i: (i, 1)),
            pl.BlockSpec((seqs_per_tile, n_state), sample_map),
            pl.BlockSpec((seqs_per_tile, n_state), sample_map),
            _const_spec(lam8[0].shape), _const_spec(lam8[1].shape),
            _const_spec(tabq[0].shape), _const_spec(tabq[1].shape),
            _const_spec(d.shape),
            _const_spec(b2[0].shape), _const_spec(b2[1].shape),
            _const_spec(ct[0].shape), _const_spec(ct[1].shape),
            _const_spec(lamg[0].shape), _const_spec(lamg[1].shape),
            hbm, hbm,
        ],
        out_specs=[
            pl.BlockSpec((TILE_M, d_model), tile_map),
            pl.BlockSpec((n_batch, n_state), lambda i: (0, 0)),
            pl.BlockSpec((n_batch, n_state), lambda i: (0, 0)),
            pl.BlockSpec((seqs_per_tile, n_state), sample_map),
            pl.BlockSpec((seqs_per_tile, n_state), sample_map),
        ],
        out_shape=[
            jax.ShapeDtypeStruct((m, d_model), BF16),
            jax.ShapeDtypeStruct((n_batch, n_state), F32),
            jax.ShapeDtypeStruct((n_batch, n_state), F32),
            jax.ShapeDtypeStruct((n_sample_seq, n_state), F32),
            jax.ShapeDtypeStruct((n_sample_seq, n_state), F32),
        ],
        scratch_shapes=[
            pltpu.VMEM((n_bundles, TILE_M // SSM_FOLD, 2 * width), F32),
            pltpu.VMEM((d_ssm // LANES, slab_rows, LANES), F32),
            pltpu.VMEM((d_ssm // LANES, slab_rows, LANES), F32),
            pltpu.VMEM((2, n_state), F32),
            pltpu.VMEM((n_bundles, SSM_FOLD * SSM_BUNDLE_CH, 2 * width), BF16),
            pltpu.VMEM((n_bundles, 2 * width + SSM_FOLD * SSM_BUNDLE_CH, SSM_FOLD * SSM_BUNDLE_CH),
                       BF16),
            pltpu.VMEM((d_ssm, d_ssm), BF16),
            pltpu.VMEM((d_ssm, d_model), BF16),
            pltpu.VMEM((2, STAGE_ROWS // 2, d_ssm), F32),
            pltpu.SemaphoreType.DMA((2,)),
        ],
        compiler_params=pltpu.CompilerParams(
            dimension_semantics=("arbitrary",), vmem_limit_bytes=VMEM_LIMIT_BYTES),
        name="ssm_branch",
    )(act, act, h0re, h0im, *lam8, *tabq, d, *b2, *ct, *lamg, w_glu, w_out_a)


def _tail_kernel(q_ref, bz_ref, ya_ref, sga_ref, sgc_ref, xp_ref, xs_ref, hist_ref,
                 cw_ref, lng_ref, lnb_ref, woutc_hbm, wo_hbm,
                 yp_ref, ys_ref,
                 pad_ref, conv_ref, yc_ref, woutc_ref, wo_ref, stage_ref, sem_ref,
                 *, n_prompt_tiles, tiles_per_seq, alpha):
    i = pl.program_id(0)
    is_prompt = i < n_prompt_tiles

    @pl.when(i == 0)
    def _():
        _stage_weight_bf16(woutc_hbm, 0, woutc_ref, stage_ref, sem_ref)
        _stage_weight_bf16(wo_hbm, 0, wo_ref, stage_ref, sem_ref)

    @pl.when(jnp.logical_and(is_prompt, i % tiles_per_seq == 0))
    def _():
        pad_ref[0:CONV_PAD, :] = jnp.zeros((CONV_PAD, pad_ref.shape[1]), F32)

    q = q_ref[...]
    pad_ref[CONV_PAD:CONV_PAD + TILE_M, :] = q
    w0 = cw_ref[0:1, :]
    w1 = cw_ref[1:2, :]
    w2 = cw_ref[2:3, :]

    def finish(q1, q2, x_ref, y_ref):
        conv_ref[...] = w0 * q2 + w1 * q1 + w2 * q
        yc_in = conv_ref[...].astype(BF16) * bz_ref[...]
        yc_ref[...] = jnp.dot(yc_in, woutc_ref[...], preferred_element_type=F32)
        merged = sga_ref[...] * ya_ref[...] + sgc_ref[...] * yc_ref[...].astype(BF16)
        out = jnp.dot(merged, wo_ref[...], preferred_element_type=F32)
        r = alpha * x_ref[...] + out
        mu = jnp.mean(r, axis=-1, keepdims=True)
        rc = r - mu
        var = jnp.mean(rc * rc, axis=-1, keepdims=True)
        y_ref[...] = rc * lax.rsqrt(var + LN_EPS) * lng_ref[...] + lnb_ref[...]

    @pl.when(is_prompt)
    def _():
        q1 = pad_ref[CONV_PAD - 1:CONV_PAD - 1 + TILE_M, :]
        q2 = pad_ref[CONV_PAD - 2:CONV_PAD - 2 + TILE_M, :]
        finish(q1, q2, xp_ref, yp_ref)
        pad_ref[0:CONV_PAD, :] = pad_ref[TILE_M:TILE_M + CONV_PAD, :]

    @pl.when(jnp.logical_not(is_prompt))
    def _():
        t = lax.broadcasted_iota(jnp.int32, q.shape, 0) % SUBLANES
        d_conv = q.shape[1]

        def history(j):
            return jnp.concatenate(
                [jnp.broadcast_to(hist_ref[s:s + 1, j * d_conv:(j + 1) * d_conv],
                                  (SUBLANES, d_conv)) for s in range(TILE_M // SUBLANES)], axis=0)

        e0 = history(0)
        e1 = history(1)
        q1 = jnp.where(t == 0, e1, pad_ref[CONV_PAD - 1:CONV_PAD - 1 + TILE_M, :])
        q2 = jnp.where(t == 0, e0,
                       jnp.where(t == 1, e1, pad_ref[CONV_PAD - 2:CONV_PAD - 2 + TILE_M, :]))
        finish(q1, q2, xs_ref, ys_ref)


def _tail_call(q, act, ya, xp, xs, hist, conv_w, w_out_c, w_o, ln_g, ln_b, seq_len, alpha):
    m, d_conv = q.shape
    m_p, d_model = xp.shape
    m_s = xs.shape[0]
    n_p = m_p // TILE_M
    kernel = functools.partial(_tail_kernel, n_prompt_tiles=n_p,
                               tiles_per_seq=seq_len // TILE_M, alpha=alpha)
    prompt_map = lambda i: (jnp.minimum(i, n_p - 1), 0)
    sample_map = lambda i: (jnp.maximum(i - n_p, 0), 0)
    row_map = lambda i: (i, 0)
    hbm = pl.BlockSpec(memory_space=pl.ANY)
    assert w_out_c.shape[1] == w_o.shape[1] == d_model
    return pl.pallas_call(
        kernel,
        grid=(m // TILE_M,),
        in_specs=[
            pl.BlockSpec((TILE_M, d_conv), row_map),
            pl.BlockSpec((TILE_M, d_conv), lambda i: (i, 2)),
            pl.BlockSpec((TILE_M, d_model), row_map),
            pl.BlockSpec((TILE_M, d_model), lambda i: (i, 2)),
            pl.BlockSpec((TILE_M, d_model), lambda i: (i, 3)),
            pl.BlockSpec((TILE_M, d_model), prompt_map),
            pl.BlockSpec((TILE_M, d_model), sample_map),
            pl.BlockSpec((TILE_M // SUBLANES, hist.shape[1]), sample_map),
            _const_spec(conv_w.shape),
            _const_spec(ln_g.shape),
            _const_spec(ln_b.shape),
            hbm, hbm,
        ],
        out_specs=[
            pl.BlockSpec((TILE_M, d_model), prompt_map),
            pl.BlockSpec((TILE_M, d_model), sample_map),
        ],
        out_shape=[
            jax.ShapeDtypeStruct((m_p, d_model), F32),
            jax.ShapeDtypeStruct((m_s, d_model), F32),
        ],
        scratch_shapes=[
            pltpu.VMEM((TILE_M + CONV_PAD, d_conv), F32),
            pltpu.VMEM((TILE_M, d_conv), F32),
            pltpu.VMEM((TILE_M, d_model), F32),
            pltpu.VMEM(w_out_c.shape, BF16),
            pltpu.VMEM(w_o.shape, BF16),
            pltpu.VMEM((2, STAGE_ROWS, d_model), F32),
            pltpu.SemaphoreType.DMA((2,)),
        ],
        compiler_params=pltpu.CompilerParams(
            dimension_semantics=("arbitrary",), vmem_limit_bytes=VMEM_LIMIT_BYTES),
        name="tail",
    )(q, act, ya, act, act, xp, xs, hist, conv_w, ln_g, ln_b, w_out_c, w_o)


def _ssm_params(a_re, a_im, log_dt, b_re, b_im, c_re, c_im, prompt_steps):
    g, p, gc = b_re.shape
    dt = jnp.exp(log_dt)[:, None]
    mag = jnp.exp(a_re * dt)
    ang = a_im * dt
    lam_re = mag * jnp.cos(ang)
    lam_im = mag * jnp.sin(ang)
    den = a_re * a_re + a_im * a_im
    q_re = ((lam_re - 1.0) * a_re + lam_im * a_im) / den
    q_im = (lam_im * a_re - (lam_re - 1.0) * a_im) / den
    bb_re = q_re[..., None] * b_re - q_im[..., None] * b_im
    bb_im = q_re[..., None] * b_im + q_im[..., None] * b_re

    def channel_rows(bb):
        rows = bb.transpose(0, 2, 1).reshape(g * gc, p)
        return jnp.tile(rows, (1, LANES // p))

    def state_rows(cc):
        rows = cc.transpose(0, 2, 1).reshape(g * p, gc)
        return jnp.tile(rows, (1, LANES // gc))

    b2 = (channel_rows(bb_re), channel_rows(bb_im))
    ct = (state_rows(c_re), -state_rows(c_im))
    lamg = (jnp.tile(lam_re, (1, LANES // p)), jnp.tile(lam_im, (1, LANES // p)))

    lam2_re = lam_re * lam_re - lam_im * lam_im
    lam2_im = 2.0 * lam_re * lam_im
    lam8 = (jnp.broadcast_to(lam2_re.reshape(1, -1), (SUBLANES, g * p)),
            jnp.broadcast_to(lam2_im.reshape(1, -1), (SUBLANES, g * p)))

    row = jnp.arange(SUBLANES, dtype=F32)[None, :, None]
    shift = jnp.array([1.0, 2.0, 4.0], F32)[:, None, None]
    exponent = jnp.concatenate([jnp.broadcast_to(shift, (3, SUBLANES, 1)), row + 1.0]) * prompt_steps
    keep = jnp.concatenate([row >= shift, jnp.ones((1, SUBLANES, 1), bool)])
    mag_e = jnp.where(keep, jnp.exp(exponent * (a_re * dt).reshape(1, 1, -1)), 0.0)
    ang_e = exponent * ang.reshape(1, 1, -1)
    tabq = (mag_e * jnp.cos(ang_e), mag_e * jnp.sin(ang_e))
    return b2, ct, lamg, lam8, tabq


def kernel(x_prompt, x_sample, state_ssm_re, state_ssm_im, state_conv, w_in, ssm_a_re, ssm_a_im, ssm_log_dt, ssm_b_re, ssm_b_im, ssm_c_re, ssm_c_im, ssm_d, w_glu, w_out_a, conv_w, w_out_c, w_o, ln_g, ln_b):
    depth = w_in.shape[0]
    assert depth == 1, "single-layer trunk"
    batch, seq, d_model = x_prompt.shape
    dec_batch, dec_seq, _ = x_sample.shape
    assert dec_seq == SUBLANES and seq % TILE_M == 0 and (dec_batch * dec_seq) % TILE_M == 0
    g, p, gc = ssm_b_re.shape[1:]
    d_ssm = g * gc
    d_conv = conv_w.shape[2]
    n_state = g * p
    alpha = (2 * depth) ** 0.25

    xp = x_prompt.reshape(batch * seq, d_model)
    xs = x_sample.reshape(dec_batch * dec_seq, d_model)
    m_p = xp.shape[0]
    w = w_in[0]

    assert d_ssm == d_conv == PROJ_COLS and d_model == 2 * PROJ_COLS
    act, q = _proj_call(xp, xs, w)

    ssm_raw = (ssm_a_re[0], ssm_a_im[0], ssm_log_dt[0], ssm_b_re[0], ssm_b_im[0], ssm_c_re[0],
               ssm_c_im[0])
    act, q, ssm_raw, state_ssm_re, state_ssm_im, state_conv = lax.optimization_barrier(
        (act, q, ssm_raw, state_ssm_re, state_ssm_im, state_conv))

    b2, ct, lamg, lam8, tabq = _ssm_params(*ssm_raw, TILE_M // SUBLANES)
    h0re = state_ssm_re[0].reshape(dec_batch, n_state)
    h0im = state_ssm_im[0].reshape(dec_batch, n_state)
    ya, spre, spim, ssre, ssim = _ssm_call(
        act, h0re, h0im, lam8, tabq, ssm_d[0][None, :], b2, ct, lamg, w_glu[0], w_out_a[0],
        m_p, seq, dec_seq, gc, p)

    hist = state_conv[0].reshape(dec_batch, (state_conv.shape[2]) * d_conv)
    yp, ys = _tail_call(q, act, ya, xp, xs, hist, conv_w[0], w_out_c[0], w_o[0],
                        ln_g[0][None, :], ln_b[0][None, :], seq, alpha)

    q8 = q.reshape(-1, SUBLANES, d_conv)
    q_p = q8[seq // SUBLANES - 1:m_p // SUBLANES:seq // SUBLANES, SUBLANES - 2:, :]
    q_s = q8[m_p // SUBLANES:, dec_seq - 2:, :]
    return (yp.reshape(batch, seq, d_model),
            ys.reshape(dec_batch, dec_seq, d_model),
            spre.reshape(1, batch, g, p),
            spim.reshape(1, batch, g, p),
            q_p[None],
            ssre.reshape(1, dec_batch, g, p),
            ssim.reshape(1, dec_batch, g, p),
            q_s[None])
```
